```python
import jax, jax.numpy as jnp
from jax import lax
import numpy as np

D_MODEL = 1024
BATCH = 8
SEQ = 4096
DEPTH = 2

CHUNK = 64
N_MIXERS = 2
DEEPNORM_ALPHA = (2 * DEPTH) ** 0.25
DEEPNORM_BETA = (8 * DEPTH) ** -0.25
LN_EPS = 1e-5
POOL_WINDOWS = (2, 4, 8, 16)
POOL_GROUPS = len(POOL_WINDOWS)
POOL_GROUP_DIM = D_MODEL // POOL_GROUPS
SSM_EXPAND = 2
SSM_D_INNER = SSM_EXPAND * D_MODEL
SSM_HEAD_DIM = 64
SSM_N_HEADS = SSM_D_INNER // SSM_HEAD_DIM
SSM_N_GROUPS = 8
SSM_HEADS_PER_GROUP = SSM_N_HEADS // SSM_N_GROUPS
SSM_D_STATE = 128
SSM_CONV = 4
SSM_CONV_DIM = SSM_D_INNER + 2 * SSM_N_GROUPS * SSM_D_STATE
SSM_IN_DIM = SSM_D_INNER + SSM_CONV_DIM + SSM_N_HEADS
SSM_NORM_EPS = 1e-5
MEM_LEN = 256
XA_HEADS = 4
XA_HEAD_DIM = D_MODEL // XA_HEADS
N_EXPERTS = 32
TOP_K = 4
D_FF = D_MODEL
SWIGLU_LIMIT = 7.0
SWIGLU_ALPHA = 1.702
MOE_BLOCK = 128

N_POOL_LAYERS = (DEPTH + 1) // 2
N_SSM_LAYERS = DEPTH // 2

kernel_name = "hybrid_pool_ssd_memxattn_moe_deepnorm"

F32 = jnp.float32


def layer_norm(x, g, b):
    xf = x.astype(F32)
    mu = jnp.mean(xf, -1, keepdims=True)
    var = jnp.mean(jnp.square(xf - mu), -1, keepdims=True)
    y = (xf - mu) * lax.rsqrt(var + LN_EPS) * g.astype(F32) + b.astype(F32)
    return y.astype(x.dtype)


def trailing_mean_minus_self(u, w):
    S = u.shape[1]
    uf = u.astype(F32)
    cs = jnp.pad(jnp.cumsum(uf, axis=1), ((0, 0), (1, 0), (0, 0)))
    upper = cs[:, 1:]
    lower = jnp.pad(cs[:, :S + 1 - w], ((0, 0), (w - 1, 0), (0, 0)))
    count = jnp.minimum(jnp.arange(S) + 1, w).astype(F32)[None, :, None]
    return ((upper - lower) / count - uf).astype(u.dtype)


def pool_mixer(x, w_in, w_grp, scale, w_out):
    Bsz, S, _ = x.shape
    u = (x @ w_in).reshape(Bsz, S, POOL_GROUPS, POOL_GROUP_DIM)
    m = jnp.stack([trailing_mean_minus_self(u[:, :, g], w) for g, w in enumerate(POOL_WINDOWS)], axis=2)
    y = jnp.einsum('bsgc,gcd->bsgd', m, w_grp).reshape(Bsz, S, D_MODEL) * scale
    return y @ w_out


def causal_depthwise_conv(u, w, b):
    K, C = w.shape
    y = lax.conv_general_dilated(u, w[:, None, :], window_strides=(1,), padding=[(K - 1, 0)],
                                 dimension_numbers=('NWC', 'WIO', 'NWC'), feature_group_count=C)
    return y + b


def ssd_scan(xh, dt, A, Bm, Cm):
    Bsz, S, H, P = xh.shape
    G, R, N, Q = SSM_N_GROUPS, SSM_HEADS_PER_GROUP, SSM_D_STATE, CHUNK
    nc = S // Q
    xd = (xh.astype(F32) * dt[..., None]).reshape(Bsz, nc, Q, G, R, P)
    a_cs = jnp.cumsum((dt * A).reshape(Bsz, nc, Q, G, R), axis=2)
    Bc = Bm.astype(F32).reshape(Bsz, nc, Q, G, N)
    Cc = Cm.astype(F32).reshape(Bsz, nc, Q, G, N)
    seg = a_cs[:, :, :, None] - a_cs[:, :, None, :]
    causal = jnp.tril(jnp.ones((Q, Q), bool))[:, :, None, None]
    L = jnp.exp(jnp.where(causal, seg, -jnp.inf))
    cb = jnp.einsum('bcign,bcjgn->bcijg', Cc, Bc)
    y_diag = jnp.einsum('bcijgr,bcjgrp->bcigrp', cb[..., None] * L, xd)
    decay_to_end = jnp.exp(a_cs[:, :, -1:] - a_cs)
    states = jnp.einsum('bcqgn,bcqgrp->bcgrpn', Bc, xd * decay_to_end[..., None])
    chunk_decay = jnp.exp(a_cs[:, :, -1])

    def step(h, inp):
        s_c, d_c = inp
        return h * d_c[..., None, None] + s_c, h

    h0 = jnp.zeros((Bsz, G, R, P, N), F32)
    _, prev = lax.scan(step, h0, (jnp.moveaxis(states, 1, 0), jnp.moveaxis(chunk_decay, 1, 0)))
    prev = jnp.moveaxis(prev, 0, 1)
    y_off = jnp.einsum('bcqgn,bcgrpn->bcqgrp', Cc, prev) * jnp.exp(a_cs)[..., None]
    return (y_diag + y_off).reshape(Bsz, S, H, P)


def ssd_mixer(x, w_in, conv_w, conv_b, dt_bias, a_log, d_skip, norm_g, w_out):
    Bsz, S, _ = x.shape
    zxbcdt = x @ w_in
    z = zxbcdt[..., :SSM_D_INNER]
    xbc = zxbcdt[..., SSM_D_INNER:SSM_D_INNER + SSM_CONV_DIM]
    dt_raw = zxbcdt[..., SSM_D_INNER + SSM_CONV_DIM:]
    xbc = jax.nn.silu(causal_depthwise_conv(xbc, conv_w, conv_b))
    GN = SSM_N_GROUPS * SSM_D_STATE
    xs = xbc[..., :SSM_D_INNER]
    Bm = xbc[..., SSM_D_INNER:SSM_D_INNER + GN].reshape(Bsz, S, SSM_N_GROUPS, SSM_D_STATE)
    Cm = xbc[..., SSM_D_INNER + GN:].reshape(Bsz, S, SSM_N_GROUPS, SSM_D_STATE)
    dt = jax.nn.softplus(dt_raw.astype(F32) + dt_bias.astype(F32))
    A = -jnp.exp(a_log.astype(F32))
    xh = xs.reshape(Bsz, S, SSM_N_HEADS, SSM_HEAD_DIM)
    y = ssd_scan(xh, dt, A, Bm, Cm) + d_skip.astype(F32)[:, None] * xh.astype(F32)
    y = y.reshape(Bsz, S, SSM_D_INNER) * jax.nn.silu(z.astype(F32))
    yg = y.reshape(Bsz, S, SSM_N_GROUPS, SSM_D_INNER // SSM_N_GROUPS)
    yg = yg * lax.rsqrt(jnp.mean(jnp.square(yg), -1, keepdims=True) + SSM_NORM_EPS)
    y = (yg.reshape(Bsz, S, SSM_D_INNER) * norm_g.astype(F32)).astype(x.dtype)
    return y @ w_out


def memory_cross_attention(x, mem, wq, wk, wv, wo):
    Bsz, S, _ = x.shape
    M = mem.shape[1]
    q = (x @ wq).reshape(Bsz, S, XA_HEADS, XA_HEAD_DIM).astype(F32)
    k = (mem @ wk).reshape(Bsz, M, XA_HEADS, XA_HEAD_DIM).astype(F32)
    v = (mem @ wv).reshape(Bsz, M, XA_HEADS, XA_HEAD_DIM)
    s = jnp.einsum('bshd,bmhd->bhsm', q, k) * (XA_HEAD_DIM ** -0.5)
    p = jax.nn.softmax(s, axis=-1).astype(x.dtype)
    o = jnp.einsum('bhsm,bmhd->bshd', p, v).reshape(Bsz, S, D_MODEL)
    return o @ wo


def moe_ffn(x, w_router, b_router, w_gate_up, b_gate_up, w_down, b_down):
    Bsz, S, Dm = x.shape
    xt = x.reshape(-1, Dm)
    T = xt.shape[0]
    logits = (xt @ w_router + b_router).astype(F32)
    top_vals, top_idx = lax.top_k(logits, TOP_K)
    gates = jax.nn.softmax(top_vals, axis=-1)
    n_assign = T * TOP_K
    flat_e = top_idx.reshape(-1).astype(jnp.int32)
    flat_tok = jnp.repeat(jnp.arange(T, dtype=jnp.int32), TOP_K)
    flat_g = gates.reshape(-1)
    order = jnp.argsort(flat_e)
    sorted_e, sorted_tok, sorted_g = flat_e[order], flat_tok[order], flat_g[order]
    counts = jnp.zeros((N_EXPERTS,), jnp.int32).at[flat_e].add(1)
    padded = (counts + MOE_BLOCK - 1) // MOE_BLOCK * MOE_BLOCK
    starts = jnp.cumsum(counts) - counts
    padded_ends = jnp.cumsum(padded)
    padded_starts = padded_ends - padded
    dest = padded_starts[sorted_e] + (jnp.arange(n_assign, dtype=jnp.int32) - starts[sorted_e])
    n_blocks = -(-n_assign // MOE_BLOCK) + N_EXPERTS
    n_rows = n_blocks * MOE_BLOCK
    row_tok = jnp.zeros((n_rows,), jnp.int32).at[dest].set(sorted_tok)
    row_gate = jnp.zeros((n_rows,), F32).at[dest].set(sorted_g)
    block_start = jnp.arange(n_blocks, dtype=jnp.int32) * MOE_BLOCK
    block_e = jnp.minimum(jnp.searchsorted(padded_ends, block_start, side='right'), N_EXPERTS - 1)

    def expert_block(args):
        e, tok = args
        h = xt[tok] @ w_gate_up[e] + b_gate_up[e]
        gate = jnp.minimum(h[:, 0::2], SWIGLU_LIMIT)
        up = jnp.clip(h[:, 1::2], -SWIGLU_LIMIT, SWIGLU_LIMIT)
        act = (up + 1) * (gate * jax.nn.sigmoid(SWIGLU_ALPHA * gate))
        return act @ w_down[e] + b_down[e]

    y_rows = lax.map(expert_block, (block_e, row_tok.reshape(n_blocks, MOE_BLOCK)))
    y_rows = y_rows.reshape(n_rows, Dm) * row_gate[:, None].astype(x.dtype)
    y = jnp.zeros_like(xt).at[row_tok].add(y_rows)
    return y.reshape(Bsz, S, Dm)


def setup_inputs(seed: int = 0) -> dict:
    key = jax.random.key(seed)
    ks = jax.random.split(key, 32)
    nrm = lambda k, shape, s: jax.random.normal(k, shape, F32) * s
    beta = DEEPNORM_BETA
    NA, NB = N_POOL_LAYERS, N_SSM_LAYERS
    dt0 = jnp.exp(jax.random.uniform(ks[8], (NB, SSM_N_HEADS), F32, np.log(1e-3), np.log(1e-1)))
    return {
        "x": nrm(ks[0], (BATCH, SEQ, D_MODEL), 1.0),
        "mem": nrm(ks[1], (BATCH, MEM_LEN, D_MODEL), 1.0),
        "pool_w_in": nrm(ks[2], (NA, D_MODEL, D_MODEL), D_MODEL ** -0.5),
        "pool_w_grp": nrm(ks[3], (NA, POOL_GROUPS, POOL_GROUP_DIM, POOL_GROUP_DIM), POOL_GROUP_DIM ** -0.5),
        "pool_scale": 1.0 + nrm(ks[4], (NA, D_MODEL), 0.1),
        "pool_w_out": nrm(ks[5], (NA, D_MODEL, D_MODEL), beta * D_MODEL ** -0.5),
        "ssm_w_in": nrm(ks[6], (NB, D_MODEL, SSM_IN_DIM), D_MODEL ** -0.5),
        "ssm_conv_w": nrm(ks[7], (NB, SSM_CONV, SSM_CONV_DIM), SSM_CONV ** -0.5),
        "ssm_conv_b": nrm(ks[9], (NB, SSM_CONV_DIM), 0.02),
        "ssm_dt_bias": dt0 + jnp.log(-jnp.expm1(-dt0)),
        "ssm_a_log": jnp.log(jax.random.uniform(ks[10], (NB, SSM_N_HEADS), F32, 1.0, 16.0)),
        "ssm_d": 1.0 + nrm(ks[11], (NB, SSM_N_HEADS), 0.1),
        "ssm_norm_g": 1.0 + nrm(ks[12], (NB, SSM_D_INNER), 0.1),
        "ssm_w_out": nrm(ks[13], (NB, SSM_D_INNER, D_MODEL), beta * SSM_D_INNER ** -0.5),
        "xa_wq": nrm(ks[14], (DEPTH, D_MODEL, D_MODEL), D_MODEL ** -0.5),
        "xa_wk": nrm(ks[15], (DEPTH, D_MODEL, D_MODEL), D_MODEL ** -0.5),
        "xa_wv": nrm(ks[16], (DEPTH, D_MODEL, D_MODEL), beta * D_MODEL ** -0.5),
        "xa_wo": nrm(ks[17], (DEPTH, D_MODEL, D_MODEL), beta * D_MODEL ** -0.5),
        "moe_w_router": nrm(ks[18], (DEPTH, D_MODEL, N_EXPERTS), D_MODEL ** -0.5),
        "moe_b_router": nrm(ks[19], (DEPTH, N_EXPERTS), 0.01),
        "moe_w_gate_up": nrm(ks[20], (DEPTH, N_EXPERTS, D_MODEL, 2 * D_FF), beta * D_MODEL ** -0.5),
        "moe_b_gate_up": nrm(ks[21], (DEPTH, N_EXPERTS, 2 * D_FF), 0.02),
        "moe_w_down": nrm(ks[22], (DEPTH, N_EXPERTS, D_FF, D_MODEL), beta * D_FF ** -0.5),
        "moe_b_down": nrm(ks[23], (DEPTH, N_EXPERTS, D_MODEL), 0.02),
        "ln_mix_g": 1.0 + nrm(ks[24], (DEPTH, D_MODEL), 0.1),
        "ln_mix_b": nrm(ks[25], (DEPTH, D_MODEL), 0.02),
        "ln_xa_g": 1.0 + nrm(ks[26], (DEPTH, D_MODEL), 0.1),
        "ln_xa_b": nrm(ks[27], (DEPTH, D_MODEL), 0.02),
        "ln_ffn_g": 1.0 + nrm(ks[28], (DEPTH, D_MODEL), 0.1),
        "ln_ffn_b": nrm(ks[29], (DEPTH, D_MODEL), 0.02),
    }


def reference(x, mem, pool_w_in, pool_w_grp, pool_scale, pool_w_out,
              ssm_w_in, ssm_conv_w, ssm_conv_b, ssm_dt_bias, ssm_a_log, ssm_d, ssm_norm_g, ssm_w_out,
              xa_wq, xa_wk, xa_wv, xa_wo,
              moe_w_router, moe_b_router, moe_w_gate_up, moe_b_gate_up, moe_w_down, moe_b_down,
              ln_mix_g, ln_mix_b, ln_xa_g, ln_xa_b, ln_ffn_g, ln_ffn_b):
    h = x
    for i in range(DEPTH):
        j = i // N_MIXERS
        if i % N_MIXERS == 0:
            mix = pool_mixer(h, pool_w_in[j], pool_w_grp[j], pool_scale[j], pool_w_out[j])
        else:
            mix = ssd_mixer(h, ssm_w_in[j], ssm_conv_w[j], ssm_conv_b[j], ssm_dt_bias[j],
                            ssm_a_log[j], ssm_d[j], ssm_norm_g[j], ssm_w_out[j])
        h = layer_norm(DEEPNORM_ALPHA * h + mix, ln_mix_g[i], ln_mix_b[i])
        xa = memory_cross_attention(h, mem, xa_wq[i], xa_wk[i], xa_wv[i], xa_wo[i])
        h = layer_norm(DEEPNORM_ALPHA * h + xa, ln_xa_g[i], ln_xa_b[i])
        ff = moe_ffn(h, moe_w_router[i], moe_b_router[i], moe_w_gate_up[i], moe_b_gate_up[i],
                     moe_w_down[i], moe_b_down[i])
        h = layer_norm(DEEPNORM_ALPHA * h + ff, ln_ffn_g[i], ln_ffn_b[i])
    return h
```

```python
import functools

import jax
import jax.numpy as jnp
from jax import lax
from jax.experimental import pallas as pl
from jax.experimental.pallas import tpu as pltpu

F32 = jnp.float32
BF16 = jnp.bfloat16

DEPTH = 2
DEEPNORM_ALPHA = (2 * DEPTH) ** 0.25
LN_EPS = 1e-5
POOL_WINDOWS = (2, 4, 8, 16)
POOL_HALO = 16
SSM_HEAD_DIM = 64
SSM_N_GROUPS = 8
SSM_HEADS_PER_GROUP = 4
SSM_D_STATE = 128
SSM_CONV = 4
CONV_HALO = 8
SSM_NORM_EPS = 1e-5
XA_HEADS = 4
N_EXPERTS = 32
TOP_K = 4
SWIGLU_LIMIT = 7.0
SWIGLU_ALPHA = 1.702

LANES = 128
VMEM_LIMIT_BYTES = 56 * 1024 * 1024

POOL_TILE = 512
XATTN_TILE = 512
SSD_TILE = 256
SSD_CHUNK = 256
MOE_TILE = 512
COMBINE_TILE = 512

NEG_BIG = -1e30


def _layer_norm(v, g, b):
    mu = jnp.mean(v, -1, keepdims=True)
    d = v - mu
    var = jnp.mean(d * d, -1, keepdims=True)
    return d * lax.rsqrt(var + LN_EPS) * g + b


def _dot(a, b):
    return jnp.dot(a, b, preferred_element_type=F32)


def _dot_nt(a, b):
    return lax.dot_general(a, b, (((1,), (1,)), ((), ())), preferred_element_type=F32)


def _dot_tn(a, b):
    return lax.dot_general(a, b, (((0,), (0,)), ((), ())), preferred_element_type=F32)


def _const_spec(shape):
    nd = len(shape)
    return pl.BlockSpec(shape, lambda *_: (0,) * nd, pipeline_mode=pl.Buffered(1))


def _params(n_axes):
    return pltpu.CompilerParams(dimension_semantics=("arbitrary",) * n_axes,
                                vmem_limit_bytes=VMEM_LIMIT_BYTES)


def _pool_kernel(x_ref, win_ref, wgrp_ref, scale_ref, wout_ref, g_ref, b_ref, o_ref, ext_ref, *, ts, gd):
    s = pl.program_id(1)

    @pl.when(s == 0)
    def _():
        ext_ref[0:POOL_HALO, :] = jnp.zeros((POOL_HALO, ext_ref.shape[1]), F32)

    x = x_ref[...]
    ext_ref[POOL_HALO:, :] = _dot(x.astype(BF16), win_ref[...])
    pos = lax.broadcasted_iota(jnp.int32, (ts, 1), 0) + s * ts
    mix = None
    for g, w in enumerate(POOL_WINDOWS):
        cols = slice(g * gd, (g + 1) * gd)
        e = ext_ref[:, cols]
        acc = e
        sh = 1
        while sh < w:
            acc = acc + pltpu.roll(acc, sh, axis=0)
            sh *= 2
        cnt = jnp.minimum(pos + 1, w).astype(F32)
        m = acc[POOL_HALO:, :] / cnt - e[POOL_HALO:, :]
        yg = _dot(m.astype(BF16), wgrp_ref[g]) * scale_ref[:, cols]
        part = _dot(yg.astype(BF16), wout_ref[cols, :])
        mix = part if mix is None else mix + part
    ext_ref[0:POOL_HALO, :] = ext_ref[ts:ts + POOL_HALO, :]
    o_ref[...] = _layer_norm(DEEPNORM_ALPHA * x + mix, g_ref[...], b_ref[...])


def _pool_layer(h, w_in, w_grp, scale, w_out, ln_g, ln_b):
    bsz, seq, d = h.shape
    ts = POOL_TILE
    gd = d // len(POOL_WINDOWS)
    tile = pl.BlockSpec((None, ts, d), lambda b, s: (b, s, 0))
    return pl.pallas_call(
        functools.partial(_pool_kernel, ts=ts, gd=gd),
        out_shape=jax.ShapeDtypeStruct((bsz, seq, d), F32),
        grid=(bsz, seq // ts),
        in_specs=[tile, _const_spec((d, d)), _const_spec((len(POOL_WINDOWS), gd, gd)), _const_spec((1, d)),
                  _const_spec((d, d)), _const_spec((1, d)), _const_spec((1, d))],
        out_specs=tile,
        scratch_shapes=[pltpu.VMEM((POOL_HALO + ts, d), F32)],
        compiler_params=_params(2),
        name="pool_mixer_ln",
    )(h, w_in.astype(BF16), w_grp.astype(BF16), scale.reshape(1, d), w_out.astype(BF16),
      ln_g.reshape(1, d), ln_b.reshape(1, d))


def _sigmoid(v):
    return 1.0 / (1.0 + jnp.exp(-v))


def _ssd_kernel(x_ref, wz_ref, wx_ref, wb_ref, wc_ref, wdt_ref, convw_ref, convb_ref, dtb_ref, alog_ref,
                dexp_ref, normg_ref, expand_ref, wout_ref, g_ref, b_ref, o_ref,
                xbc_scr, z_scr, actx_scr, bmat_scr, cmat_scr, y_scr, state_scr, *, ts, q, d_inner, gn):
    s = pl.program_id(1)
    conv_dim = d_inner + 2 * gn
    gw = d_inner // SSM_N_GROUPS

    @pl.when(s == 0)
    def _():
        xbc_scr[0:CONV_HALO, :] = jnp.zeros((CONV_HALO, conv_dim), F32)
        state_scr[...] = jnp.zeros(state_scr.shape, F32)

    x = x_ref[...]
    xb = x.astype(BF16)
    z_scr[...] = _dot(xb, wz_ref[...])
    xbc_scr[CONV_HALO:, 0:d_inner] = _dot(xb, wx_ref[...])
    xbc_scr[CONV_HALO:, d_inner:d_inner + gn] = _dot(xb, wb_ref[...])
    xbc_scr[CONV_HALO:, d_inner + gn:] = _dot(xb, wc_ref[...])
    dt_raw = _dot(xb, wdt_ref[...]) + dtb_ref[...]
    dtv = jnp.maximum(dt_raw, 0.0) + jnp.log1p(jnp.exp(-jnp.abs(dt_raw)))
    a_all = dtv * (-jnp.exp(alog_ref[...]))

    strip = 512
    for c in range(0, conv_dim, strip):
        cs = slice(c, c + strip)
        acc = convb_ref[:, cs]
        for k in range(SSM_CONV):
            r0 = CONV_HALO - (SSM_CONV - 1) + k
            acc = acc + convw_ref[k:k + 1, cs] * xbc_scr[r0:r0 + ts, cs]
        act = acc * _sigmoid(acc)
        if c < d_inner:
            actx_scr[:, cs] = act
        elif c < d_inner + gn:
            bmat_scr[:, c - d_inner:c - d_inner + strip] = act.astype(BF16)
        else:
            cmat_scr[:, c - d_inner - gn:c - d_inner - gn + strip] = act.astype(BF16)
    xbc_scr[0:CONV_HALO, :] = xbc_scr[ts:ts + CONV_HALO, :]

    expand = expand_ref[...]
    row_i = lax.broadcasted_iota(jnp.int32, (q, LANES), 0)
    causal = lax.broadcasted_iota(jnp.int32, (q, q), 0) >= lax.broadcasted_iota(jnp.int32, (q, q), 1)
    head_of_lane = lax.broadcasted_iota(jnp.int32, (q, gw), 1) // SSM_HEAD_DIM

    for c in range(ts // q):
        rows = slice(c * q, (c + 1) * q)
        acs = a_all[rows, :]
        sh = 1
        while sh < q:
            acs = acs + jnp.where(row_i >= sh, pltpu.roll(acs, sh, axis=0), 0.0)
            sh *= 2
        acs_t = acs.T
        a_last = acs[q - 1:q, :]
        e_in = _dot(jnp.exp(acs).astype(BF16), expand)
        dec = _dot(jnp.exp(a_last - acs).astype(BF16), expand)
        dtx = _dot(dtv[rows, :].astype(BF16), expand)
        cd = jnp.broadcast_to(jnp.exp(a_last), (8, LANES))
        cd_hi = cd.astype(BF16)
        cd_lo = (cd - cd_hi.astype(F32)).astype(BF16)
        cdx = (_dot(cd_hi, expand) + _dot(cd_lo, expand))[0:1, :]
        xd = actx_scr[rows, :] * dtx
        xdd = xd * dec
        for g in range(SSM_N_GROUPS):
            gc = slice(g * gw, (g + 1) * gw)
            nc = slice(g * SSM_D_STATE, (g + 1) * SSM_D_STATE)
            bg = bmat_scr[rows, nc]
            cg = cmat_scr[rows, nc]
            cb = _dot_nt(cg, bg)
            xg = xd[:, gc]
            yg = _dot(cg, state_scr[g].astype(BF16)) * e_in[:, gc]
            for r in range(SSM_HEADS_PER_GROUP):
                hd = g * SSM_HEADS_PER_GROUP + r
                seg = acs[:, hd:hd + 1] - acs_t[hd:hd + 1, :]
                lmat = jnp.exp(jnp.where(causal, seg, NEG_BIG))
                mh = (cb * lmat).astype(BF16)
                xm = jnp.where(head_of_lane == r, xg, 0.0).astype(BF16)
                yg = yg + _dot(mh, xm)
            y_scr[rows, gc] = yg
            state_scr[g] = state_scr[g] * cdx[:, gc] + _dot_tn(bg, xdd[:, gc].astype(BF16))

    mix = None
    for g in range(SSM_N_GROUPS):
        gc = slice(g * gw, (g + 1) * gw)
        zz = z_scr[:, gc]
        yv = (y_scr[:, gc] + dexp_ref[:, gc] * actx_scr[:, gc]) * (zz * _sigmoid(zz))
        yv = yv * lax.rsqrt(jnp.mean(yv * yv, -1, keepdims=True) + SSM_NORM_EPS) * normg_ref[:, gc]
        part = _dot(yv.astype(BF16), wout_ref[gc, :])
        mix = part if mix is None else mix + part
    o_ref[...] = _layer_norm(DEEPNORM_ALPHA * x + mix, g_ref[...], b_ref[...])


def _ssd_layer(h, w_in, conv_w, conv_b, dt_bias, a_log, d_skip, norm_g, w_out, ln_g, ln_b):
    bsz, seq, d = h.shape
    n_heads = a_log.shape[0]
    d_inner = n_heads * SSM_HEAD_DIM
    gn = SSM_N_GROUPS * SSM_D_STATE
    conv_dim = d_inner + 2 * gn
    ts, q = SSD_TILE, SSD_CHUNK
    w_in = w_in.astype(BF16)
    wz = w_in[:, :d_inner]
    wx = w_in[:, d_inner:2 * d_inner]
    wb = w_in[:, 2 * d_inner:2 * d_inner + gn]
    wc = w_in[:, 2 * d_inner + gn:2 * d_inner + 2 * gn]
    pad = LANES - n_heads
    wdt = jnp.pad(w_in[:, d_inner + conv_dim:], ((0, 0), (0, pad)))
    dtb = jnp.pad(dt_bias.astype(F32), (0, pad)).reshape(1, LANES)
    alog = jnp.pad(a_log.astype(F32), (0, pad)).reshape(1, LANES)
    dexp = jnp.repeat(d_skip.astype(F32), SSM_HEAD_DIM).reshape(1, d_inner)
    expand = (jnp.arange(LANES)[:, None] == (jnp.arange(d_inner)[None, :] // SSM_HEAD_DIM)).astype(BF16)
    tile = pl.BlockSpec((None, ts, d), lambda b, s: (b, s, 0))
    return pl.pallas_call(
        functools.partial(_ssd_kernel, ts=ts, q=q, d_inner=d_inner, gn=gn),
        out_shape=jax.ShapeDtypeStruct((bsz, seq, d), F32),
        grid=(bsz, seq // ts),
        in_specs=[tile, _const_spec((d, d_inner)), _const_spec((d, d_inner)), _const_spec((d, gn)),
                  _const_spec((d, gn)), _const_spec((d, LANES)), _const_spec((SSM_CONV, conv_dim)),
                  _const_spec((1, conv_dim)), _const_spec((1, LANES)), _const_spec((1, LANES)),
                  _const_spec((1, d_inner)), _const_spec((1, d_inner)), _const_spec((LANES, d_inner)),
                  _const_spec((d_inner, d)), _const_spec((1, d)), _const_spec((1, d))],
        out_specs=tile,
        scratch_shapes=[pltpu.VMEM((CONV_HALO + ts, conv_dim), F32), pltpu.VMEM((ts, d_inner), F32),
                        pltpu.VMEM((ts, d_inner), F32), pltpu.VMEM((ts, gn), BF16), pltpu.VMEM((ts, gn), BF16),
                        pltpu.VMEM((ts, d_inner), F32),
                        pltpu.VMEM((SSM_N_GROUPS, SSM_D_STATE, d_inner // SSM_N_GROUPS), F32)],
        compiler_params=_params(2),
        name="ssd_mixer_ln",
    )(h, wz, wx, wb, wc, wdt, conv_w.astype(F32), conv_b.reshape(1, conv_dim), dtb, alog, dexp,
      norm_g.reshape(1, d_inner), expand, w_out.astype(BF16), ln_g.reshape(1, d), ln_b.reshape(1, d))


def _xattn_kernel(h_ref, mem_ref, wq_ref, wk_ref, wv_ref, wo_ref, g_ref, b_ref, wrh_ref, wrl_ref, br_ref,
                  o_ref, ob_ref, route_ref, cnt_ref, k_scr, v_scr, carry_scr, *, ts, hd):
    b = pl.program_id(0)
    s = pl.program_id(1)

    @pl.when(s == 0)
    def _():
        mb = mem_ref[...].astype(BF16)
        k_scr[...] = _dot(mb, wk_ref[...]).astype(BF16)
        v_scr[...] = _dot(mb, wv_ref[...]).astype(BF16)

    @pl.when((b == 0) & (s == 0))
    def _():
        carry_scr[...] = jnp.zeros(carry_scr.shape, F32)

    h = h_ref[...]
    qv = (_dot(h.astype(BF16), wq_ref[...]) * (hd ** -0.5)).astype(BF16)
    xa = None
    for hh in range(XA_HEADS):
        cols = slice(hh * hd, (hh + 1) * hd)
        sc = _dot_nt(qv[:, cols], k_scr[:, cols])
        p = jnp.exp(sc - jnp.max(sc, -1, keepdims=True))
        o = _dot(p.astype(BF16), v_scr[:, cols]) / jnp.sum(p, -1, keepdims=True)
        part = _dot(o.astype(BF16), wo_ref[cols, :])
        xa = part if xa is None else xa + part
    h2 = _layer_norm(DEEPNORM_ALPHA * h + xa, g_ref[...], b_ref[...])
    o_ref[...] = h2
    h2_hi = h2.astype(BF16)
    ob_ref[...] = h2_hi

    h2_lo = (h2 - h2_hi.astype(F32)).astype(BF16)
    logits = _dot(h2_hi, wrh_ref[...]) + _dot(h2_lo, wrh_ref[...]) + _dot(h2_hi, wrl_ref[...]) + br_ref[...]
    lane = lax.broadcasted_iota(jnp.int32, (ts, LANES), 1).astype(F32)
    work = logits
    vals, idxs, sels = [], [], []
    for _ in range(TOP_K):
        m = jnp.max(work, -1, keepdims=True)
        ik = jnp.min(jnp.where(work == m, lane, float(LANES)), -1, keepdims=True)
        sel = lane == ik
        vals.append(m)
        idxs.append(ik)
        sels.append(sel)
        work = jnp.where(sel, -jnp.inf, work)
    exps = [jnp.exp(v - vals[0]) for v in vals]
    den = exps[0]
    for e in exps[1:]:
        den = den + e
    onehot = jnp.zeros((ts, LANES), F32)
    for sel in sels:
        onehot = onehot + sel.astype(F32)
    below = (lax.broadcasted_iota(jnp.int32, (ts, ts), 0) > lax.broadcasted_iota(jnp.int32, (ts, ts), 1))
    before = _dot(jnp.where(below, 1.0, 0.0).astype(BF16), onehot.astype(BF16)) + carry_scr[0:1, :]
    route = jnp.zeros((ts, LANES), F32)
    for k in range(TOP_K):
        rank = jnp.sum(jnp.where(sels[k], before, 0.0), -1, keepdims=True)
        route = jnp.where(lane == float(k), idxs[k], route)
        route = jnp.where(lane == float(TOP_K + k), exps[k] / den, route)
        route = jnp.where(lane == float(2 * TOP_K + k), rank, route)
    route_ref[...] = route
    carry_scr[...] = carry_scr[...] + jnp.sum(onehot, 0, keepdims=True)
    cnt_ref[...] = carry_scr[...]


def _xattn_layer(h, mem, wq, wk, wv, wo, ln_g, ln_b, w_router, b_router):
    bsz, seq, d = h.shape
    mlen = mem.shape[1]
    ts = XATTN_TILE
    hd = d // XA_HEADS
    n_exp = w_router.shape[1]
    wr = jnp.pad(w_router.astype(F32), ((0, 0), (0, LANES - n_exp)))
    wr_hi = wr.astype(BF16)
    wr_lo = (wr - wr_hi.astype(F32)).astype(BF16)
    br = jnp.pad(b_router.astype(F32), (0, LANES - n_exp), constant_values=-jnp.inf).reshape(1, LANES)
    tile = pl.BlockSpec((None, ts, d), lambda b, s: (b, s, 0))
    rtile = pl.BlockSpec((None, ts, LANES), lambda b, s: (b, s, 0))
    return pl.pallas_call(
        functools.partial(_xattn_kernel, ts=ts, hd=hd),
        out_shape=(jax.ShapeDtypeStruct((bsz, seq, d), F32), jax.ShapeDtypeStruct((bsz, seq, d), BF16),
                   jax.ShapeDtypeStruct((bsz, seq, LANES), F32), jax.ShapeDtypeStruct((8, LANES), F32)),
        grid=(bsz, seq // ts),
        in_specs=[tile, pl.BlockSpec((None, mlen, d), lambda b, s: (b, 0, 0)),
                  _const_spec((d, d)), _const_spec((d, d)), _const_spec((d, d)), _const_spec((d, d)),
                  _const_spec((1, d)), _const_spec((1, d)), _const_spec((d, LANES)), _const_spec((d, LANES)),
                  _const_spec((1, LANES))],
        out_specs=(tile, tile, rtile, pl.BlockSpec((8, LANES), lambda b, s: (0, 0))),
        scratch_shapes=[pltpu.VMEM((mlen, d), BF16), pltpu.VMEM((mlen, d), BF16), pltpu.VMEM((8, LANES), F32)],
        compiler_params=_params(2),
        name="xattn_ln_router",
    )(h, mem, wq.astype(BF16), wk.astype(BF16), wv.astype(BF16), wo.astype(BF16),
      ln_g.reshape(1, d), ln_b.reshape(1, d), wr_hi, wr_lo, br)


def _moe_kernel(be_ref, nu_ref, x_ref, wg_ref, wu_ref, bg_ref, bu_ref, wd_ref, bd_ref, o_ref):
    i = pl.program_id(0)

    @pl.when(i < nu_ref[0])
    def _():
        x = x_ref[...]
        gate = jnp.minimum(_dot(x, wg_ref[...]) + bg_ref[...], SWIGLU_LIMIT)
        up = jnp.clip(_dot(x, wu_ref[...]) + bu_ref[...], -SWIGLU_LIMIT, SWIGLU_LIMIT)
        act = (up + 1.0) * (gate * _sigmoid(SWIGLU_ALPHA * gate))
        o_ref[...] = _dot(act.astype(BF16), wd_ref[...]) + bd_ref[...]


def _moe_experts(xs, block_e, n_used, w_gate, w_up, b_gate, b_up, w_down, b_down):
    n_rows, d = xs.shape
    n_exp, _, f = w_gate.shape
    tm = MOE_TILE
    n_blocks = n_rows // tm

    def row_map(i, be, nu):
        return (jnp.minimum(i, nu[0] - 1), 0)

    def w_map(i, be, nu):
        return (be[i], 0, 0)

    grid_spec = pltpu.PrefetchScalarGridSpec(
        num_scalar_prefetch=2,
        grid=(n_blocks,),
        in_specs=[pl.BlockSpec((tm, d), row_map),
                  pl.BlockSpec((None, d, f), w_map), pl.BlockSpec((None, d, f), w_map),
                  pl.BlockSpec((None, 1, f), w_map), pl.BlockSpec((None, 1, f), w_map),
                  pl.BlockSpec((None, f, d), w_map), pl.BlockSpec((None, 1, d), w_map)],
        out_specs=pl.BlockSpec((tm, d), row_map),
    )
    return pl.pallas_call(
        _moe_kernel,
        out_shape=jax.ShapeDtypeStruct((n_rows, d), F32),
        grid_spec=grid_spec,
        compiler_params=_params(1),
        name="moe_experts",
    )(block_e, n_used, xs, w_gate, w_up, b_gate.reshape(n_exp, 1, f), b_up.reshape(n_exp, 1, f),
      w_down, b_down.reshape(n_exp, 1, d))


def _combine_kernel(h_ref, y_ref, route_ref, g_ref, b_ref, o_ref):
    ff = None
    for k in range(TOP_K):
        part = y_ref[k] * route_ref[:, TOP_K + k:TOP_K + k + 1]
        ff = part if ff is None else ff + part
    o_ref[...] = _layer_norm(DEEPNORM_ALPHA * h_ref[...] + ff, g_ref[...], b_ref[...])


def _combine_layer(h, y4, route, ln_g, ln_b):
    t, d = h.shape
    ts = COMBINE_TILE
    return pl.pallas_call(
        _combine_kernel,
        out_shape=jax.ShapeDtypeStruct((t, d), F32),
        grid=(t // ts,),
        in_specs=[pl.BlockSpec((ts, d), lambda i: (i, 0)), pl.BlockSpec((TOP_K, ts, d), lambda i: (0, i, 0)),
                  pl.BlockSpec((ts, LANES), lambda i: (i, 0)), _const_spec((1, d)), _const_spec((1, d))],
        out_specs=pl.BlockSpec((ts, d), lambda i: (i, 0)),
        compiler_params=_params(1),
        name="moe_combine_ln",
    )(h, y4, route, ln_g.reshape(1, d), ln_b.reshape(1, d))


def _moe_layer(h2, h2b, route, cnt, w_gate_up, b_gate_up, w_down, b_down, ln_g, ln_b):
    bsz, seq, d = h2.shape
    t = bsz * seq
    tm = MOE_TILE
    n_exp = w_gate_up.shape[0]
    route = route.reshape(t, LANES)
    idx = route[:, 0:TOP_K].astype(jnp.int32)
    rank = route[:, 2 * TOP_K:3 * TOP_K].astype(jnp.int32)
    counts = cnt[0, :n_exp].astype(jnp.int32)
    padded = (counts + tm - 1) // tm * tm
    pend = jnp.cumsum(padded)
    pstart = pend - padded
    dest = pstart[idx] + rank
    n_blocks = (t * TOP_K) // tm + n_exp
    n_rows = n_blocks * tm
    n_used = (pend[-1] // tm).astype(jnp.int32)
    blk = jnp.arange(n_blocks, dtype=jnp.int32)
    block_e = jnp.searchsorted(pend, blk * tm, side='right').astype(jnp.int32)
    block_e = jnp.minimum(block_e, n_exp - 1)
    block_e = jnp.where(blk < n_used, block_e, block_e[jnp.maximum(n_used - 1, 0)])
    tok = jnp.repeat(jnp.arange(t, dtype=jnp.int32), TOP_K)
    row_tok = jnp.zeros((n_rows,), jnp.int32).at[dest.reshape(-1)].set(tok)
    xs = h2b.reshape(t, d)[row_tok]
    w_gate = w_gate_up[:, :, 0::2].astype(BF16)
    w_up = w_gate_up[:, :, 1::2].astype(BF16)
    y = _moe_experts(xs, block_e, n_used.reshape(1), w_gate, w_up, b_gate_up[:, 0::2], b_gate_up[:, 1::2],
                     w_down.astype(BF16), b_down)
    y4 = y[dest.T.reshape(-1)].reshape(TOP_K, t, d)
    return _combine_layer(h2.reshape(t, d), y4, route, ln_g, ln_b).reshape(bsz, seq, d)


def kernel(x, mem, pool_w_in, pool_w_grp, pool_scale, pool_w_out, ssm_w_in, ssm_conv_w, ssm_conv_b, ssm_dt_bias, ssm_a_log, ssm_d, ssm_norm_g, ssm_w_out, xa_wq, xa_wk, xa_wv, xa_wo, moe_w_router, moe_b_router, moe_w_gate_up, moe_b_gate_up, moe_w_down, moe_b_down, ln_mix_g, ln_mix_b, ln_xa_g, ln_xa_b, ln_ffn_g, ln_ffn_b):
    h = x
    for i in range(DEPTH):
        j = i // 2
        if i % 2 == 0:
            h = _pool_layer(h, pool_w_in[j], pool_w_grp[j], pool_scale[j], pool_w_out[j], ln_mix_g[i], ln_mix_b[i])
        else:
            h = _ssd_layer(h, ssm_w_in[j], ssm_conv_w[j], ssm_conv_b[j], ssm_dt_bias[j], ssm_a_log[j], ssm_d[j],
                           ssm_norm_g[j], ssm_w_out[j], ln_mix_g[i], ln_mix_b[i])
        h2, h2b, route, cnt = _xattn_layer(h, mem, xa_wq[i], xa_wk[i], xa_wv[i], xa_wo[i], ln_xa_g[i], ln_xa_b[i],
                                           moe_w_router[i], moe_b_router[i])
        h = _moe_layer(h2, h2b, route, cnt, moe_w_gate_up[i], moe_b_gate_up[i], moe_w_down[i], moe_b_down[i],
                       ln_ffn_g[i], ln_ffn_b[i])
    return h
```

```python
import functools

import jax
import jax.numpy as jnp
from jax import lax
from jax.experimental import pallas as pl
from jax.experimental.pallas import tpu as pltpu

F32 = jnp.float32
BF16 = jnp.bfloat16

DEPTH = 2
DEEPNORM_ALPHA = (2 * DEPTH) ** 0.25
LN_EPS = 1e-5
POOL_WINDOWS = (2, 4, 8, 16)
POOL_HALO = 16
SSM_HEAD_DIM = 64
SSM_N_GROUPS = 8
SSM_HEADS_PER_GROUP = 4
SSM_D_STATE = 128
SSM_CONV = 4
CONV_HALO = 8
SSM_NORM_EPS = 1e-5
XA_HEADS = 4
N_EXPERTS = 32
TOP_K = 4
SWIGLU_LIMIT = 7.0
SWIGLU_ALPHA = 1.702

LANES = 128
VMEM_LIMIT_BYTES = 56 * 1024 * 1024

POOL_TILE = 512
XATTN_TILE = 512
SSD_TILE = 256
SSD_CHUNK = 256
MOE_TILE = 512
COMBINE_TILE = 512

NEG_BIG = -1e30


def _layer_norm(v, g, b):
    mu = jnp.mean(v, -1, keepdims=True)
    d = v - mu
    var = jnp.mean(d * d, -1, keepdims=True)
    return d * lax.rsqrt(var + LN_EPS) * g + b


def _dot(a, b):
    return jnp.dot(a, b, preferred_element_type=F32)


def _dot_nt(a, b):
    return lax.dot_general(a, b, (((1,), (1,)), ((), ())), preferred_element_type=F32)


def _dot_tn(a, b):
    return lax.dot_general(a, b, (((0,), (0,)), ((), ())), preferred_element_type=F32)


def _const_spec(shape):
    nd = len(shape)
    return pl.BlockSpec(shape, lambda *_: (0,) * nd, pipeline_mode=pl.Buffered(1))


def _params(n_axes):
    return pltpu.CompilerParams(dimension_semantics=("arbitrary",) * n_axes,
                                vmem_limit_bytes=VMEM_LIMIT_BYTES)


def _pool_kernel(x_ref, win_ref, wgrp_ref, scale_ref, wout_ref, g_ref, b_ref, o_ref, ext_ref, *, ts, gd):
    s = pl.program_id(1)

    @pl.when(s == 0)
    def _():
        ext_ref[0:POOL_HALO, :] = jnp.zeros((POOL_HALO, ext_ref.shape[1]), F32)

    x = x_ref[...]
    ext_ref[POOL_HALO:, :] = _dot(x.astype(BF16), win_ref[...])
    pos = lax.broadcasted_iota(jnp.int32, (ts, 1), 0) + s * ts
    mix = None
    for g, w in enumerate(POOL_WINDOWS):
        cols = slice(g * gd, (g + 1) * gd)
        e = ext_ref[:, cols]
        acc = e
        sh = 1
        while sh < w:
            acc = acc + pltpu.roll(acc, sh, axis=0)
            sh *= 2
        cnt = jnp.minimum(pos + 1, w).astype(F32)
        m = acc[POOL_HALO:, :] / cnt - e[POOL_HALO:, :]
        yg = _dot(m.astype(BF16), wgrp_ref[g]) * scale_ref[:, cols]
        part = _dot(yg.astype(BF16), wout_ref[cols, :])
        mix = part if mix is None else mix + part
    ext_ref[0:POOL_HALO, :] = ext_ref[ts:ts + POOL_HALO, :]
    o_ref[...] = _layer_norm(DEEPNORM_ALPHA * x + mix, g_ref[...], b_ref[...])


def _pool_layer(h, w_in, w_grp, scale, w_out, ln_g, ln_b):
    bsz, seq, d = h.shape
    ts = POOL_TILE
    gd = d // len(POOL_WINDOWS)
    tile = pl.BlockSpec((None, ts, d), lambda b, s: (b, s, 0))
    return pl.pallas_call(
        functools.partial(_pool_kernel, ts=ts, gd=gd),
        out_shape=jax.ShapeDtypeStruct((bsz, seq, d), F32),
        grid=(bsz, seq // ts),
        in_specs=[tile, _const_spec((d, d)), _const_spec((len(POOL_WINDOWS), gd, gd)), _const_spec((1, d)),
                  _const_spec((d, d)), _const_spec((1, d)), _const_spec((1, d))],
        out_specs=tile,
        scratch_shapes=[pltpu.VMEM((POOL_HALO + ts, d), F32)],
        compiler_params=_params(2),
        name="pool_mixer_ln",
    )(h, w_in.astype(BF16), w_grp.astype(BF16), scale.reshape(1, d), w_out.astype(BF16),
      ln_g.reshape(1, d), ln_b.reshape(1, d))


def _sigmoid(v):
    return 1.0 / (1.0 + jnp.exp(-v))


def _ssd_kernel(x_ref, wz_ref, wx_ref, wb_ref, wc_ref, wdt_ref, convw_ref, convb_ref, dtb_ref, alog_ref,
                dexp_ref, normg_ref, expand_ref, wout_ref, g_ref, b_ref, o_ref,
                xbc_scr, z_scr, actx_scr, bmat_scr, cmat_scr, y_scr, state_scr, *, ts, q, d_inner, gn):
    s = pl.program_id(1)
    conv_dim = d_inner + 2 * gn
    gw = d_inner // SSM_N_GROUPS

    @pl.when(s == 0)
    def _():
        xbc_scr[0:CONV_HALO, :] = jnp.zeros((CONV_HALO, conv_dim), F32)
        state_scr[...] = jnp.zeros(state_scr.shape, F32)

    x = x_ref[...]
    xb = x.astype(BF16)
    z_scr[...] = _dot(xb, wz_ref[...])
    xbc_scr[CONV_HALO:, 0:d_inner] = _dot(xb, wx_ref[...])
    xbc_scr[CONV_HALO:, d_inner:d_inner + gn] = _dot(xb, wb_ref[...])
    xbc_scr[CONV_HALO:, d_inner + gn:] = _dot(xb, wc_ref[...])
    dt_raw = _dot(xb, wdt_ref[...]) + dtb_ref[...]
    dtv = jnp.maximum(dt_raw, 0.0) + jnp.log1p(jnp.exp(-jnp.abs(dt_raw)))
    a_all = dtv * (-jnp.exp(alog_ref[...]))

    strip = 512
    for c in range(0, conv_dim, strip):
        cs = slice(c, c + strip)
        acc = convb_ref[:, cs]
        for k in range(SSM_CONV):
            r0 = CONV_HALO - (SSM_CONV - 1) + k
            acc = acc + convw_ref[k:k + 1, cs] * xbc_scr[r0:r0 + ts, cs]
        act = acc * _sigmoid(acc)
        if c < d_inner:
            actx_scr[:, cs] = act
        elif c < d_inner + gn:
            bmat_scr[:, c - d_inner:c - d_inner + strip] = act.astype(BF16)
        else:
            cmat_scr[:, c - d_inner - gn:c - d_inner - gn + strip] = act.astype(BF16)
    xbc_scr[0:CONV_HALO, :] = xbc_scr[ts:ts + CONV_HALO, :]

    expand = expand_ref[...]
    row_i = lax.broadcasted_iota(jnp.int32, (q, LANES), 0)
    causal = lax.broadcasted_iota(jnp.int32, (q, q), 0) >= lax.broadcasted_iota(jnp.int32, (q, q), 1)
    head_of_lane = lax.broadcasted_iota(jnp.int32, (q, gw), 1) // SSM_HEAD_DIM

    for c in range(ts // q):
        rows = slice(c * q, (c + 1) * q)
        acs = a_all[rows, :]
        sh = 1
        while sh < q:
            acs = acs + jnp.where(row_i >= sh, pltpu.roll(acs, sh, axis=0), 0.0)
            sh *= 2
        acs_t = acs.T
        a_last = acs[q - 1:q, :]
        e_in = _dot(jnp.exp(acs).astype(BF16), expand)
        dec = _dot(jnp.exp(a_last - acs).astype(BF16), expand)
        dtx = _dot(dtv[rows, :].astype(BF16), expand)
        cd = jnp.broadcast_to(jnp.exp(a_last), (8, LANES))
        cd_hi = cd.astype(BF16)
        cd_lo = (cd - cd_hi.astype(F32)).astype(BF16)
        cdx = (_dot(cd_hi, expand) + _dot(cd_lo, expand))[0:1, :]
        xd = actx_scr[rows, :] * dtx
        xdd = xd * dec
        for g in range(SSM_N_GROUPS):
            gc = slice(g * gw, (g + 1) * gw)
            nc = slice(g * SSM_D_STATE, (g + 1) * SSM_D_STATE)
            bg = bmat_scr[rows, nc]
            cg = cmat_scr[rows, nc]
            cb = _dot_nt(cg, bg)
            xg = xd[:, gc]
            yg = _dot(cg, state_scr[g].astype(BF16)) * e_in[:, gc]
            for r in range(SSM_HEADS_PER_GROUP):
                hd = g * SSM_HEADS_PER_GROUP + r
                seg = acs[:, hd:hd + 1] - acs_t[hd:hd + 1, :]
                lmat = jnp.exp(jnp.where(causal, seg, NEG_BIG))
                mh = (cb * lmat).astype(BF16)
                xm = jnp.where(head_of_lane == r, xg, 0.0).astype(BF16)
                yg = yg + _dot(mh, xm)
            y_scr[rows, gc] = yg
            state_scr[g] = state_scr[g] * cdx[:, gc] + _dot_tn(bg, xdd[:, gc].astype(BF16))

    mix = None
    for g in range(SSM_N_GROUPS):
        gc = slice(g * gw, (g + 1) * gw)
        zz = z_scr[:, gc]
        yv = (y_scr[:, gc] + dexp_ref[:, gc] * actx_scr[:, gc]) * (zz * _sigmoid(zz))
        yv = yv * lax.rsqrt(jnp.mean(yv * yv, -1, keepdims=True) + SSM_NORM_EPS) * normg_ref[:, gc]
        part = _dot(yv.astype(BF16), wout_ref[gc, :])
        mix = part if mix is None else mix + part
    o_ref[...] = _layer_norm(DEEPNORM_ALPHA * x + mix, g_ref[...], b_ref[...])


def _ssd_layer(h, w_in, conv_w, conv_b, dt_bias, a_log, d_skip, norm_g, w_out, ln_g, ln_b):
    bsz, seq, d = h.shape
    n_heads = a_log.shape[0]
    d_inner = n_heads * SSM_HEAD_DIM
    gn = SSM_N_GROUPS * SSM_D_STATE
    conv_dim = d_inner + 2 * gn
    ts, q = SSD_TILE, SSD_CHUNK
    w_in = w_in.astype(BF16)
    wz = w_in[:, :d_inner]
    wx = w_in[:, d_inner:2 * d_inner]
    wb = w_in[:, 2 * d_inner:2 * d_inner + gn]
    wc = w_in[:, 2 * d_inner + gn:2 * d_inner + 2 * gn]
    pad = LANES - n_heads
    wdt = jnp.pad(w_in[:, d_inner + conv_dim:], ((0, 0), (0, pad)))
    dtb = jnp.pad(dt_bias.astype(F32), (0, pad)).reshape(1, LANES)
    alog = jnp.pad(a_log.astype(F32), (0, pad)).reshape(1, LANES)
    dexp = jnp.repeat(d_skip.astype(F32), SSM_HEAD_DIM).reshape(1, d_inner)
    expand = (jnp.arange(LANES)[:, None] == (jnp.arange(d_inner)[None, :] // SSM_HEAD_DIM)).astype(BF16)
    tile = pl.BlockSpec((None, ts, d), lambda b, s: (b, s, 0))
    return pl.pallas_call(
        functools.partial(_ssd_kernel, ts=ts, q=q, d_inner=d_inner, gn=gn),
        out_shape=jax.ShapeDtypeStruct((bsz, seq, d), F32),
        grid=(bsz, seq // ts),
        in_specs=[tile, _const_spec((d, d_inner)), _const_spec((d, d_inner)), _const_spec((d, gn)),
                  _const_spec((d, gn)), _const_spec((d, LANES)), _const_spec((SSM_CONV, conv_dim)),
                  _const_spec((1, conv_dim)), _const_spec((1, LANES)), _const_spec((1, LANES)),
                  _const_spec((1, d_inner)), _const_spec((1, d_inner)), _const_spec((LANES, d_inner)),
                  _const_spec((d_inner, d)), _const_spec((1, d)), _const_spec((1, d))],
        out_specs=tile,
        scratch_shapes=[pltpu.VMEM((CONV_HALO + ts, conv_dim), F32), pltpu.VMEM((ts, d_inner), F32),
                        pltpu.VMEM((ts, d_inner), F32), pltpu.VMEM((ts, gn), BF16), pltpu.VMEM((ts, gn), BF16),
                        pltpu.VMEM((ts, d_inner), F32),
                        pltpu.VMEM((SSM_N_GROUPS, SSM_D_STATE, d_inner // SSM_N_GROUPS), F32)],
        compiler_params=_params(2),
        name="ssd_mixer_ln",
    )(h, wz, wx, wb, wc, wdt, conv_w.astype(F32), conv_b.reshape(1, conv_dim), dtb, alog, dexp,
      norm_g.reshape(1, d_inner), expand, w_out.astype(BF16), ln_g.reshape(1, d), ln_b.reshape(1, d))


def _xattn_kernel(h_ref, mem_ref, wq_ref, wk_ref, wv_ref, wo_ref, g_ref, b_ref, wrh_ref, wrl_ref, br_ref,
                  o_ref, ob_ref, route_ref, cnt_ref, k_scr, v_scr, carry_scr, *, ts, hd):
    b = pl.program_id(0)
    s = pl.program_id(1)

    @pl.when(s == 0)
    def _():
        mb = mem_ref[...].astype(BF16)
        k_scr[...] = _dot(mb, wk_ref[...]).astype(BF16)
        v_scr[...] = _dot(mb, wv_ref[...]).astype(BF16)

    @pl.when((b == 0) & (s == 0))
    def _():
        carry_scr[...] = jnp.zeros(carry_scr.shape, F32)

    h = h_ref[...]
    qv = (_dot(h.astype(BF16), wq_ref[...]) * (hd ** -0.5)).astype(BF16)
    xa = None
    for hh in range(XA_HEADS):
        cols = slice(hh * hd, (hh + 1) * hd)
        sc = _dot_nt(qv[:, cols], k_scr[:, cols])
        p = jnp.exp(sc - jnp.max(sc, -1, keepdims=True))
        o = _dot(p.astype(BF16), v_scr[:, cols]) / jnp.sum(p, -1, keepdims=True)
        part = _dot(o.astype(BF16), wo_ref[cols, :])
        xa = part if xa is None else xa + part
    h2 = _layer_norm(DEEPNORM_ALPHA * h + xa, g_ref[...], b_ref[...])
    o_ref[...] = h2
    h2_hi = h2.astype(BF16)
    ob_ref[...] = h2_hi

    h2_lo = (h2 - h2_hi.astype(F32)).astype(BF16)
    logits = _dot(h2_hi, wrh_ref[...]) + _dot(h2_lo, wrh_ref[...]) + _dot(h2_hi, wrl_ref[...]) + br_ref[...]
    lane = lax.broadcasted_iota(jnp.int32, (ts, LANES), 1).astype(F32)
    work = logits
    vals, idxs, sels = [], [], []
    for _ in range(TOP_K):
        m = jnp.max(work, -1, keepdims=True)
        ik = jnp.min(jnp.where(work == m, lane, float(LANES)), -1, keepdims=True)
        sel = lane == ik
        vals.append(m)
        idxs.append(ik)
        sels.append(sel)
        work = jnp.where(sel, -jnp.inf, work)
    exps = [jnp.exp(v - vals[0]) for v in vals]
    den = exps[0]
    for e in exps[1:]:
        den = den + e
    onehot = jnp.zeros((ts, LANES), F32)
    for sel in sels:
        onehot = onehot + sel.astype(F32)
    below = (lax.broadcasted_iota(jnp.int32, (ts, ts), 0) > lax.broadcasted_iota(jnp.int32, (ts, ts), 1))
    before = _dot(jnp.where(below, 1.0, 0.0).astype(BF16), onehot.astype(BF16)) + carry_scr[0:1, :]
    route = jnp.zeros((ts, LANES), F32)
    for k in range(TOP_K):
        rank = jnp.sum(jnp.where(sels[k], before, 0.0), -1, keepdims=True)
        route = jnp.where(lane == float(k), idxs[k], route)
        route = jnp.where(lane == float(TOP_K + k), exps[k] / den, route)
        route = jnp.where(lane == float(2 * TOP_K + k), rank, route)
    route_ref[...] = route
    carry_scr[...] = carry_scr[...] + jnp.sum(onehot, 0, keepdims=True)
    cnt_ref[...] = carry_scr[...]


def _xattn_layer(h, mem, wq, wk, wv, wo, ln_g, ln_b, w_router, b_router):
    bsz, seq, d = h.shape
    mlen = mem.shape[1]
    ts = XATTN_TILE
    hd = d // XA_HEADS
    n_exp = w_router.shape[1]
    wr = jnp.pad(w_router.astype(F32), ((0, 0), (0, LANES - n_exp)))
    wr_hi = wr.astype(BF16)
    wr_lo = (wr - wr_hi.astype(F32)).astype(BF16)
    br = jnp.pad(b_router.astype(F32), (0, LANES - n_exp), constant_values=-jnp.inf).reshape(1, LANES)
    tile = pl.BlockSpec((None, ts, d), lambda b, s: (b, s, 0))
    rtile = pl.BlockSpec((None, ts, LANES), lambda b, s: (b, s, 0))
    return pl.pallas_call(
        functools.partial(_xattn_kernel, ts=ts, hd=hd),
        out_shape=(jax.ShapeDtypeStruct((bsz, seq, d), F32), jax.ShapeDtypeStruct((bsz, seq, d), BF16),
                   jax.ShapeDtypeStruct((bsz, seq, LANES), F32), jax.ShapeDtypeStruct((8, LANES), F32)),
        grid=(bsz, seq // ts),
        in_specs=[tile, pl.BlockSpec((None, mlen, d), lambda b, s: (b, 0, 0)),
                  _const_spec((d, d)), _const_spec((d, d)), _const_spec((d, d)), _const_spec((d, d)),
                  _const_spec((1, d)), _const_spec((1, d)), _const_spec((d, LANES)), _const_spec((d, LANES)),
                  _const_spec((1, LANES))],
        out_specs=(tile, tile, rtile, pl.BlockSpec((8, LANES), lambda b, s: (0, 0))),
        scratch_shapes=[pltpu.VMEM((mlen, d), BF16), pltpu.VMEM((mlen, d), BF16), pltpu.VMEM((8, LANES), F32)],
        compiler_params=_params(2),
        name="xattn_ln_router",
    )(h, mem, wq.astype(BF16), wk.astype(BF16), wv.astype(BF16), wo.astype(BF16),
      ln_g.reshape(1, d), ln_b.reshape(1, d), wr_hi, wr_lo, br)


def _moe_kernel(be_ref, nu_ref, x_ref, wgu_ref, bg_ref, bu_ref, wd_ref, bd_ref, o_ref, wg_scr, wu_scr, wd_scr):
    i = pl.program_id(0)
    active = i < nu_ref[0]
    new_expert = (i == 0) | (be_ref[i] != be_ref[jnp.maximum(i - 1, 0)])

    @pl.when(active & new_expert)
    def _():
        w2 = 2 * LANES
        src = lax.broadcasted_iota(jnp.int32, (w2, w2), 0)
        dst = lax.broadcasted_iota(jnp.int32, (w2, w2), 1)
        perm = jnp.where(src == jnp.where(dst < LANES, 2 * dst, 2 * (dst - LANES) + 1), 1.0, 0.0).astype(BF16)
        for c in range(wgu_ref.shape[1] // w2):
            res = _dot(wgu_ref[:, c * w2:(c + 1) * w2].astype(BF16), perm)
            wg_scr[:, c * LANES:(c + 1) * LANES] = res[:, :LANES].astype(BF16)
            wu_scr[:, c * LANES:(c + 1) * LANES] = res[:, LANES:].astype(BF16)
        wd_scr[...] = wd_ref[...].astype(BF16)

    @pl.when(active)
    def _():
        x = x_ref[...]
        gate = jnp.minimum(_dot(x, wg_scr[...]) + bg_ref[...], SWIGLU_LIMIT)
        up = jnp.clip(_dot(x, wu_scr[...]) + bu_ref[...], -SWIGLU_LIMIT, SWIGLU_LIMIT)
        act = (up + 1.0) * (gate * _sigmoid(SWIGLU_ALPHA * gate))
        o_ref[...] = _dot(act.astype(BF16), wd_scr[...]) + bd_ref[...]


def _moe_experts(xs, block_e, n_used, w_gate_up, b_gate, b_up, w_down, b_down):
    n_rows, d = xs.shape
    n_exp, f, _ = w_down.shape
    tm = MOE_TILE
    n_blocks = n_rows // tm

    def row_map(i, be, nu):
        return (jnp.minimum(i, nu[0] - 1), 0)

    def w_map(i, be, nu):
        return (be[i], 0, 0)

    grid_spec = pltpu.PrefetchScalarGridSpec(
        num_scalar_prefetch=2,
        grid=(n_blocks,),
        in_specs=[pl.BlockSpec((tm, d), row_map),
                  pl.BlockSpec((None, d, 2 * f), w_map),
                  pl.BlockSpec((None, 1, f), w_map), pl.BlockSpec((None, 1, f), w_map),
                  pl.BlockSpec((None, f, d), w_map), pl.BlockSpec((None, 1, d), w_map)],
        out_specs=pl.BlockSpec((tm, d), row_map),
        scratch_shapes=[pltpu.VMEM((d, f), BF16), pltpu.VMEM((d, f), BF16), pltpu.VMEM((f, d), BF16)],
    )
    return pl.pallas_call(
        _moe_kernel,
        out_shape=jax.ShapeDtypeStruct((n_rows, d), F32),
        grid_spec=grid_spec,
        compiler_params=_params(1),
        name="moe_experts",
    )(block_e, n_used, xs, w_gate_up, b_gate.reshape(n_exp, 1, f), b_up.reshape(n_exp, 1, f),
      w_down, b_down.reshape(n_exp, 1, d))


def _combine_kernel(h_ref, y_ref, route_ref, g_ref, b_ref, o_ref):
    ff = None
    for k in range(TOP_K):
        part = y_ref[k] * route_ref[:, TOP_K + k:TOP_K + k + 1]
        ff = part if ff is None else ff + part
    o_ref[...] = _layer_norm(DEEPNORM_ALPHA * h_ref[...] + ff, g_ref[...], b_ref[...])


def _combine_layer(h, y4, route, ln_g, ln_b):
    t, d = h.shape
    ts = COMBINE_TILE
    return pl.pallas_call(
        _combine_kernel,
        out_shape=jax.ShapeDtypeStruct((t, d), F32),
        grid=(t // ts,),
        in_specs=[pl.BlockSpec((ts, d), lambda i: (i, 0)), pl.BlockSpec((TOP_K, ts, d), lambda i: (0, i, 0)),
                  pl.BlockSpec((ts, LANES), lambda i: (i, 0)), _const_spec((1, d)), _const_spec((1, d))],
        out_specs=pl.BlockSpec((ts, d), lambda i: (i, 0)),
        compiler_params=_params(1),
        name="moe_combine_ln",
    )(h, y4, route, ln_g.reshape(1, d), ln_b.reshape(1, d))


def _moe_layer(h2, h2b, route, cnt, w_gate_up, b_gate_up, w_down, b_down, ln_g, ln_b):
    bsz, seq, d = h2.shape
    t = bsz * seq
    tm = MOE_TILE
    n_exp = w_gate_up.shape[0]
    route = route.reshape(t, LANES)
    idx = route[:, 0:TOP_K].astype(jnp.int32)
    rank = route[:, 2 * TOP_K:3 * TOP_K].astype(jnp.int32)
    counts = cnt[0, :n_exp].astype(jnp.int32)
    padded = (counts + tm - 1) // tm * tm
    pend = jnp.cumsum(padded)
    pstart = pend - padded
    dest = pstart[idx] + rank
    n_blocks = (t * TOP_K) // tm + n_exp
    n_rows = n_blocks * tm
    n_used = (pend[-1] // tm).astype(jnp.int32)
    blk = jnp.arange(n_blocks, dtype=jnp.int32)
    block_e = jnp.searchsorted(pend, blk * tm, side='right').astype(jnp.int32)
    block_e = jnp.minimum(block_e, n_exp - 1)
    block_e = jnp.where(blk < n_used, block_e, block_e[jnp.maximum(n_used - 1, 0)])
    tok = jnp.repeat(jnp.arange(t, dtype=jnp.int32), TOP_K)
    row_tok = jnp.zeros((n_rows,), jnp.int32).at[dest.reshape(-1)].set(tok)
    xs = h2b.reshape(t, d)[row_tok]
    y = _moe_experts(xs, block_e, n_used.reshape(1), w_gate_up, b_gate_up[:, 0::2], b_gate_up[:, 1::2],
                     w_down, b_down)
    y4 = y[dest.T.reshape(-1)].reshape(TOP_K, t, d)
    return _combine_layer(h2.reshape(t, d), y4, route, ln_g, ln_b).reshape(bsz, seq, d)


def kernel(x, mem, pool_w_in, pool_w_grp, pool_scale, pool_w_out, ssm_w_in, ssm_conv_w, ssm_conv_b, ssm_dt_bias, ssm_a_log, ssm_d, ssm_norm_g, ssm_w_out, xa_wq, xa_wk, xa_wv, xa_wo, moe_w_router, moe_b_router, moe_w_gate_up, moe_b_gate_up, moe_w_down, moe_b_down, ln_mix_g, ln_mix_b, ln_xa_g, ln_xa_b, ln_ffn_g, ln_ffn_b):
    h = x
    for i in range(DEPTH):
        j = i // 2
        if i % 2 == 0:
            h = _pool_layer(h, pool_w_in[j], pool_w_grp[j], pool_scale[j], pool_w_out[j], ln_mix_g[i], ln_mix_b[i])
        else:
            h = _ssd_layer(h, ssm_w_in[j], ssm_conv_w[j], ssm_conv_b[j], ssm_dt_bias[j], ssm_a_log[j], ssm_d[j],
                           ssm_norm_g[j], ssm_w_out[j], ln_mix_g[i], ln_mix_b[i])
        h2, h2b, route, cnt = _xattn_layer(h, mem, xa_wq[i], xa_wk[i], xa_wv[i], xa_wo[i], ln_xa_g[i], ln_xa_b[i],
                                           moe_w_router[i], moe_b_router[i])
        h = _moe_layer(h2, h2b, route, cnt, moe_w_gate_up[i], moe_b_gate_up[i], moe_w_down[i], moe_b_down[i],
                       ln_ffn_g[i], ln_ffn_b[i])
    return h
```

```python
import functools

import jax
import jax.numpy as jnp
from jax import lax
from jax.experimental import pallas as pl
from jax.experimental.pallas import tpu as pltpu

F32 = jnp.float32
BF16 = jnp.bfloat16

DEPTH = 2
DEEPNORM_ALPHA = (2 * DEPTH) ** 0.25
LN_EPS = 1e-5
POOL_WINDOWS = (2, 4, 8, 16)
POOL_HALO = 16
SSM_HEAD_DIM = 64
SSM_N_GROUPS = 8
SSM_HEADS_PER_GROUP = 4
SSM_D_STATE = 128
SSM_CONV = 4
CONV_HALO = 8
SSM_NORM_EPS = 1e-5
XA_HEADS = 4
N_EXPERTS = 32
TOP_K = 4
SWIGLU_LIMIT = 7.0
SWIGLU_ALPHA = 1.702

LANES = 128
VMEM_LIMIT_BYTES = 56 * 1024 * 1024

POOL_TILE = 512
XATTN_TILE = 512
SSD_TILE = 256
SSD_CHUNK = 256
MOE_TILE = 512
COMBINE_TILE = 512

NEG_BIG = -1e30


def _layer_norm(v, g, b):
    mu = jnp.mean(v, -1, keepdims=True)
    d = v - mu
    var = jnp.mean(d * d, -1, keepdims=True)
    return d * lax.rsqrt(var + LN_EPS) * g + b


def _dot(a, b):
    return jnp.dot(a, b, preferred_element_type=F32)


def _dot_nt(a, b):
    return lax.dot_general(a, b, (((1,), (1,)), ((), ())), preferred_element_type=F32)


def _dot_tn(a, b):
    return lax.dot_general(a, b, (((0,), (0,)), ((), ())), preferred_element_type=F32)


def _const_spec(shape):
    nd = len(shape)
    return pl.BlockSpec(shape, lambda *_: (0,) * nd, pipeline_mode=pl.Buffered(1))


def _params(n_axes):
    return pltpu.CompilerParams(dimension_semantics=("arbitrary",) * n_axes,
                                vmem_limit_bytes=VMEM_LIMIT_BYTES)


def _pool_kernel(x_ref, win_ref, wgrp_ref, scale_ref, wout_ref, g_ref, b_ref, o_ref, ext_ref, *, ts, gd):
    s = pl.program_id(1)

    @pl.when(s == 0)
    def _():
        ext_ref[0:POOL_HALO, :] = jnp.zeros((POOL_HALO, ext_ref.shape[1]), F32)

    x = x_ref[...]
    ext_ref[POOL_HALO:, :] = _dot(x.astype(BF16), win_ref[...])
    pos = lax.broadcasted_iota(jnp.int32, (ts, 1), 0) + s * ts
    mix = None
    for g, w in enumerate(POOL_WINDOWS):
        cols = slice(g * gd, (g + 1) * gd)
        e = ext_ref[:, cols]
        acc = e
        sh = 1
        while sh < w:
            acc = acc + pltpu.roll(acc, sh, axis=0)
            sh *= 2
        cnt = jnp.minimum(pos + 1, w).astype(F32)
        m = acc[POOL_HALO:, :] / cnt - e[POOL_HALO:, :]
        yg = _dot(m.astype(BF16), wgrp_ref[g]) * scale_ref[:, cols]
        part = _dot(yg.astype(BF16), wout_ref[cols, :])
        mix = part if mix is None else mix + part
    ext_ref[0:POOL_HALO, :] = ext_ref[ts:ts + POOL_HALO, :]
    o_ref[...] = _layer_norm(DEEPNORM_ALPHA * x + mix, g_ref[...], b_ref[...])


def _pool_layer(h, w_in, w_grp, scale, w_out, ln_g, ln_b):
    bsz, seq, d = h.shape
    ts = POOL_TILE
    gd = d // len(POOL_WINDOWS)
    tile = pl.BlockSpec((None, ts, d), lambda b, s: (b, s, 0))
    return pl.pallas_call(
        functools.partial(_pool_kernel, ts=ts, gd=gd),
        out_shape=jax.ShapeDtypeStruct((bsz, seq, d), F32),
        grid=(bsz, seq // ts),
        in_specs=[tile, _const_spec((d, d)), _const_spec((len(POOL_WINDOWS), gd, gd)), _const_spec((1, d)),
                  _const_spec((d, d)), _const_spec((1, d)), _const_spec((1, d))],
        out_specs=tile,
        scratch_shapes=[pltpu.VMEM((POOL_HALO + ts, d), F32)],
        compiler_params=_params(2),
        name="pool_mixer_ln",
    )(h, w_in.astype(BF16), w_grp.astype(BF16), scale.reshape(1, d), w_out.astype(BF16),
      ln_g.reshape(1, d), ln_b.reshape(1, d))


def _sigmoid(v):
    return 1.0 / (1.0 + jnp.exp(-v))


def _ssd_kernel(x_ref, wz_ref, wx_ref, wb_ref, wc_ref, wdt_ref, convw_ref, convb_ref, dtb_ref, alog_ref,
                dexp_ref, normg_ref, expand_ref, wout_ref, g_ref, b_ref, o_ref,
                xbc_scr, z_scr, actx_scr, bmat_scr, cmat_scr, y_scr, state_scr, *, ts, q, d_inner, gn):
    s = pl.program_id(1)
    conv_dim = d_inner + 2 * gn
    gw = d_inner // SSM_N_GROUPS

    @pl.when(s == 0)
    def _():
        xbc_scr[0:CONV_HALO, :] = jnp.zeros((CONV_HALO, conv_dim), F32)
        state_scr[...] = jnp.zeros(state_scr.shape, F32)

    x = x_ref[...]
    xb = x.astype(BF16)
    z_scr[...] = _dot(xb, wz_ref[...])
    xbc_scr[CONV_HALO:, 0:d_inner] = _dot(xb, wx_ref[...])
    xbc_scr[CONV_HALO:, d_inner:d_inner + gn] = _dot(xb, wb_ref[...])
    xbc_scr[CONV_HALO:, d_inner + gn:] = _dot(xb, wc_ref[...])
    dt_raw = _dot(xb, wdt_ref[...]) + dtb_ref[...]
    dtv = jnp.maximum(dt_raw, 0.0) + jnp.log1p(jnp.exp(-jnp.abs(dt_raw)))
    a_all = dtv * (-jnp.exp(alog_ref[...]))

    strip = 512
    for c in range(0, conv_dim, strip):
        cs = slice(c, c + strip)
        acc = convb_ref[:, cs]
        for k in range(SSM_CONV):
            r0 = CONV_HALO - (SSM_CONV - 1) + k
            acc = acc + convw_ref[k:k + 1, cs] * xbc_scr[r0:r0 + ts, cs]
        act = acc * _sigmoid(acc)
        if c < d_inner:
            actx_scr[:, cs] = act
        elif c < d_inner + gn:
            bmat_scr[:, c - d_inner:c - d_inner + strip] = act.astype(BF16)
        else:
            cmat_scr[:, c - d_inner - gn:c - d_inner - gn + strip] = act.astype(BF16)
    xbc_scr[0:CONV_HALO, :] = xbc_scr[ts:ts + CONV_HALO, :]

    expand = expand_ref[...]
    row_i = lax.broadcasted_iota(jnp.int32, (q, LANES), 0)
    causal = lax.broadcasted_iota(jnp.int32, (q, q), 0) >= lax.broadcasted_iota(jnp.int32, (q, q), 1)
    head_of_lane = lax.broadcasted_iota(jnp.int32, (q, gw), 1) // SSM_HEAD_DIM

    for c in range(ts // q):
        rows = slice(c * q, (c + 1) * q)
        acs = a_all[rows, :]
        sh = 1
        while sh < q:
            acs = acs + jnp.where(row_i >= sh, pltpu.roll(acs, sh, axis=0), 0.0)
            sh *= 2
        acs_t = acs.T
        a_last = acs[q - 1:q, :]
        e_in = _dot(jnp.exp(acs).astype(BF16), expand)
        dec = _dot(jnp.exp(a_last - acs).astype(BF16), expand)
        dtx = _dot(dtv[rows, :].astype(BF16), expand)
        cd = jnp.broadcast_to(jnp.exp(a_last), (8, LANES))
        cd_hi = cd.astype(BF16)
        cd_lo = (cd - cd_hi.astype(F32)).astype(BF16)
        cdx = (_dot(cd_hi, expand) + _dot(cd_lo, expand))[0:1, :]
        xd = actx_scr[rows, :] * dtx
        xdd = xd * dec
        for g in range(SSM_N_GROUPS):
            gc = slice(g * gw, (g + 1) * gw)
            nc = slice(g * SSM_D_STATE, (g + 1) * SSM_D_STATE)
            bg = bmat_scr[rows, nc]
            cg = cmat_scr[rows, nc]
            cb = _dot_nt(cg, bg)
            xg = xd[:, gc]
            yg = _dot(cg, state_scr[g].astype(BF16)) * e_in[:, gc]
            for r in range(SSM_HEADS_PER_GROUP):
                hd = g * SSM_HEADS_PER_GROUP + r
                seg = acs[:, hd:hd + 1] - acs_t[hd:hd + 1, :]
                lmat = jnp.exp(jnp.where(causal, seg, NEG_BIG))
                mh = (cb * lmat).astype(BF16)
                xm = jnp.where(head_of_lane == r, xg, 0.0).astype(BF16)
                yg = yg + _dot(mh, xm)
            y_scr[rows, gc] = yg
            state_scr[g] = state_scr[g] * cdx[:, gc] + _dot_tn(bg, xdd[:, gc].astype(BF16))

    mix = None
    for g in range(SSM_N_GROUPS):
        gc = slice(g * gw, (g + 1) * gw)
        zz = z_scr[:, gc]
        yv = (y_scr[:, gc] + dexp_ref[:, gc] * actx_scr[:, gc]) * (zz * _sigmoid(zz))
        yv = yv * lax.rsqrt(jnp.mean(yv * yv, -1, keepdims=True) + SSM_NORM_EPS) * normg_ref[:, gc]
        part = _dot(yv.astype(BF16), wout_ref[gc, :])
        mix = part if mix is None else mix + part
    o_ref[...] = _layer_norm(DEEPNORM_ALPHA * x + mix, g_ref[...], b_ref[...])


def _ssd_layer(h, w_in, conv_w, conv_b, dt_bias, a_log, d_skip, norm_g, w_out, ln_g, ln_b):
    bsz, seq, d = h.shape
    n_heads = a_log.shape[0]
    d_inner = n_heads * SSM_HEAD_DIM
    gn = SSM_N_GROUPS * SSM_D_STATE
    conv_dim = d_inner + 2 * gn
    ts, q = SSD_TILE, SSD_CHUNK
    w_in = w_in.astype(BF16)
    wz = w_in[:, :d_inner]
    wx = w_in[:, d_inner:2 * d_inner]
    wb = w_in[:, 2 * d_inner:2 * d_inner + gn]
    wc = w_in[:, 2 * d_inner + gn:2 * d_inner + 2 * gn]
    pad = LANES - n_heads
    wdt = jnp.pad(w_in[:, d_inner + conv_dim:], ((0, 0), (0, pad)))
    dtb = jnp.pad(dt_bias.astype(F32), (0, pad)).reshape(1, LANES)
    alog = jnp.pad(a_log.astype(F32), (0, pad)).reshape(1, LANES)
    dexp = jnp.repeat(d_skip.astype(F32), SSM_HEAD_DIM).reshape(1, d_inner)
    expand = (jnp.arange(LANES)[:, None] == (jnp.arange(d_inner)[None, :] // SSM_HEAD_DIM)).astype(BF16)
    tile = pl.BlockSpec((None, ts, d), lambda b, s: (b, s, 0))
    return pl.pallas_call(
        functools.partial(_ssd_kernel, ts=ts, q=q, d_inner=d_inner, gn=gn),
        out_shape=jax.ShapeDtypeStruct((bsz, seq, d), F32),
        grid=(bsz, seq // ts),
        in_specs=[tile, _const_spec((d, d_inner)), _const_spec((d, d_inner)), _const_spec((d, gn)),
                  _const_spec((d, gn)), _const_spec((d, LANES)), _const_spec((SSM_CONV, conv_dim)),
                  _const_spec((1, conv_dim)), _const_spec((1, LANES)), _const_spec((1, LANES)),
                  _const_spec((1, d_inner)), _const_spec((1, d_inner)), _const_spec((LANES, d_inner)),
                  _const_spec((d_inner, d)), _const_spec((1, d)), _const_spec((1, d))],
        out_specs=tile,
        scratch_shapes=[pltpu.VMEM((CONV_HALO + ts, conv_dim), F32), pltpu.VMEM((ts, d_inner), F32),
                        pltpu.VMEM((ts, d_inner), F32), pltpu.VMEM((ts, gn), BF16), pltpu.VMEM((ts, gn), BF16),
                        pltpu.VMEM((ts, d_inner), F32),
                        pltpu.VMEM((SSM_N_GROUPS, SSM_D_STATE, d_inner // SSM_N_GROUPS), F32)],
        compiler_params=_params(2),
        name="ssd_mixer_ln",
    )(h, wz, wx, wb, wc, wdt, conv_w.astype(F32), conv_b.reshape(1, conv_dim), dtb, alog, dexp,
      norm_g.reshape(1, d_inner), expand, w_out.astype(BF16), ln_g.reshape(1, d), ln_b.reshape(1, d))


def _xattn_kernel(h_ref, mem_ref, wq_ref, wk_ref, wv_ref, wo_ref, g_ref, b_ref, wrh_ref, wrl_ref, br_ref,
                  o_ref, ob_ref, route_ref, cnt_ref, k_scr, v_scr, carry_scr, *, ts, hd):
    b = pl.program_id(0)
    s = pl.program_id(1)

    @pl.when(s == 0)
    def _():
        mb = mem_ref[...].astype(BF16)
        k_scr[...] = _dot(mb, wk_ref[...]).astype(BF16)
        v_scr[...] = _dot(mb, wv_ref[...]).astype(BF16)

    @pl.when((b == 0) & (s == 0))
    def _():
        carry_scr[...] = jnp.zeros(carry_scr.shape, F32)

    h = h_ref[...]
    qv = (_dot(h.astype(BF16), wq_ref[...]) * (hd ** -0.5)).astype(BF16)
    xa = None
    for hh in range(XA_HEADS):
        cols = slice(hh * hd, (hh + 1) * hd)
        sc = _dot_nt(qv[:, cols], k_scr[:, cols])
        p = jnp.exp(sc - jnp.max(sc, -1, keepdims=True))
        o = _dot(p.astype(BF16), v_scr[:, cols]) / jnp.sum(p, -1, keepdims=True)
        part = _dot(o.astype(BF16), wo_ref[cols, :])
        xa = part if xa is None else xa + part
    h2 = _layer_norm(DEEPNORM_ALPHA * h + xa, g_ref[...], b_ref[...])
    o_ref[...] = h2
    h2_hi = h2.astype(BF16)
    ob_ref[...] = h2_hi

    h2_lo = (h2 - h2_hi.astype(F32)).astype(BF16)
    logits = _dot(h2_hi, wrh_ref[...]) + _dot(h2_lo, wrh_ref[...]) + _dot(h2_hi, wrl_ref[...]) + br_ref[...]
    lane = lax.broadcasted_iota(jnp.int32, (ts, LANES), 1).astype(F32)
    work = logits
    vals, idxs, sels = [], [], []
    for _ in range(TOP_K):
        m = jnp.max(work, -1, keepdims=True)
        ik = jnp.min(jnp.where(work == m, lane, float(LANES)), -1, keepdims=True)
        sel = lane == ik
        vals.append(m)
        idxs.append(ik)
        sels.append(sel)
        work = jnp.where(sel, -jnp.inf, work)
    exps = [jnp.exp(v - vals[0]) for v in vals]
    den = exps[0]
    for e in exps[1:]:
        den = den + e
    onehot = jnp.zeros((ts, LANES), F32)
    for sel in sels:
        onehot = onehot + sel.astype(F32)
    below = (lax.broadcasted_iota(jnp.int32, (ts, ts), 0) > lax.broadcasted_iota(jnp.int32, (ts, ts), 1))
    before = _dot(jnp.where(below, 1.0, 0.0).astype(BF16), onehot.astype(BF16)) + carry_scr[0:1, :]
    route = jnp.zeros((ts, LANES), F32)
    for k in range(TOP_K):
        rank = jnp.sum(jnp.where(sels[k], before, 0.0), -1, keepdims=True)
        route = jnp.where(lane == float(k), idxs[k], route)
        route = jnp.where(lane == float(TOP_K + k), exps[k] / den, route)
        route = jnp.where(lane == float(2 * TOP_K + k), rank, route)
    route_ref[...] = route
    carry_scr[...] = carry_scr[...] + jnp.sum(onehot, 0, keepdims=True)
    cnt_ref[...] = carry_scr[...]


def _xattn_layer(h, mem, wq, wk, wv, wo, ln_g, ln_b, w_router, b_router):
    bsz, seq, d = h.shape
    mlen = mem.shape[1]
    ts = XATTN_TILE
    hd = d // XA_HEADS
    n_exp = w_router.shape[1]
    wr = jnp.pad(w_router.astype(F32), ((0, 0), (0, LANES - n_exp)))
    wr_hi = wr.astype(BF16)
    wr_lo = (wr - wr_hi.astype(F32)).astype(BF16)
    br = jnp.pad(b_router.astype(F32), (0, LANES - n_exp), constant_values=-jnp.inf).reshape(1, LANES)
    tile = pl.BlockSpec((None, ts, d), lambda b, s: (b, s, 0))
    rtile = pl.BlockSpec((None, ts, LANES), lambda b, s: (b, s, 0))
    return pl.pallas_call(
        functools.partial(_xattn_kernel, ts=ts, hd=hd),
        out_shape=(jax.ShapeDtypeStruct((bsz, seq, d), F32), jax.ShapeDtypeStruct((bsz, seq, d), BF16),
                   jax.ShapeDtypeStruct((bsz, seq, LANES), F32), jax.ShapeDtypeStruct((8, LANES), F32)),
        grid=(bsz, seq // ts),
        in_specs=[tile, pl.BlockSpec((None, mlen, d), lambda b, s: (b, 0, 0)),
                  _const_spec((d, d)), _const_spec((d, d)), _const_spec((d, d)), _const_spec((d, d)),
                  _const_spec((1, d)), _const_spec((1, d)), _const_spec((d, LANES)), _const_spec((d, LANES)),
                  _const_spec((1, LANES))],
        out_specs=(tile, tile, rtile, pl.BlockSpec((8, LANES), lambda b, s: (0, 0))),
        scratch_shapes=[pltpu.VMEM((mlen, d), BF16), pltpu.VMEM((mlen, d), BF16), pltpu.VMEM((8, LANES), F32)],
        compiler_params=_params(2),
        name="xattn_ln_router",
    )(h, mem, wq.astype(BF16), wk.astype(BF16), wv.astype(BF16), wo.astype(BF16),
      ln_g.reshape(1, d), ln_b.reshape(1, d), wr_hi, wr_lo, br)


def _moe_kernel(be_ref, nu_ref, x_ref, wgu_ref, bg_ref, bu_ref, wd_ref, bd_ref, o_ref, wg_scr, wu_scr, wd_scr):
    i = pl.program_id(0)
    active = i < nu_ref[0]
    new_expert = (i == 0) | (be_ref[i] != be_ref[jnp.maximum(i - 1, 0)])

    @pl.when(active & new_expert)
    def _():
        w2 = 2 * LANES
        src = lax.broadcasted_iota(jnp.int32, (w2, w2), 0)
        dst = lax.broadcasted_iota(jnp.int32, (w2, w2), 1)
        perm = jnp.where(src == jnp.where(dst < LANES, 2 * dst, 2 * (dst - LANES) + 1), 1.0, 0.0).astype(BF16)
        for c in range(wgu_ref.shape[1] // w2):
            res = _dot(wgu_ref[:, c * w2:(c + 1) * w2].astype(BF16), perm)
            wg_scr[:, c * LANES:(c + 1) * LANES] = res[:, :LANES].astype(BF16)
            wu_scr[:, c * LANES:(c + 1) * LANES] = res[:, LANES:].astype(BF16)
        wd_scr[...] = wd_ref[...].astype(BF16)

    @pl.when(active)
    def _():
        x = x_ref[...]
        gate = jnp.minimum(_dot(x, wg_scr[...]) + bg_ref[...], SWIGLU_LIMIT)
        up = jnp.clip(_dot(x, wu_scr[...]) + bu_ref[...], -SWIGLU_LIMIT, SWIGLU_LIMIT)
        act = (up + 1.0) * (gate * _sigmoid(SWIGLU_ALPHA * gate))
        o_ref[...] = _dot(act.astype(BF16), wd_scr[...]) + bd_ref[...]


def _moe_experts(xs, block_e, n_used, layer, w_gate_up, b_gate, b_up, w_down, b_down):
    n_rows, d = xs.shape
    _, n_exp, f, _ = w_down.shape
    tm = MOE_TILE
    n_blocks = n_rows // tm

    def row_map(i, be, nu):
        return (jnp.minimum(i, nu[0] - 1), 0)

    def w_map(i, be, nu):
        return (be[i], 0, 0)

    def lw_map(i, be, nu):
        return (layer, be[i], 0, 0)

    grid_spec = pltpu.PrefetchScalarGridSpec(
        num_scalar_prefetch=2,
        grid=(n_blocks,),
        in_specs=[pl.BlockSpec((tm, d), row_map),
                  pl.BlockSpec((None, None, d, 2 * f), lw_map),
                  pl.BlockSpec((None, 1, f), w_map), pl.BlockSpec((None, 1, f), w_map),
                  pl.BlockSpec((None, None, f, d), lw_map), pl.BlockSpec((None, 1, d), w_map)],
        out_specs=pl.BlockSpec((tm, d), row_map),
        scratch_shapes=[pltpu.VMEM((d, f), BF16), pltpu.VMEM((d, f), BF16), pltpu.VMEM((f, d), BF16)],
    )
    return pl.pallas_call(
        _moe_kernel,
        out_shape=jax.ShapeDtypeStruct((n_rows, d), F32),
        grid_spec=grid_spec,
        compiler_params=_params(1),
        name="moe_experts",
    )(block_e, n_used, xs, w_gate_up, b_gate.reshape(n_exp, 1, f), b_up.reshape(n_exp, 1, f),
      w_down, b_down.reshape(n_exp, 1, d))


def _combine_kernel(h_ref, y_ref, route_ref, g_ref, b_ref, o_ref):
    ff = None
    for k in range(TOP_K):
        part = y_ref[k] * route_ref[:, TOP_K + k:TOP_K + k + 1]
        ff = part if ff is None else ff + part
    o_ref[...] = _layer_norm(DEEPNORM_ALPHA * h_ref[...] + ff, g_ref[...], b_ref[...])


def _combine_layer(h, y4, route, ln_g, ln_b):
    t, d = h.shape
    ts = COMBINE_TILE
    return pl.pallas_call(
        _combine_kernel,
        out_shape=jax.ShapeDtypeStruct((t, d), F32),
        grid=(t // ts,),
        in_specs=[pl.BlockSpec((ts, d), lambda i: (i, 0)), pl.BlockSpec((TOP_K, ts, d), lambda i: (0, i, 0)),
                  pl.BlockSpec((ts, LANES), lambda i: (i, 0)), _const_spec((1, d)), _const_spec((1, d))],
        out_specs=pl.BlockSpec((ts, d), lambda i: (i, 0)),
        compiler_params=_params(1),
        name="moe_combine_ln",
    )(h, y4, route, ln_g.reshape(1, d), ln_b.reshape(1, d))


def _moe_layer(h2, h2b, route, cnt, layer, w_gate_up, b_gate_up, w_down, b_down, ln_g, ln_b):
    bsz, seq, d = h2.shape
    t = bsz * seq
    tm = MOE_TILE
    n_exp = w_gate_up.shape[1]
    route = route.reshape(t, LANES)
    idx = route[:, 0:TOP_K].astype(jnp.int32)
    rank = route[:, 2 * TOP_K:3 * TOP_K].astype(jnp.int32)
    counts = cnt[0, :n_exp].astype(jnp.int32)
    padded = (counts + tm - 1) // tm * tm
    pend = jnp.cumsum(padded)
    pstart = pend - padded
    dest = pstart[idx] + rank
    n_blocks = (t * TOP_K) // tm + n_exp
    n_rows = n_blocks * tm
    n_used = (pend[-1] // tm).astype(jnp.int32)
    blk = jnp.arange(n_blocks, dtype=jnp.int32)
    block_e = jnp.searchsorted(pend, blk * tm, side='right').astype(jnp.int32)
    block_e = jnp.minimum(block_e, n_exp - 1)
    block_e = jnp.where(blk < n_used, block_e, block_e[jnp.maximum(n_used - 1, 0)])
    tok = jnp.repeat(jnp.arange(t, dtype=jnp.int32), TOP_K)
    row_tok = jnp.zeros((n_rows,), jnp.int32).at[dest.reshape(-1)].set(tok)
    xs = h2b.reshape(t, d)[row_tok]
    y = _moe_experts(xs, block_e, n_used.reshape(1), layer, w_gate_up, b_gate_up[:, 0::2], b_gate_up[:, 1::2],
                     w_down, b_down)
    y4 = y[dest.T.reshape(-1)].reshape(TOP_K, t, d)
    return _combine_layer(h2.reshape(t, d), y4, route, ln_g, ln_b).reshape(bsz, seq, d)


def kernel(x, mem, pool_w_in, pool_w_grp, pool_scale, pool_w_out, ssm_w_in, ssm_conv_w, ssm_conv_b, ssm_dt_bias, ssm_a_log, ssm_d, ssm_norm_g, ssm_w_out, xa_wq, xa_wk, xa_wv, xa_wo, moe_w_router, moe_b_router, moe_w_gate_up, moe_b_gate_up, moe_w_down, moe_b_down, ln_mix_g, ln_mix_b, ln_xa_g, ln_xa_b, ln_ffn_g, ln_ffn_b):
    h = x
    for i in range(DEPTH):
        j = i // 2
        if i % 2 == 0:
            h = _pool_layer(h, pool_w_in[j], pool_w_grp[j], pool_scale[j], pool_w_out[j], ln_mix_g[i], ln_mix_b[i])
        else:
            h = _ssd_layer(h, ssm_w_in[j], ssm_conv_w[j], ssm_conv_b[j], ssm_dt_bias[j], ssm_a_log[j], ssm_d[j],
                           ssm_norm_g[j], ssm_w_out[j], ln_mix_g[i], ln_mix_b[i])
        h2, h2b, route, cnt = _xattn_layer(h, mem, xa_wq[i], xa_wk[i], xa_wv[i], xa_wo[i], ln_xa_g[i], ln_xa_b[i],
                                           moe_w_router[i], moe_b_router[i])
        h = _moe_layer(h2, h2b, route, cnt, i, moe_w_gate_up, moe_b_gate_up[i], moe_w_down, moe_b_down[i],
                       ln_ffn_g[i], ln_ffn_b[i])
    return h
```

```python
import functools

import jax
import jax.numpy as jnp
from jax import lax
from jax.experimental import pallas as pl
from jax.experimental.pallas import tpu as pltpu

F32 = jnp.float32
BF16 = jnp.bfloat16

DEPTH = 2
DEEPNORM_ALPHA = (2 * DEPTH) ** 0.25
LN_EPS = 1e-5
POOL_WINDOWS = (2, 4, 8, 16)
POOL_HALO = 16
SSM_HEAD_DIM = 64
SSM_N_GROUPS = 8
SSM_HEADS_PER_GROUP = 4
SSM_D_STATE = 128
SSM_CONV = 4
CONV_HALO = 8
SSM_NORM_EPS = 1e-5
XA_HEADS = 4
N_EXPERTS = 32
TOP_K = 4
SWIGLU_LIMIT = 7.0
SWIGLU_ALPHA = 1.702

LANES = 128
SUBLANES = 8
VMEM_LIMIT_BYTES = 56 * 1024 * 1024

POOL_TILE = 512
XATTN_TILE = 512
SSD_TILE = 256
SSD_CHUNK = 256
MOE_TILE = 512
DISPATCH_TILE = 256
COMBINE_TILE = 256
ROWS_PER_ISSUE = 8

NEG_BIG = -1e30


def _layer_norm(v, g, b):
    mu = jnp.mean(v, -1, keepdims=True)
    d = v - mu
    var = jnp.mean(d * d, -1, keepdims=True)
    return d * lax.rsqrt(var + LN_EPS) * g + b


def _dot(a, b):
    return jnp.dot(a, b, preferred_element_type=F32)


def _dot_nt(a, b):
    return lax.dot_general(a, b, (((1,), (1,)), ((), ())), preferred_element_type=F32)


def _dot_tn(a, b):
    return lax.dot_general(a, b, (((0,), (0,)), ((), ())), preferred_element_type=F32)


def _const_spec(shape):
    nd = len(shape)
    return pl.BlockSpec(shape, lambda *_: (0,) * nd, pipeline_mode=pl.Buffered(1))


def _params(n_axes):
    return pltpu.CompilerParams(dimension_semantics=("arbitrary",) * n_axes,
                                vmem_limit_bytes=VMEM_LIMIT_BYTES)


def _pool_kernel(x_ref, win_ref, wgrp_ref, scale_ref, wout_ref, g_ref, b_ref, o_ref, ext_ref, *, ts, gd):
    s = pl.program_id(1)

    @pl.when(s == 0)
    def _():
        ext_ref[0:POOL_HALO, :] = jnp.zeros((POOL_HALO, ext_ref.shape[1]), F32)

    x = x_ref[...]
    ext_ref[POOL_HALO:, :] = _dot(x.astype(BF16), win_ref[...])
    pos = lax.broadcasted_iota(jnp.int32, (ts, 1), 0) + s * ts
    mix = None
    for g, w in enumerate(POOL_WINDOWS):
        cols = slice(g * gd, (g + 1) * gd)
        e = ext_ref[:, cols]
        acc = e
        sh = 1
        while sh < w:
            acc = acc + pltpu.roll(acc, sh, axis=0)
            sh *= 2
        cnt = jnp.minimum(pos + 1, w).astype(F32)
        m = acc[POOL_HALO:, :] / cnt - e[POOL_HALO:, :]
        yg = _dot(m.astype(BF16), wgrp_ref[g]) * scale_ref[:, cols]
        part = _dot(yg.astype(BF16), wout_ref[cols, :])
        mix = part if mix is None else mix + part
    ext_ref[0:POOL_HALO, :] = ext_ref[ts:ts + POOL_HALO, :]
    o_ref[...] = _layer_norm(DEEPNORM_ALPHA * x + mix, g_ref[...], b_ref[...])


def _pool_layer(h, w_in, w_grp, scale, w_out, ln_g, ln_b):
    bsz, seq, d = h.shape
    ts = POOL_TILE
    gd = d // len(POOL_WINDOWS)
    tile = pl.BlockSpec((None, ts, d), lambda b, s: (b, s, 0))
    return pl.pallas_call(
        functools.partial(_pool_kernel, ts=ts, gd=gd),
        out_shape=jax.ShapeDtypeStruct((bsz, seq, d), F32),
        grid=(bsz, seq // ts),
        in_specs=[tile, _const_spec((d, d)), _const_spec((len(POOL_WINDOWS), gd, gd)), _const_spec((1, d)),
                  _const_spec((d, d)), _const_spec((1, d)), _const_spec((1, d))],
        out_specs=tile,
        scratch_shapes=[pltpu.VMEM((POOL_HALO + ts, d), F32)],
        compiler_params=_params(2),
        name="pool_mixer_ln",
    )(h, w_in.astype(BF16), w_grp.astype(BF16), scale.reshape(1, d), w_out.astype(BF16),
      ln_g.reshape(1, d), ln_b.reshape(1, d))


def _sigmoid(v):
    return 1.0 / (1.0 + jnp.exp(-v))


def _ssd_kernel(x_ref, wz_ref, wx_ref, wb_ref, wc_ref, wdt_ref, convw_ref, convb_ref, dtb_ref, alog_ref,
                dexp_ref, normg_ref, expand_ref, wout_ref, g_ref, b_ref, o_ref,
                xbc_scr, z_scr, actx_scr, bmat_scr, cmat_scr, y_scr, state_scr, *, ts, q, d_inner, gn):
    s = pl.program_id(1)
    conv_dim = d_inner + 2 * gn
    gw = d_inner // SSM_N_GROUPS

    @pl.when(s == 0)
    def _():
        xbc_scr[0:CONV_HALO, :] = jnp.zeros((CONV_HALO, conv_dim), F32)
        state_scr[...] = jnp.zeros(state_scr.shape, F32)

    x = x_ref[...]
    xb = x.astype(BF16)
    z_scr[...] = _dot(xb, wz_ref[...])
    xbc_scr[CONV_HALO:, 0:d_inner] = _dot(xb, wx_ref[...])
    xbc_scr[CONV_HALO:, d_inner:d_inner + gn] = _dot(xb, wb_ref[...])
    xbc_scr[CONV_HALO:, d_inner + gn:] = _dot(xb, wc_ref[...])
    dt_raw = _dot(xb, wdt_ref[...]) + dtb_ref[...]
    dtv = jnp.maximum(dt_raw, 0.0) + jnp.log1p(jnp.exp(-jnp.abs(dt_raw)))
    a_all = dtv * (-jnp.exp(alog_ref[...]))

    strip = 512
    for c in range(0, conv_dim, strip):
        cs = slice(c, c + strip)
        acc = convb_ref[:, cs]
        for k in range(SSM_CONV):
            r0 = CONV_HALO - (SSM_CONV - 1) + k
            acc = acc + convw_ref[k:k + 1, cs] * xbc_scr[r0:r0 + ts, cs]
        act = acc * _sigmoid(acc)
        if c < d_inner:
            actx_scr[:, cs] = act
        elif c < d_inner + gn:
            bmat_scr[:, c - d_inner:c - d_inner + strip] = act.astype(BF16)
        else:
            cmat_scr[:, c - d_inner - gn:c - d_inner - gn + strip] = act.astype(BF16)
    xbc_scr[0:CONV_HALO, :] = xbc_scr[ts:ts + CONV_HALO, :]

    expand = expand_ref[...]
    row_i = lax.broadcasted_iota(jnp.int32, (q, LANES), 0)
    causal = lax.broadcasted_iota(jnp.int32, (q, q), 0) >= lax.broadcasted_iota(jnp.int32, (q, q), 1)
    head_of_lane = lax.broadcasted_iota(jnp.int32, (q, gw), 1) // SSM_HEAD_DIM

    for c in range(ts // q):
        rows = slice(c * q, (c + 1) * q)
        acs = a_all[rows, :]
        sh = 1
        while sh < q:
            acs = acs + jnp.where(row_i >= sh, pltpu.roll(acs, sh, axis=0), 0.0)
            sh *= 2
        acs_t = acs.T
        a_last = acs[q - 1:q, :]
        e_in = _dot(jnp.exp(acs).astype(BF16), expand)
        dec = _dot(jnp.exp(a_last - acs).astype(BF16), expand)
        dtx = _dot(dtv[rows, :].astype(BF16), expand)
        cd = jnp.broadcast_to(jnp.exp(a_last), (8, LANES))
        cd_hi = cd.astype(BF16)
        cd_lo = (cd - cd_hi.astype(F32)).astype(BF16)
        cdx = (_dot(cd_hi, expand) + _dot(cd_lo, expand))[0:1, :]
        xd = actx_scr[rows, :] * dtx
        xdd = xd * dec
        for g in range(SSM_N_GROUPS):
            gc = slice(g * gw, (g + 1) * gw)
            nc = slice(g * SSM_D_STATE, (g + 1) * SSM_D_STATE)
            bg = bmat_scr[rows, nc]
            cg = cmat_scr[rows, nc]
            cb = _dot_nt(cg, bg)
            xg = xd[:, gc]
            yg = _dot(cg, state_scr[g].astype(BF16)) * e_in[:, gc]
            for r in range(SSM_HEADS_PER_GROUP):
                hd = g * SSM_HEADS_PER_GROUP + r
                seg = acs[:, hd:hd + 1] - acs_t[hd:hd + 1, :]
                lmat = jnp.exp(jnp.where(causal, seg, NEG_BIG))
                mh = (cb * lmat).astype(BF16)
                xm = jnp.where(head_of_lane == r, xg, 0.0).astype(BF16)
                yg = yg + _dot(mh, xm)
            y_scr[rows, gc] = yg
            state_scr[g] = state_scr[g] * cdx[:, gc] + _dot_tn(bg, xdd[:, gc].astype(BF16))

    mix = None
    for g in range(SSM_N_GROUPS):
        gc = slice(g * gw, (g + 1) * gw)
        zz = z_scr[:, gc]
        yv = (y_scr[:, gc] + dexp_ref[:, gc] * actx_scr[:, gc]) * (zz * _sigmoid(zz))
        yv = yv * lax.rsqrt(jnp.mean(yv * yv, -1, keepdims=True) + SSM_NORM_EPS) * normg_ref[:, gc]
        part = _dot(yv.astype(BF16), wout_ref[gc, :])
        mix = part if mix is None else mix + part
    o_ref[...] = _layer_norm(DEEPNORM_ALPHA * x + mix, g_ref[...], b_ref[...])


def _ssd_layer(h, w_in, conv_w, conv_b, dt_bias, a_log, d_skip, norm_g, w_out, ln_g, ln_b):
    bsz, seq, d = h.shape
    n_heads = a_log.shape[0]
    d_inner = n_heads * SSM_HEAD_DIM
    gn = SSM_N_GROUPS * SSM_D_STATE
    conv_dim = d_inner + 2 * gn
    ts, q = SSD_TILE, SSD_CHUNK
    w_in = w_in.astype(BF16)
    wz = w_in[:, :d_inner]
    wx = w_in[:, d_inner:2 * d_inner]
    wb = w_in[:, 2 * d_inner:2 * d_inner + gn]
    wc = w_in[:, 2 * d_inner + gn:2 * d_inner + 2 * gn]
    pad = LANES - n_heads
    wdt = jnp.pad(w_in[:, d_inner + conv_dim:], ((0, 0), (0, pad)))
    dtb = jnp.pad(dt_bias.astype(F32), (0, pad)).reshape(1, LANES)
    alog = jnp.pad(a_log.astype(F32), (0, pad)).reshape(1, LANES)
    dexp = jnp.repeat(d_skip.astype(F32), SSM_HEAD_DIM).reshape(1, d_inner)
    expand = (jnp.arange(LANES)[:, None] == (jnp.arange(d_inner)[None, :] // SSM_HEAD_DIM)).astype(BF16)
    tile = pl.BlockSpec((None, ts, d), lambda b, s: (b, s, 0))
    return pl.pallas_call(
        functools.partial(_ssd_kernel, ts=ts, q=q, d_inner=d_inner, gn=gn),
        out_shape=jax.ShapeDtypeStruct((bsz, seq, d), F32),
        grid=(bsz, seq // ts),
        in_specs=[tile, _const_spec((d, d_inner)), _const_spec((d, d_inner)), _const_spec((d, gn)),
                  _const_spec((d, gn)), _const_spec((d, LANES)), _const_spec((SSM_CONV, conv_dim)),
                  _const_spec((1, conv_dim)), _const_spec((1, LANES)), _const_spec((1, LANES)),
                  _const_spec((1, d_inner)), _const_spec((1, d_inner)), _const_spec((LANES, d_inner)),
                  _const_spec((d_inner, d)), _const_spec((1, d)), _const_spec((1, d))],
        out_specs=tile,
        scratch_shapes=[pltpu.VMEM((CONV_HALO + ts, conv_dim), F32), pltpu.VMEM((ts, d_inner), F32),
                        pltpu.VMEM((ts, d_inner), F32), pltpu.VMEM((ts, gn), BF16), pltpu.VMEM((ts, gn), BF16),
                        pltpu.VMEM((ts, d_inner), F32),
                        pltpu.VMEM((SSM_N_GROUPS, SSM_D_STATE, d_inner // SSM_N_GROUPS), F32)],
        compiler_params=_params(2),
        name="ssd_mixer_ln",
    )(h, wz, wx, wb, wc, wdt, conv_w.astype(F32), conv_b.reshape(1, conv_dim), dtb, alog, dexp,
      norm_g.reshape(1, d_inner), expand, w_out.astype(BF16), ln_g.reshape(1, d), ln_b.reshape(1, d))


def _xattn_kernel(h_ref, mem_ref, wq_ref, wk_ref, wv_ref, wo_ref, g_ref, b_ref, wrh_ref, wrl_ref, br_ref,
                  o_ref, route_ref, cnt_ref, k_scr, v_scr, carry_scr, *, ts, hd):
    b = pl.program_id(0)
    s = pl.program_id(1)

    @pl.when(s == 0)
    def _():
        mb = mem_ref[...].astype(BF16)
        k_scr[...] = _dot(mb, wk_ref[...]).astype(BF16)
        v_scr[...] = _dot(mb, wv_ref[...]).astype(BF16)

    @pl.when((b == 0) & (s == 0))
    def _():
        carry_scr[...] = jnp.zeros(carry_scr.shape, F32)

    h = h_ref[...]
    qv = (_dot(h.astype(BF16), wq_ref[...]) * (hd ** -0.5)).astype(BF16)
    xa = None
    for hh in range(XA_HEADS):
        cols = slice(hh * hd, (hh + 1) * hd)
        sc = _dot_nt(qv[:, cols], k_scr[:, cols])
        p = jnp.exp(sc - jnp.max(sc, -1, keepdims=True))
        o = _dot(p.astype(BF16), v_scr[:, cols]) / jnp.sum(p, -1, keepdims=True)
        part = _dot(o.astype(BF16), wo_ref[cols, :])
        xa = part if xa is None else xa + part
    h2 = _layer_norm(DEEPNORM_ALPHA * h + xa, g_ref[...], b_ref[...])
    o_ref[...] = h2
    h2_hi = h2.astype(BF16)

    h2_lo = (h2 - h2_hi.astype(F32)).astype(BF16)
    logits = _dot(h2_hi, wrh_ref[...]) + _dot(h2_lo, wrh_ref[...]) + _dot(h2_hi, wrl_ref[...]) + br_ref[...]
    lane = lax.broadcasted_iota(jnp.int32, (ts, LANES), 1).astype(F32)
    work = logits
    vals, idxs, sels = [], [], []
    for _ in range(TOP_K):
        m = jnp.max(work, -1, keepdims=True)
        ik = jnp.min(jnp.where(work == m, lane, float(LANES)), -1, keepdims=True)
        sel = lane == ik
        vals.append(m)
        idxs.append(ik)
        sels.append(sel)
        work = jnp.where(sel, -jnp.inf, work)
    exps = [jnp.exp(v - vals[0]) for v in vals]
    den = exps[0]
    for e in exps[1:]:
        den = den + e
    onehot = jnp.zeros((ts, LANES), F32)
    for sel in sels:
        onehot = onehot + sel.astype(F32)
    below = (lax.broadcasted_iota(jnp.int32, (ts, ts), 0) > lax.broadcasted_iota(jnp.int32, (ts, ts), 1))
    before = _dot(jnp.where(below, 1.0, 0.0).astype(BF16), onehot.astype(BF16)) + carry_scr[0:1, :]
    route = jnp.zeros((ts, LANES), F32)
    for k in range(TOP_K):
        rank = jnp.sum(jnp.where(sels[k], before, 0.0), -1, keepdims=True)
        route = jnp.where(lane == float(k), idxs[k], route)
        route = jnp.where(lane == float(TOP_K + k), exps[k] / den, route)
        route = jnp.where(lane == float(2 * TOP_K + k), rank, route)
    route_ref[...] = route
    carry_scr[...] = carry_scr[...] + jnp.sum(onehot, 0, keepdims=True)
    cnt_ref[...] = carry_scr[...]


def _xattn_layer(h, mem, wq, wk, wv, wo, ln_g, ln_b, w_router, b_router):
    bsz, seq, d = h.shape
    mlen = mem.shape[1]
    ts = XATTN_TILE
    hd = d // XA_HEADS
    n_exp = w_router.shape[1]
    wr = jnp.pad(w_router.astype(F32), ((0, 0), (0, LANES - n_exp)))
    wr_hi = wr.astype(BF16)
    wr_lo = (wr - wr_hi.astype(F32)).astype(BF16)
    br = jnp.pad(b_router.astype(F32), (0, LANES - n_exp), constant_values=-jnp.inf).reshape(1, LANES)
    tile = pl.BlockSpec((None, ts, d), lambda b, s: (b, s, 0))
    rtile = pl.BlockSpec((None, ts, LANES), lambda b, s: (b, s, 0))
    return pl.pallas_call(
        functools.partial(_xattn_kernel, ts=ts, hd=hd),
        out_shape=(jax.ShapeDtypeStruct((bsz, seq, d), F32),
                   jax.ShapeDtypeStruct((bsz, seq, LANES), F32), jax.ShapeDtypeStruct((8, LANES), F32)),
        grid=(bsz, seq // ts),
        in_specs=[tile, pl.BlockSpec((None, mlen, d), lambda b, s: (b, 0, 0)),
                  _const_spec((d, d)), _const_spec((d, d)), _const_spec((d, d)), _const_spec((d, d)),
                  _const_spec((1, d)), _const_spec((1, d)), _const_spec((d, LANES)), _const_spec((d, LANES)),
                  _const_spec((1, LANES))],
        out_specs=(tile, rtile, pl.BlockSpec((8, LANES), lambda b, s: (0, 0))),
        scratch_shapes=[pltpu.VMEM((mlen, d), BF16), pltpu.VMEM((mlen, d), BF16), pltpu.VMEM((8, LANES), F32)],
        compiler_params=_params(2),
        name="xattn_ln_router",
    )(h, mem, wq.astype(BF16), wk.astype(BF16), wv.astype(BF16), wo.astype(BF16),
      ln_g.reshape(1, d), ln_b.reshape(1, d), wr_hi, wr_lo, br)


def _rows_from_tiles(ref, n):
    return jnp.concatenate([ref[pl.ds(c, n, stride=SUBLANES), :] for c in range(SUBLANES)], axis=1)


def _rows_to_tiles(ref, v, n):
    for c in range(SUBLANES):
        ref[pl.ds(c, n, stride=SUBLANES), :] = v[:, c * LANES:(c + 1) * LANES]


def _moe_kernel(be_ref, nu_ref, x_ref, wgu_ref, bg_ref, bu_ref, wd_ref, bd_ref, o_ref, wg_scr, wu_scr, wd_scr,
                *, tm):
    i = pl.program_id(0)
    active = i < nu_ref[0]
    new_expert = (i == 0) | (be_ref[i] != be_ref[jnp.maximum(i - 1, 0)])

    @pl.when(active & new_expert)
    def _():
        w2 = 2 * LANES
        src = lax.broadcasted_iota(jnp.int32, (w2, w2), 0)
        dst = lax.broadcasted_iota(jnp.int32, (w2, w2), 1)
        perm = jnp.where(src == jnp.where(dst < LANES, 2 * dst, 2 * (dst - LANES) + 1), 1.0, 0.0).astype(BF16)
        for c in range(wgu_ref.shape[1] // w2):
            res = _dot(wgu_ref[:, c * w2:(c + 1) * w2].astype(BF16), perm)
            wg_scr[:, c * LANES:(c + 1) * LANES] = res[:, :LANES].astype(BF16)
            wu_scr[:, c * LANES:(c + 1) * LANES] = res[:, LANES:].astype(BF16)
        wd_scr[...] = wd_ref[...].astype(BF16)

    @pl.when(active)
    def _():
        x = _rows_from_tiles(x_ref, tm).astype(BF16)
        gate = jnp.minimum(_dot(x, wg_scr[...]) + bg_ref[...], SWIGLU_LIMIT)
        up = jnp.clip(_dot(x, wu_scr[...]) + bu_ref[...], -SWIGLU_LIMIT, SWIGLU_LIMIT)
        act = (up + 1.0) * (gate * _sigmoid(SWIGLU_ALPHA * gate))
        _rows_to_tiles(o_ref, _dot(act.astype(BF16), wd_scr[...]) + bd_ref[...], tm)

    @pl.when(jnp.logical_not(active))
    def _():
        o_ref[...] = jnp.zeros(o_ref.shape, F32)


def _moe_experts(xs, block_e, n_used, layer, w_gate_up, b_gate, b_up, w_down, b_down):
    _, n_exp, f, d = w_down.shape
    n_rows = xs.shape[0] // SUBLANES
    tm = MOE_TILE
    n_blocks = n_rows // tm

    def row_map(i, be, nu):
        return (jnp.minimum(i, nu[0] - 1), 0)

    def w_map(i, be, nu):
        return (be[i], 0, 0)

    def lw_map(i, be, nu):
        return (layer, be[i], 0, 0)

    grid_spec = pltpu.PrefetchScalarGridSpec(
        num_scalar_prefetch=2,
        grid=(n_blocks,),
        in_specs=[pl.BlockSpec((tm * SUBLANES, LANES), row_map),
                  pl.BlockSpec((None, None, d, 2 * f), lw_map),
                  pl.BlockSpec((None, 1, f), w_map), pl.BlockSpec((None, 1, f), w_map),
                  pl.BlockSpec((None, None, f, d), lw_map), pl.BlockSpec((None, 1, d), w_map)],
        out_specs=pl.BlockSpec((tm * SUBLANES, LANES), lambda i, be, nu: (i, 0)),
        scratch_shapes=[pltpu.VMEM((d, f), BF16), pltpu.VMEM((d, f), BF16), pltpu.VMEM((f, d), BF16)],
    )
    return pl.pallas_call(
        functools.partial(_moe_kernel, tm=tm),
        out_shape=jax.ShapeDtypeStruct((n_rows * SUBLANES, LANES), F32),
        grid_spec=grid_spec,
        compiler_params=_params(1),
        name="moe_experts",
    )(block_e, n_used, xs, w_gate_up, b_gate.reshape(n_exp, 1, f), b_up.reshape(n_exp, 1, f),
      w_down, b_down.reshape(n_exp, 1, d))


def _row_copy(src, src_row, dst, dst_row, sem):
    return pltpu.make_async_copy(src.at[pl.ds(pl.multiple_of(src_row * SUBLANES, SUBLANES), SUBLANES)],
                                 dst.at[pl.ds(pl.multiple_of(dst_row * SUBLANES, SUBLANES), SUBLANES)], sem)


def _dispatch_kernel(dest_ref, zblk_ref, h_ref, xs_ref, stage0, stage1, zero_scr, sem, zsem, *, ts, n_tiles, tm):
    i = pl.program_id(0)

    @pl.when(i == 0)
    def _():
        zero_scr[...] = jnp.zeros(zero_scr.shape, F32)
        for j in range(zblk_ref.shape[0]):
            @pl.when(zblk_ref[j] >= 0)
            def _():
                start = pl.multiple_of(zblk_ref[j] * (tm * SUBLANES), tm * SUBLANES)
                pltpu.make_async_copy(zero_scr, xs_ref.at[pl.ds(start, tm * SUBLANES)], zsem).start()
        for j in range(zblk_ref.shape[0]):
            @pl.when(zblk_ref[j] >= 0)
            def _():
                pltpu.make_async_copy(zero_scr, xs_ref.at[pl.ds(0, tm * SUBLANES)], zsem).wait()

    def drain(stage, s):
        for _ in range(TOP_K):
            pltpu.make_async_copy(stage, xs_ref.at[pl.ds(0, ts * SUBLANES)], sem.at[s]).wait()

    def step(stage, s, other, so):
        @pl.when(i >= 2)
        def _():
            drain(stage, s)

        _rows_to_tiles(stage, h_ref[...], ts)

        def issue(j, carry):
            for u in range(ROWS_PER_ISSUE):
                tok = j * ROWS_PER_ISSUE + u
                for k in range(TOP_K):
                    row = dest_ref[(i * ts + tok) * TOP_K + k]
                    _row_copy(stage, tok, xs_ref, row, sem.at[s]).start(priority=k % 2)
            return carry

        lax.fori_loop(0, ts // ROWS_PER_ISSUE, issue, 0)

        @pl.when(i == n_tiles - 1)
        def _():
            drain(other, so)
            drain(stage, s)

    @pl.when(i % 2 == 0)
    def _():
        step(stage0, 0, stage1, 1)

    @pl.when(i % 2 == 1)
    def _():
        step(stage1, 1, stage0, 0)


def _dispatch_rows(h, dest, zero_blocks, n_rows):
    t, d = h.shape
    ts = DISPATCH_TILE
    tm = MOE_TILE
    n_tiles = t // ts
    assert d == SUBLANES * LANES and n_tiles >= 2
    grid_spec = pltpu.PrefetchScalarGridSpec(
        num_scalar_prefetch=2,
        grid=(n_tiles,),
        in_specs=[pl.BlockSpec((ts, d), lambda i, dest, zb: (i, 0))],
        out_specs=pl.BlockSpec(memory_space=pl.ANY),
        scratch_shapes=[pltpu.VMEM((ts * SUBLANES, LANES), F32), pltpu.VMEM((ts * SUBLANES, LANES), F32),
                        pltpu.VMEM((tm * SUBLANES, LANES), F32),
                        pltpu.SemaphoreType.DMA((2,)), pltpu.SemaphoreType.DMA(())],
    )
    return pl.pallas_call(
        functools.partial(_dispatch_kernel, ts=ts, n_tiles=n_tiles, tm=tm),
        out_shape=jax.ShapeDtypeStruct((n_rows * SUBLANES, LANES), F32),
        grid_spec=grid_spec,
        compiler_params=_params(1),
        name="moe_dispatch",
    )(dest, zero_blocks, h)


def _combine_kernel(dest_ref, h_ref, route_ref, g_ref, b_ref, y_ref, o_ref, gbuf0, gbuf1, sem, *, ts, n_tiles):
    i = pl.program_id(0)

    def issue(tile, gbuf, s):
        def body(j, carry):
            for u in range(ROWS_PER_ISSUE):
                tok = j * ROWS_PER_ISSUE + u
                for k in range(TOP_K):
                    row = dest_ref[(tile * ts + tok) * TOP_K + k]
                    _row_copy(y_ref, row, gbuf.at[k], tok, sem.at[s]).start(priority=k % 2)
            return carry

        lax.fori_loop(0, ts // ROWS_PER_ISSUE, body, 0)

    def step(gbuf, s, nxt, sn):
        @pl.when(i + 1 < n_tiles)
        def _():
            issue(i + 1, nxt, sn)

        for k in range(TOP_K):
            pltpu.make_async_copy(y_ref.at[pl.ds(0, ts * SUBLANES)], gbuf.at[k], sem.at[s]).wait()
        parts = []
        for c in range(SUBLANES):
            acc = None
            for k in range(TOP_K):
                piece = gbuf[k, pl.ds(c, ts, stride=SUBLANES), :] * route_ref[:, TOP_K + k:TOP_K + k + 1]
                acc = piece if acc is None else acc + piece
            parts.append(acc)
        ff = jnp.concatenate(parts, axis=1)
        o_ref[...] = _layer_norm(DEEPNORM_ALPHA * h_ref[...] + ff, g_ref[...], b_ref[...])

    @pl.when(i == 0)
    def _():
        issue(0, gbuf0, 0)

    @pl.when(i % 2 == 0)
    def _():
        step(gbuf0, 0, gbuf1, 1)

    @pl.when(i % 2 == 1)
    def _():
        step(gbuf1, 1, gbuf0, 0)


def _combine_layer(h, y, dest, route, ln_g, ln_b):
    t, d = h.shape
    ts = COMBINE_TILE
    n_tiles = t // ts
    grid_spec = pltpu.PrefetchScalarGridSpec(
        num_scalar_prefetch=1,
        grid=(n_tiles,),
        in_specs=[pl.BlockSpec((ts, d), lambda i, dest: (i, 0)), pl.BlockSpec((ts, LANES), lambda i, dest: (i, 0)),
                  pl.BlockSpec((1, d), lambda i, dest: (0, 0)), pl.BlockSpec((1, d), lambda i, dest: (0, 0)),
                  pl.BlockSpec(memory_space=pl.ANY)],
        out_specs=pl.BlockSpec((ts, d), lambda i, dest: (i, 0)),
        scratch_shapes=[pltpu.VMEM((TOP_K, ts * SUBLANES, LANES), F32),
                        pltpu.VMEM((TOP_K, ts * SUBLANES, LANES), F32), pltpu.SemaphoreType.DMA((2,))],
    )
    return pl.pallas_call(
        functools.partial(_combine_kernel, ts=ts, n_tiles=n_tiles),
        out_shape=jax.ShapeDtypeStruct((t, d), F32),
        grid_spec=grid_spec,
        compiler_params=_params(1),
        name="moe_combine_ln",
    )(dest, h, route, ln_g.reshape(1, d), ln_b.reshape(1, d), y)


def _moe_layer(h2, route, cnt, layer, w_gate_up, b_gate_up, w_down, b_down, ln_g, ln_b):
    bsz, seq, d = h2.shape
    t = bsz * seq
    tm = MOE_TILE
    n_exp = w_gate_up.shape[1]
    h2 = h2.reshape(t, d)
    route = route.reshape(t, LANES)
    idx = route[:, 0:TOP_K].astype(jnp.int32)
    rank = route[:, 2 * TOP_K:3 * TOP_K].astype(jnp.int32)
    counts = cnt[0, :n_exp].astype(jnp.int32)
    padded = (counts + tm - 1) // tm * tm
    pend = jnp.cumsum(padded)
    pstart = pend - padded
    dest = (pstart[idx] + rank).reshape(-1)
    n_blocks = (t * TOP_K) // tm + n_exp
    n_used = (pend[-1] // tm).astype(jnp.int32)
    blk = jnp.arange(n_blocks, dtype=jnp.int32)
    block_e = jnp.sum((pend[None, :] <= (blk * tm)[:, None]).astype(jnp.int32), axis=1)
    block_e = jnp.minimum(block_e, n_exp - 1)
    block_e = jnp.where(blk < n_used, block_e, block_e[jnp.maximum(n_used - 1, 0)])
    last_blk = jnp.where(padded > 0, pend // tm - 1, -1)
    tail_blk = n_used + jnp.arange(n_exp, dtype=jnp.int32)
    zero_blocks = jnp.concatenate([last_blk, jnp.where(tail_blk < n_blocks, tail_blk, -1)]).astype(jnp.int32)
    xs = _dispatch_rows(h2, dest, zero_blocks, n_blocks * tm)
    y = _moe_experts(xs, block_e, n_used.reshape(1), layer, w_gate_up, b_gate_up[:, 0::2], b_gate_up[:, 1::2],
                     w_down, b_down)
    return _combine_layer(h2, y, dest, route, ln_g, ln_b).reshape(bsz, seq, d)


def kernel(x, mem, pool_w_in, pool_w_grp, pool_scale, pool_w_out, ssm_w_in, ssm_conv_w, ssm_conv_b, ssm_dt_bias, ssm_a_log, ssm_d, ssm_norm_g, ssm_w_out, xa_wq, xa_wk, xa_wv, xa_wo, moe_w_router, moe_b_router, moe_w_gate_up, moe_b_gate_up, moe_w_down, moe_b_down, ln_mix_g, ln_mix_b, ln_xa_g, ln_xa_b, ln_ffn_g, ln_ffn_b):
    h = x
    for i in range(DEPTH):
        j = i // 2
        if i % 2 == 0:
            h = _pool_layer(h, pool_w_in[j], pool_w_grp[j], pool_scale[j], pool_w_out[j], ln_mix_g[i], ln_mix_b[i])
        else:
            h = _ssd_layer(h, ssm_w_in[j], ssm_conv_w[j], ssm_conv_b[j], ssm_dt_bias[j], ssm_a_log[j], ssm_d[j],
                           ssm_norm_g[j], ssm_w_out[j], ln_mix_g[i], ln_mix_b[i])
        h2, route, cnt = _xattn_layer(h, mem, xa_wq[i], xa_wk[i], xa_wv[i], xa_wo[i], ln_xa_g[i], ln_xa_b[i],
                                      moe_w_router[i], moe_b_router[i])
        h = _moe_layer(h2, route, cnt, i, moe_w_gate_up, moe_b_gate_up[i], moe_w_down, moe_b_down[i],
                       ln_ffn_g[i], ln_ffn_b[i])
    return h
```

```python
import functools

import jax
import jax.numpy as jnp
from jax import lax
from jax.experimental import pallas as pl
from jax.experimental.pallas import tpu as pltpu

F32 = jnp.float32
BF16 = jnp.bfloat16

DEPTH = 2
DEEPNORM_ALPHA = (2 * DEPTH) ** 0.25
LN_EPS = 1e-5
POOL_WINDOWS = (2, 4, 8, 16)
POOL_HALO = 16
SSM_HEAD_DIM = 64
SSM_N_GROUPS = 8
SSM_HEADS_PER_GROUP = 4
SSM_D_STATE = 128
SSM_CONV = 4
CONV_HALO = 8
SSM_NORM_EPS = 1e-5
XA_HEADS = 4
N_EXPERTS = 32
TOP_K = 4
SWIGLU_LIMIT = 7.0
SWIGLU_ALPHA = 1.702

LANES = 128
SUBLANES = 8
VMEM_LIMIT_BYTES = 56 * 1024 * 1024

POOL_TILE = 512
XATTN_TILE = 512
SSD_TILE = 256
SSD_CHUNK = 128
MOE_TILE = 512
DISPATCH_TILE = 256
COMBINE_TILE = 256
ROWS_PER_ISSUE = 8

NEG_BIG = -1e30


def _layer_norm(v, g, b):
    mu = jnp.mean(v, -1, keepdims=True)
    d = v - mu
    var = jnp.mean(d * d, -1, keepdims=True)
    return d * lax.rsqrt(var + LN_EPS) * g + b


def _dot(a, b):
    return jnp.dot(a, b, preferred_element_type=F32)


def _dot_nt(a, b):
    return lax.dot_general(a, b, (((1,), (1,)), ((), ())), preferred_element_type=F32)


def _dot_tn(a, b):
    return lax.dot_general(a, b, (((0,), (0,)), ((), ())), preferred_element_type=F32)


def _const_spec(shape):
    nd = len(shape)
    return pl.BlockSpec(shape, lambda *_: (0,) * nd, pipeline_mode=pl.Buffered(1))


def _params(n_axes):
    return pltpu.CompilerParams(dimension_semantics=("arbitrary",) * n_axes,
                                vmem_limit_bytes=VMEM_LIMIT_BYTES)


def _pool_kernel(x_ref, win_ref, wgrp_ref, scale_ref, wout_ref, g_ref, b_ref, o_ref, ext_ref, *, ts, gd):
    s = pl.program_id(1)

    @pl.when(s == 0)
    def _():
        ext_ref[0:POOL_HALO, :] = jnp.zeros((POOL_HALO, ext_ref.shape[1]), F32)

    x = x_ref[...]
    ext_ref[POOL_HALO:, :] = _dot(x.astype(BF16), win_ref[...])
    pos = lax.broadcasted_iota(jnp.int32, (ts, 1), 0) + s * ts
    mix = None
    for g, w in enumerate(POOL_WINDOWS):
        cols = slice(g * gd, (g + 1) * gd)
        e = ext_ref[:, cols]
        acc = e
        sh = 1
        while sh < w:
            acc = acc + pltpu.roll(acc, sh, axis=0)
            sh *= 2
        cnt = jnp.minimum(pos + 1, w).astype(F32)
        m = acc[POOL_HALO:, :] / cnt - e[POOL_HALO:, :]
        yg = _dot(m.astype(BF16), wgrp_ref[g]) * scale_ref[:, cols]
        part = _dot(yg.astype(BF16), wout_ref[cols, :])
        mix = part if mix is None else mix + part
    ext_ref[0:POOL_HALO, :] = ext_ref[ts:ts + POOL_HALO, :]
    o_ref[...] = _layer_norm(DEEPNORM_ALPHA * x + mix, g_ref[...], b_ref[...])


def _pool_layer(h, w_in, w_grp, scale, w_out, ln_g, ln_b):
    bsz, seq, d = h.shape
    ts = POOL_TILE
    gd = d // len(POOL_WINDOWS)
    tile = pl.BlockSpec((None, ts, d), lambda b, s: (b, s, 0))
    return pl.pallas_call(
        functools.partial(_pool_kernel, ts=ts, gd=gd),
        out_shape=jax.ShapeDtypeStruct((bsz, seq, d), F32),
        grid=(bsz, seq // ts),
        in_specs=[tile, _const_spec((d, d)), _const_spec((len(POOL_WINDOWS), gd, gd)), _const_spec((1, d)),
                  _const_spec((d, d)), _const_spec((1, d)), _const_spec((1, d))],
        out_specs=tile,
        scratch_shapes=[pltpu.VMEM((POOL_HALO + ts, d), F32)],
        compiler_params=_params(2),
        name="pool_mixer_ln",
    )(h, w_in.astype(BF16), w_grp.astype(BF16), scale.reshape(1, d), w_out.astype(BF16),
      ln_g.reshape(1, d), ln_b.reshape(1, d))


def _sigmoid(v):
    return 1.0 / (1.0 + jnp.exp(-v))


def _ssd_kernel(x_ref, wz_ref, wx_ref, wb_ref, wc_ref, wdt_ref, convw_ref, convb_ref, dtb_ref, alog_ref,
                dexp_ref, normg_ref, expand_ref, wout_ref, g_ref, b_ref, o_ref,
                xbc_scr, z_scr, actx_scr, bmat_scr, cmat_scr, y_scr, state_scr, *, ts, q, d_inner, gn):
    s = pl.program_id(1)
    conv_dim = d_inner + 2 * gn
    gw = d_inner // SSM_N_GROUPS

    @pl.when(s == 0)
    def _():
        xbc_scr[:, 0:CONV_HALO, :] = jnp.zeros((conv_dim // LANES, CONV_HALO, LANES), F32)
        state_scr[...] = jnp.zeros(state_scr.shape, F32)

    x = x_ref[...]
    xb = x.astype(BF16)
    z_scr[...] = _dot(xb, wz_ref[...])
    strip = 512
    for w_ref, base in ((wx_ref, 0), (wb_ref, d_inner), (wc_ref, d_inner + gn)):
        for c in range(0, w_ref.shape[1], strip):
            res = _dot(xb, w_ref[:, c:c + strip])
            for j in range(strip // LANES):
                xbc_scr[(base + c) // LANES + j, CONV_HALO:, :] = res[:, j * LANES:(j + 1) * LANES]
    dt_raw = _dot(xb, wdt_ref[...]) + dtb_ref[...]
    dtv = jnp.maximum(dt_raw, 0.0) + jnp.log1p(jnp.exp(-jnp.abs(dt_raw)))
    a_all = dtv * (-jnp.exp(alog_ref[...]))

    for j in range(conv_dim // LANES):
        cs = slice(j * LANES, (j + 1) * LANES)
        acc = convb_ref[:, cs]
        for k in range(SSM_CONV):
            r0 = CONV_HALO - (SSM_CONV - 1) + k
            acc = acc + convw_ref[k:k + 1, cs] * xbc_scr[j, r0:r0 + ts, :]
        act = acc * _sigmoid(acc)
        c = j * LANES
        if c < d_inner:
            actx_scr[:, cs] = act
        elif c < d_inner + gn:
            bmat_scr[:, c - d_inner:c - d_inner + LANES] = act.astype(BF16)
        else:
            cmat_scr[:, c - d_inner - gn:c - d_inner - gn + LANES] = act.astype(BF16)
    xbc_scr[:, 0:CONV_HALO, :] = xbc_scr[:, ts:ts + CONV_HALO, :]

    expand = expand_ref[...]
    row_i = lax.broadcasted_iota(jnp.int32, (q, LANES), 0)
    causal = lax.broadcasted_iota(jnp.int32, (q, q), 0) >= lax.broadcasted_iota(jnp.int32, (q, q), 1)
    head_of_lane = lax.broadcasted_iota(jnp.int32, (q, gw), 1) // SSM_HEAD_DIM
    head_mask = [jnp.where(head_of_lane == r, 1.0, 0.0).astype(BF16) for r in range(SSM_HEADS_PER_GROUP)]

    for c in range(ts // q):
        rows = slice(c * q, (c + 1) * q)
        acs = a_all[rows, :]
        sh = 1
        while sh < q:
            acs = acs + jnp.where(row_i >= sh, pltpu.roll(acs, sh, axis=0), 0.0)
            sh *= 2
        acs_t = acs.T
        a_last = acs[q - 1:q, :]
        e_in = _dot(jnp.exp(acs).astype(BF16), expand)
        dec = _dot(jnp.exp(a_last - acs).astype(BF16), expand)
        dtx = _dot(dtv[rows, :].astype(BF16), expand)
        cd = jnp.broadcast_to(jnp.exp(a_last), (8, LANES))
        cd_hi = cd.astype(BF16)
        cd_lo = (cd - cd_hi.astype(F32)).astype(BF16)
        cdx = (_dot(cd_hi, expand) + _dot(cd_lo, expand))[0:1, :]
        xd = actx_scr[rows, :] * dtx
        xdd = xd * dec
        xdb = xd.astype(BF16)
        for g in range(SSM_N_GROUPS):
            gc = slice(g * gw, (g + 1) * gw)
            nc = slice(g * SSM_D_STATE, (g + 1) * SSM_D_STATE)
            bg = bmat_scr[rows, nc]
            cg = cmat_scr[rows, nc]
            cb = _dot_nt(cg, bg)
            xg = xdb[:, gc]
            mhs, xms = [], []
            for r in range(SSM_HEADS_PER_GROUP):
                hd = g * SSM_HEADS_PER_GROUP + r
                seg = acs[:, hd:hd + 1] - acs_t[hd:hd + 1, :]
                lmat = jnp.exp(jnp.where(causal, seg, NEG_BIG))
                mhs.append((cb * lmat).astype(BF16))
                xms.append(xg * head_mask[r])
            yg = (_dot(cg, state_scr[g].astype(BF16)) * e_in[:, gc]
                  + _dot(jnp.concatenate(mhs, axis=1), jnp.concatenate(xms, axis=0)))
            y_scr[rows, gc] = yg
            state_scr[g] = state_scr[g] * cdx[:, gc] + _dot_tn(bg, xdd[:, gc].astype(BF16))

    mix = None
    for g in range(SSM_N_GROUPS):
        gc = slice(g * gw, (g + 1) * gw)
        zz = z_scr[:, gc]
        yv = (y_scr[:, gc] + dexp_ref[:, gc] * actx_scr[:, gc]) * (zz * _sigmoid(zz))
        yv = yv * lax.rsqrt(jnp.mean(yv * yv, -1, keepdims=True) + SSM_NORM_EPS) * normg_ref[:, gc]
        part = _dot(yv.astype(BF16), wout_ref[gc, :])
        mix = part if mix is None else mix + part
    o_ref[...] = _layer_norm(DEEPNORM_ALPHA * x + mix, g_ref[...], b_ref[...])


def _ssd_layer(h, w_in, conv_w, conv_b, dt_bias, a_log, d_skip, norm_g, w_out, ln_g, ln_b):
    bsz, seq, d = h.shape
    n_heads = a_log.shape[0]
    d_inner = n_heads * SSM_HEAD_DIM
    gn = SSM_N_GROUPS * SSM_D_STATE
    conv_dim = d_inner + 2 * gn
    ts, q = SSD_TILE, SSD_CHUNK
    w_in = w_in.astype(BF16)
    wz = w_in[:, :d_inner]
    wx = w_in[:, d_inner:2 * d_inner]
    wb = w_in[:, 2 * d_inner:2 * d_inner + gn]
    wc = w_in[:, 2 * d_inner + gn:2 * d_inner + 2 * gn]
    pad = LANES - n_heads
    wdt = jnp.pad(w_in[:, d_inner + conv_dim:], ((0, 0), (0, pad)))
    dtb = jnp.pad(dt_bias.astype(F32), (0, pad)).reshape(1, LANES)
    alog = jnp.pad(a_log.astype(F32), (0, pad)).reshape(1, LANES)
    dexp = jnp.repeat(d_skip.astype(F32), SSM_HEAD_DIM).reshape(1, d_inner)
    expand = (jnp.arange(LANES)[:, None] == (jnp.arange(d_inner)[None, :] // SSM_HEAD_DIM)).astype(BF16)
    tile = pl.BlockSpec((None, ts, d), lambda b, s: (b, s, 0))
    return pl.pallas_call(
        functools.partial(_ssd_kernel, ts=ts, q=q, d_inner=d_inner, gn=gn),
        out_shape=jax.ShapeDtypeStruct((bsz, seq, d), F32),
        grid=(bsz, seq // ts),
        in_specs=[tile, _const_spec((d, d_inner)), _const_spec((d, d_inner)), _const_spec((d, gn)),
                  _const_spec((d, gn)), _const_spec((d, LANES)), _const_spec((SSM_CONV, conv_dim)),
                  _const_spec((1, conv_dim)), _const_spec((1, LANES)), _const_spec((1, LANES)),
                  _const_spec((1, d_inner)), _const_spec((1, d_inner)), _const_spec((LANES, d_inner)),
                  _const_spec((d_inner, d)), _const_spec((1, d)), _const_spec((1, d))],
        out_specs=tile,
        scratch_shapes=[pltpu.VMEM((conv_dim // LANES, CONV_HALO + ts, LANES), F32), pltpu.VMEM((ts, d_inner), F32),
                        pltpu.VMEM((ts, d_inner), F32), pltpu.VMEM((ts, gn), BF16), pltpu.VMEM((ts, gn), BF16),
                        pltpu.VMEM((ts, d_inner), F32),
                        pltpu.VMEM((SSM_N_GROUPS, SSM_D_STATE, d_inner // SSM_N_GROUPS), F32)],
        compiler_params=_params(2),
        name="ssd_mixer_ln",
    )(h, wz, wx, wb, wc, wdt, conv_w.astype(F32), conv_b.reshape(1, conv_dim), dtb, alog, dexp,
      norm_g.reshape(1, d_inner), expand, w_out.astype(BF16), ln_g.reshape(1, d), ln_b.reshape(1, d))


def _xattn_kernel(h_ref, mem_ref, wq_ref, wk_ref, wv_ref, wo_ref, g_ref, b_ref, wrh_ref, wrl_ref, br_ref,
                  o_ref, route_ref, cnt_ref, k_scr, v_scr, carry_scr, *, ts, hd):
    b = pl.program_id(0)
    s = pl.program_id(1)

    @pl.when(s == 0)
    def _():
        mb = mem_ref[...].astype(BF16)
        k_scr[...] = _dot(mb, wk_ref[...]).astype(BF16)
        v_scr[...] = _dot(mb, wv_ref[...]).astype(BF16)

    @pl.when((b == 0) & (s == 0))
    def _():
        carry_scr[...] = jnp.zeros(carry_scr.shape, F32)

    h = h_ref[...]
    qv = (_dot(h.astype(BF16), wq_ref[...]) * (hd ** -0.5)).astype(BF16)
    xa = None
    for hh in range(XA_HEADS):
        cols = slice(hh * hd, (hh + 1) * hd)
        sc = _dot_nt(qv[:, cols], k_scr[:, cols])
        p = jnp.exp(sc - jnp.max(sc, -1, keepdims=True))
        o = _dot(p.astype(BF16), v_scr[:, cols]) / jnp.sum(p, -1, keepdims=True)
        part = _dot(o.astype(BF16), wo_ref[cols, :])
        xa = part if xa is None else xa + part
    h2 = _layer_norm(DEEPNORM_ALPHA * h + xa, g_ref[...], b_ref[...])
    o_ref[...] = h2
    h2_hi = h2.astype(BF16)

    h2_lo = (h2 - h2_hi.astype(F32)).astype(BF16)
    logits = _dot(h2_hi, wrh_ref[...]) + _dot(h2_lo, wrh_ref[...]) + _dot(h2_hi, wrl_ref[...]) + br_ref[...]
    lane = lax.broadcasted_iota(jnp.int32, (ts, LANES), 1).astype(F32)
    work = logits
    vals, idxs, sels = [], [], []
    for _ in range(TOP_K):
        m = jnp.max(work, -1, keepdims=True)
        ik = jnp.min(jnp.where(work == m, lane, float(LANES)), -1, keepdims=True)
        sel = lane == ik
        vals.append(m)
        idxs.append(ik)
        sels.append(sel)
        work = jnp.where(sel, -jnp.inf, work)
    exps = [jnp.exp(v - vals[0]) for v in vals]
    den = exps[0]
    for e in exps[1:]:
        den = den + e
    onehot = jnp.zeros((ts, LANES), F32)
    for sel in sels:
        onehot = onehot + sel.astype(F32)
    below = (lax.broadcasted_iota(jnp.int32, (ts, ts), 0) > lax.broadcasted_iota(jnp.int32, (ts, ts), 1))
    before = _dot(jnp.where(below, 1.0, 0.0).astype(BF16), onehot.astype(BF16)) + carry_scr[0:1, :]
    route = jnp.zeros((ts, LANES), F32)
    for k in range(TOP_K):
        rank = jnp.sum(jnp.where(sels[k], before, 0.0), -1, keepdims=True)
        route = jnp.where(lane == float(k), idxs[k], route)
        route = jnp.where(lane == float(TOP_K + k), exps[k] / den, route)
        route = jnp.where(lane == float(2 * TOP_K + k), rank, route)
    route_ref[...] = route
    carry_scr[...] = carry_scr[...] + jnp.sum(onehot, 0, keepdims=True)
    cnt_ref[...] = carry_scr[...]


def _xattn_layer(h, mem, wq, wk, wv, wo, ln_g, ln_b, w_router, b_router):
    bsz, seq, d = h.shape
    mlen = mem.shape[1]
    ts = XATTN_TILE
    hd = d // XA_HEADS
    n_exp = w_router.shape[1]
    wr = jnp.pad(w_router.astype(F32), ((0, 0), (0, LANES - n_exp)))
    wr_hi = wr.astype(BF16)
    wr_lo = (wr - wr_hi.astype(F32)).astype(BF16)
    br = jnp.pad(b_router.astype(F32), (0, LANES - n_exp), constant_values=-jnp.inf).reshape(1, LANES)
    tile = pl.BlockSpec((None, ts, d), lambda b, s: (b, s, 0))
    rtile = pl.BlockSpec((None, ts, LANES), lambda b, s: (b, s, 0))
    return pl.pallas_call(
        functools.partial(_xattn_kernel, ts=ts, hd=hd),
        out_shape=(jax.ShapeDtypeStruct((bsz, seq, d), F32),
                   jax.ShapeDtypeStruct((bsz, seq, LANES), F32), jax.ShapeDtypeStruct((8, LANES), F32)),
        grid=(bsz, seq // ts),
        in_specs=[tile, pl.BlockSpec((None, mlen, d), lambda b, s: (b, 0, 0)),
                  _const_spec((d, d)), _const_spec((d, d)), _const_spec((d, d)), _const_spec((d, d)),
                  _const_spec((1, d)), _const_spec((1, d)), _const_spec((d, LANES)), _const_spec((d, LANES)),
                  _const_spec((1, LANES))],
        out_specs=(tile, rtile, pl.BlockSpec((8, LANES), lambda b, s: (0, 0))),
        scratch_shapes=[pltpu.VMEM((mlen, d), BF16), pltpu.VMEM((mlen, d), BF16), pltpu.VMEM((8, LANES), F32)],
        compiler_params=_params(2),
        name="xattn_ln_router",
    )(h, mem, wq.astype(BF16), wk.astype(BF16), wv.astype(BF16), wo.astype(BF16),
      ln_g.reshape(1, d), ln_b.reshape(1, d), wr_hi, wr_lo, br)


def _rows_from_tiles(ref, n):
    return jnp.concatenate([ref[pl.ds(c, n, stride=SUBLANES), :] for c in range(SUBLANES)], axis=1)


def _rows_to_tiles(ref, v, n):
    for c in range(SUBLANES):
        ref[pl.ds(c, n, stride=SUBLANES), :] = v[:, c * LANES:(c + 1) * LANES]


def _moe_kernel(be_ref, nu_ref, x_ref, wgu_ref, bg_ref, bu_ref, wd_ref, bd_ref, o_ref, wg_scr, wu_scr, wd_scr,
                *, tm):
    i = pl.program_id(0)
    active = i < nu_ref[0]
    new_expert = (i == 0) | (be_ref[i] != be_ref[jnp.maximum(i - 1, 0)])

    @pl.when(active & new_expert)
    def _():
        w2 = 2 * LANES
        src = lax.broadcasted_iota(jnp.int32, (w2, w2), 0)
        dst = lax.broadcasted_iota(jnp.int32, (w2, w2), 1)
        perm = jnp.where(src == jnp.where(dst < LANES, 2 * dst, 2 * (dst - LANES) + 1), 1.0, 0.0).astype(BF16)
        for c in range(wgu_ref.shape[1] // w2):
            res = _dot(wgu_ref[:, c * w2:(c + 1) * w2].astype(BF16), perm)
            wg_scr[:, c * LANES:(c + 1) * LANES] = res[:, :LANES].astype(BF16)
            wu_scr[:, c * LANES:(c + 1) * LANES] = res[:, LANES:].astype(BF16)
        wd_scr[...] = wd_ref[...].astype(BF16)

    @pl.when(active)
    def _():
        x = _rows_from_tiles(x_ref, tm).astype(BF16)
        gate = jnp.minimum(_dot(x, wg_scr[...]) + bg_ref[...], SWIGLU_LIMIT)
        up = jnp.clip(_dot(x, wu_scr[...]) + bu_ref[...], -SWIGLU_LIMIT, SWIGLU_LIMIT)
        act = (up + 1.0) * (gate * _sigmoid(SWIGLU_ALPHA * gate))
        _rows_to_tiles(o_ref, _dot(act.astype(BF16), wd_scr[...]) + bd_ref[...], tm)

    @pl.when(jnp.logical_not(active))
    def _():
        o_ref[...] = jnp.zeros(o_ref.shape, F32)


def _moe_experts(xs, block_e, n_used, layer, w_gate_up, b_gate, b_up, w_down, b_down):
    _, n_exp, f, d = w_down.shape
    n_rows = xs.shape[0] // SUBLANES
    tm = MOE_TILE
    n_blocks = n_rows // tm

    def row_map(i, be, nu):
        return (jnp.minimum(i, nu[0] - 1), 0)

    def w_map(i, be, nu):
        return (be[i], 0, 0)

    def lw_map(i, be, nu):
        return (layer, be[i], 0, 0)

    grid_spec = pltpu.PrefetchScalarGridSpec(
        num_scalar_prefetch=2,
        grid=(n_blocks,),
        in_specs=[pl.BlockSpec((tm * SUBLANES, LANES), row_map),
                  pl.BlockSpec((None, None, d, 2 * f), lw_map),
                  pl.BlockSpec((None, 1, f), w_map), pl.BlockSpec((None, 1, f), w_map),
                  pl.BlockSpec((None, None, f, d), lw_map), pl.BlockSpec((None, 1, d), w_map)],
        out_specs=pl.BlockSpec((tm * SUBLANES, LANES), lambda i, be, nu: (i, 0)),
        scratch_shapes=[pltpu.VMEM((d, f), BF16), pltpu.VMEM((d, f), BF16), pltpu.VMEM((f, d), BF16)],
    )
    return pl.pallas_call(
        functools.partial(_moe_kernel, tm=tm),
        out_shape=jax.ShapeDtypeStruct((n_rows * SUBLANES, LANES), F32),
        grid_spec=grid_spec,
        compiler_params=_params(1),
        name="moe_experts",
    )(block_e, n_used, xs, w_gate_up, b_gate.reshape(n_exp, 1, f), b_up.reshape(n_exp, 1, f),
      w_down, b_down.reshape(n_exp, 1, d))


def _row_copy(src, src_row, dst, dst_row, sem):
    return pltpu.make_async_copy(src.at[pl.ds(pl.multiple_of(src_row * SUBLANES, SUBLANES), SUBLANES)],
                                 dst.at[pl.ds(pl.multiple_of(dst_row * SUBLANES, SUBLANES), SUBLANES)], sem)


def _dispatch_kernel(dest_ref, zblk_ref, h_ref, xs_ref, stage0, stage1, zero_scr, sem, zsem, *, ts, n_tiles, tm):
    i = pl.program_id(0)

    @pl.when(i == 0)
    def _():
        zero_scr[...] = jnp.zeros(zero_scr.shape, F32)
        for j in range(zblk_ref.shape[0]):
            @pl.when(zblk_ref[j] >= 0)
            def _():
                start = pl.multiple_of(zblk_ref[j] * (tm * SUBLANES), tm * SUBLANES)
                pltpu.make_async_copy(zero_scr, xs_ref.at[pl.ds(start, tm * SUBLANES)], zsem).start()
        for j in range(zblk_ref.shape[0]):
            @pl.when(zblk_ref[j] >= 0)
            def _():
                pltpu.make_async_copy(zero_scr, xs_ref.at[pl.ds(0, tm * SUBLANES)], zsem).wait()

    def drain(stage, s):
        for _ in range(TOP_K):
            pltpu.make_async_copy(stage, xs_ref.at[pl.ds(0, ts * SUBLANES)], sem.at[s]).wait()

    def step(stage, s, other, so):
        @pl.when(i >= 2)
        def _():
            drain(stage, s)

        _rows_to_tiles(stage, h_ref[...], ts)

        def issue(j, carry):
            for u in range(ROWS_PER_ISSUE):
                tok = j * ROWS_PER_ISSUE + u
                for k in range(TOP_K):
                    row = dest_ref[(i * ts + tok) * TOP_K + k]
                    _row_copy(stage, tok, xs_ref, row, sem.at[s]).start(priority=k % 2)
            return carry

        lax.fori_loop(0, ts // ROWS_PER_ISSUE, issue, 0)

        @pl.when(i == n_tiles - 1)
        def _():
            drain(other, so)
            drain(stage, s)

    @pl.when(i % 2 == 0)
    def _():
        step(stage0, 0, stage1, 1)

    @pl.when(i % 2 == 1)
    def _():
        step(stage1, 1, stage0, 0)


def _dispatch_rows(h, dest, zero_blocks, n_rows):
    t, d = h.shape
    ts = DISPATCH_TILE
    tm = MOE_TILE
    n_tiles = t // ts
    assert d == SUBLANES * LANES and n_tiles >= 2
    grid_spec = pltpu.PrefetchScalarGridSpec(
        num_scalar_prefetch=2,
        grid=(n_tiles,),
        in_specs=[pl.BlockSpec((ts, d), lambda i, dest, zb: (i, 0))],
        out_specs=pl.BlockSpec(memory_space=pl.ANY),
        scratch_shapes=[pltpu.VMEM((ts * SUBLANES, LANES), F32), pltpu.VMEM((ts * SUBLANES, LANES), F32),
                        pltpu.VMEM((tm * SUBLANES, LANES), F32),
                        pltpu.SemaphoreType.DMA((2,)), pltpu.SemaphoreType.DMA(())],
    )
    return pl.pallas_call(
        functools.partial(_dispatch_kernel, ts=ts, n_tiles=n_tiles, tm=tm),
        out_shape=jax.ShapeDtypeStruct((n_rows * SUBLANES, LANES), F32),
        grid_spec=grid_spec,
        compiler_params=_params(1),
        name="moe_dispatch",
    )(dest, zero_blocks, h)


def _combine_kernel(dest_ref, h_ref, route_ref, g_ref, b_ref, y_ref, o_ref, gbuf0, gbuf1, sem, *, ts, n_tiles):
    i = pl.program_id(0)

    def issue(tile, gbuf, s):
        def body(j, carry):
            for u in range(ROWS_PER_ISSUE):
                tok = j * ROWS_PER_ISSUE + u
                for k in range(TOP_K):
                    row = dest_ref[(tile * ts + tok) * TOP_K + k]
                    _row_copy(y_ref, row, gbuf.at[k], tok, sem.at[s]).start(priority=k % 2)
            return carry

        lax.fori_loop(0, ts // ROWS_PER_ISSUE, body, 0)

    def step(gbuf, s, nxt, sn):
        @pl.when(i + 1 < n_tiles)
        def _():
            issue(i + 1, nxt, sn)

        for k in range(TOP_K):
            pltpu.make_async_copy(y_ref.at[pl.ds(0, ts * SUBLANES)], gbuf.at[k], sem.at[s]).wait()
        parts = []
        for c in range(SUBLANES):
            acc = None
            for k in range(TOP_K):
                piece = gbuf[k, pl.ds(c, ts, stride=SUBLANES), :] * route_ref[:, TOP_K + k:TOP_K + k + 1]
                acc = piece if acc is None else acc + piece
            parts.append(acc)
        ff = jnp.concatenate(parts, axis=1)
        o_ref[...] = _layer_norm(DEEPNORM_ALPHA * h_ref[...] + ff, g_ref[...], b_ref[...])

    @pl.when(i == 0)
    def _():
        issue(0, gbuf0, 0)

    @pl.when(i % 2 == 0)
    def _():
        step(gbuf0, 0, gbuf1, 1)

    @pl.when(i % 2 == 1)
    def _():
        step(gbuf1, 1, gbuf0, 0)


def _combine_layer(h, y, dest, route, ln_g, ln_b):
    t, d = h.shape
    ts = COMBINE_TILE
    n_tiles = t // ts
    grid_spec = pltpu.PrefetchScalarGridSpec(
        num_scalar_prefetch=1,
        grid=(n_tiles,),
        in_specs=[pl.BlockSpec((ts, d), lambda i, dest: (i, 0)), pl.BlockSpec((ts, LANES), lambda i, dest: (i, 0)),
                  pl.BlockSpec((1, d), lambda i, dest: (0, 0)), pl.BlockSpec((1, d), lambda i, dest: (0, 0)),
                  pl.BlockSpec(memory_space=pl.ANY)],
        out_specs=pl.BlockSpec((ts, d), lambda i, dest: (i, 0)),
        scratch_shapes=[pltpu.VMEM((TOP_K, ts * SUBLANES, LANES), F32),
                        pltpu.VMEM((TOP_K, ts * SUBLANES, LANES), F32), pltpu.SemaphoreType.DMA((2,))],
    )
    return pl.pallas_call(
        functools.partial(_combine_kernel, ts=ts, n_tiles=n_tiles),
        out_shape=jax.ShapeDtypeStruct((t, d), F32),
        grid_spec=grid_spec,
        compiler_params=_params(1),
        name="moe_combine_ln",
    )(dest, h, route, ln_g.reshape(1, d), ln_b.reshape(1, d), y)


def _moe_layer(h2, route, cnt, layer, w_gate_up, b_gate_up, w_down, b_down, ln_g, ln_b):
    bsz, seq, d = h2.shape
    t = bsz * seq
    tm = MOE_TILE
    n_exp = w_gate_up.shape[1]
    h2 = h2.reshape(t, d)
    route = route.reshape(t, LANES)
    idx = route[:, 0:TOP_K].astype(jnp.int32)
    rank = route[:, 2 * TOP_K:3 * TOP_K].astype(jnp.int32)
    counts = cnt[0, :n_exp].astype(jnp.int32)
    padded = (counts + tm - 1) // tm * tm
    pend = jnp.cumsum(padded)
    pstart = pend - padded
    dest = (pstart[idx] + rank).reshape(-1)
    n_blocks = (t * TOP_K) // tm + n_exp
    n_used = (pend[-1] // tm).astype(jnp.int32)
    blk = jnp.arange(n_blocks, dtype=jnp.int32)
    block_e = jnp.sum((pend[None, :] <= (blk * tm)[:, None]).astype(jnp.int32), axis=1)
    block_e = jnp.minimum(block_e, n_exp - 1)
    block_e = jnp.where(blk < n_used, block_e, block_e[jnp.maximum(n_used - 1, 0)])
    last_blk = jnp.where(padded > 0, pend // tm - 1, -1)
    tail_blk = n_used + jnp.arange(n_exp, dtype=jnp.int32)
    zero_blocks = jnp.concatenate([last_blk, jnp.where(tail_blk < n_blocks, tail_blk, -1)]).astype(jnp.int32)
    xs = _dispatch_rows(h2, dest, zero_blocks, n_blocks * tm)
    y = _moe_experts(xs, block_e, n_used.reshape(1), layer, w_gate_up, b_gate_up[:, 0::2], b_gate_up[:, 1::2],
                     w_down, b_down)
    return _combine_layer(h2, y, dest, route, ln_g, ln_b).reshape(bsz, seq, d)


def kernel(x, mem, pool_w_in, pool_w_grp, pool_scale, pool_w_out, ssm_w_in, ssm_conv_w, ssm_conv_b, ssm_dt_bias, ssm_a_log, ssm_d, ssm_norm_g, ssm_w_out, xa_wq, xa_wk, xa_wv, xa_wo, moe_w_router, moe_b_router, moe_w_gate_up, moe_b_gate_up, moe_w_down, moe_b_down, ln_mix_g, ln_mix_b, ln_xa_g, ln_xa_b, ln_ffn_g, ln_ffn_b):
    h = x
    for i in range(DEPTH):
        j = i // 2
        if i % 2 == 0:
            h = _pool_layer(h, pool_w_in[j], pool_w_grp[j], pool_scale[j], pool_w_out[j], ln_mix_g[i], ln_mix_b[i])
        else:
            h = _ssd_layer(h, ssm_w_in[j], ssm_conv_w[j], ssm_conv_b[j], ssm_dt_bias[j], ssm_a_log[j], ssm_d[j],
                           ssm_norm_g[j], ssm_w_out[j], ln_mix_g[i], ln_mix_b[i])
        h2, route, cnt = _xattn_layer(h, mem, xa_wq[i], xa_wk[i], xa_wv[i], xa_wo[i], ln_xa_g[i], ln_xa_b[i],
                                      moe_w_router[i], moe_b_router[i])
        h = _moe_layer(h2, route, cnt, i, moe_w_gate_up, moe_b_gate_up[i], moe_w_down, moe_b_down[i],
                       ln_ffn_g[i], ln_ffn_b[i])
    return h
```

```python
import functools

import jax
import jax.numpy as jnp
from jax import lax
from jax.experimental import pallas as pl
from jax.experimental.pallas import tpu as pltpu

F32 = jnp.float32
BF16 = jnp.bfloat16

DEPTH = 2
DEEPNORM_ALPHA = (2 * DEPTH) ** 0.25
LN_EPS = 1e-5
POOL_WINDOWS = (2, 4, 8, 16)
POOL_HALO = 16
SSM_HEAD_DIM = 64
SSM_N_GROUPS = 8
SSM_HEADS_PER_GROUP = 4
SSM_D_STATE = 128
SSM_CONV = 4
CONV_HALO = 8
SSM_NORM_EPS = 1e-5
XA_HEADS = 4
N_EXPERTS = 32
TOP_K = 4
SWIGLU_LIMIT = 7.0
SWIGLU_ALPHA = 1.702

LANES = 128
SUBLANES = 8
VMEM_LIMIT_BYTES = 56 * 1024 * 1024

POOL_TILE = 512
XATTN_TILE = 512
SSD_TILE = 256
SSD_CHUNK = 128
MOE_TILE = 512
MOE_ROW_GROUP = 128
DISPATCH_TILE = 256
COMBINE_TILE = 256
ROWS_PER_ISSUE = 8

NEG_BIG = -1e30


def _layer_norm(v, g, b):
    mu = jnp.mean(v, -1, keepdims=True)
    d = v - mu
    var = jnp.mean(d * d, -1, keepdims=True)
    return d * lax.rsqrt(var + LN_EPS) * g + b


def _dot(a, b):
    return jnp.dot(a, b, preferred_element_type=F32)


def _dot_nt(a, b):
    return lax.dot_general(a, b, (((1,), (1,)), ((), ())), preferred_element_type=F32)


def _dot_tn(a, b):
    return lax.dot_general(a, b, (((0,), (0,)), ((), ())), preferred_element_type=F32)


def _const_spec(shape):
    nd = len(shape)
    return pl.BlockSpec(shape, lambda *_: (0,) * nd, pipeline_mode=pl.Buffered(1))


def _params(n_axes):
    return pltpu.CompilerParams(dimension_semantics=("arbitrary",) * n_axes,
                                vmem_limit_bytes=VMEM_LIMIT_BYTES)


def _pool_kernel(x_ref, win_ref, wgrp_ref, scale_ref, wout_ref, g_ref, b_ref, o_ref, ext_ref, *, ts, gd):
    s = pl.program_id(1)

    @pl.when(s == 0)
    def _():
        ext_ref[0:POOL_HALO, :] = jnp.zeros((POOL_HALO, ext_ref.shape[1]), F32)

    x = x_ref[...]
    ext_ref[POOL_HALO:, :] = _dot(x.astype(BF16), win_ref[...])
    pos = lax.broadcasted_iota(jnp.int32, (ts, 1), 0) + s * ts
    mix = None
    for g, w in enumerate(POOL_WINDOWS):
        cols = slice(g * gd, (g + 1) * gd)
        e = ext_ref[:, cols]
        acc = e
        sh = 1
        while sh < w:
            acc = acc + pltpu.roll(acc, sh, axis=0)
            sh *= 2
        cnt = jnp.minimum(pos + 1, w).astype(F32)
        m = acc[POOL_HALO:, :] / cnt - e[POOL_HALO:, :]
        yg = _dot(m.astype(BF16), wgrp_ref[g]) * scale_ref[:, cols]
        part = _dot(yg.astype(BF16), wout_ref[cols, :])
        mix = part if mix is None else mix + part
    ext_ref[0:POOL_HALO, :] = ext_ref[ts:ts + POOL_HALO, :]
    o_ref[...] = _layer_norm(DEEPNORM_ALPHA * x + mix, g_ref[...], b_ref[...])


def _pool_layer(h, w_in, w_grp, scale, w_out, ln_g, ln_b):
    bsz, seq, d = h.shape
    ts = POOL_TILE
    gd = d // len(POOL_WINDOWS)
    tile = pl.BlockSpec((None, ts, d), lambda b, s: (b, s, 0))
    return pl.pallas_call(
        functools.partial(_pool_kernel, ts=ts, gd=gd),
        out_shape=jax.ShapeDtypeStruct((bsz, seq, d), F32),
        grid=(bsz, seq // ts),
        in_specs=[tile, _const_spec((d, d)), _const_spec((len(POOL_WINDOWS), gd, gd)), _const_spec((1, d)),
                  _const_spec((d, d)), _const_spec((1, d)), _const_spec((1, d))],
        out_specs=tile,
        scratch_shapes=[pltpu.VMEM((POOL_HALO + ts, d), F32)],
        compiler_params=_params(2),
        name="pool_mixer_ln",
    )(h, w_in.astype(BF16), w_grp.astype(BF16), scale.reshape(1, d), w_out.astype(BF16),
      ln_g.reshape(1, d), ln_b.reshape(1, d))


def _sigmoid(v):
    return 1.0 / (1.0 + jnp.exp(-v))


def _ssd_kernel(x_ref, wz_ref, wx_ref, wb_ref, wc_ref, wdt_ref, convw_ref, convb_ref, dtb_ref, alog_ref,
                dexp_ref, normg_ref, expand_ref, wout_ref, g_ref, b_ref, o_ref,
                xbc_scr, z_scr, actx_scr, bmat_scr, cmat_scr, y_scr, state_scr, *, ts, q, d_inner, gn):
    s = pl.program_id(1)
    conv_dim = d_inner + 2 * gn
    gw = d_inner // SSM_N_GROUPS

    @pl.when(s == 0)
    def _():
        xbc_scr[:, 0:CONV_HALO, :] = jnp.zeros((conv_dim // LANES, CONV_HALO, LANES), F32)
        state_scr[...] = jnp.zeros(state_scr.shape, F32)

    x = x_ref[...]
    xb = x.astype(BF16)
    z_scr[...] = _dot(xb, wz_ref[...])
    strip = 512
    for w_ref, base in ((wx_ref, 0), (wb_ref, d_inner), (wc_ref, d_inner + gn)):
        for c in range(0, w_ref.shape[1], strip):
            res = _dot(xb, w_ref[:, c:c + strip])
            for j in range(strip // LANES):
                xbc_scr[(base + c) // LANES + j, CONV_HALO:, :] = res[:, j * LANES:(j + 1) * LANES]
    dt_raw = _dot(xb, wdt_ref[...]) + dtb_ref[...]
    dtv = jnp.maximum(dt_raw, 0.0) + jnp.log1p(jnp.exp(-jnp.abs(dt_raw)))
    a_all = dtv * (-jnp.exp(alog_ref[...]))

    for j in range(conv_dim // LANES):
        cs = slice(j * LANES, (j + 1) * LANES)
        acc = convb_ref[:, cs]
        for k in range(SSM_CONV):
            r0 = CONV_HALO - (SSM_CONV - 1) + k
            acc = acc + convw_ref[k:k + 1, cs] * xbc_scr[j, r0:r0 + ts, :]
        act = acc * _sigmoid(acc)
        c = j * LANES
        if c < d_inner:
            actx_scr[:, cs] = act
        elif c < d_inner + gn:
            bmat_scr[:, c - d_inner:c - d_inner + LANES] = act.astype(BF16)
        else:
            cmat_scr[:, c - d_inner - gn:c - d_inner - gn + LANES] = act.astype(BF16)
    xbc_scr[:, 0:CONV_HALO, :] = xbc_scr[:, ts:ts + CONV_HALO, :]

    expand = expand_ref[...]
    row_i = lax.broadcasted_iota(jnp.int32, (q, LANES), 0)
    causal = lax.broadcasted_iota(jnp.int32, (q, q), 0) >= lax.broadcasted_iota(jnp.int32, (q, q), 1)
    head_of_lane = lax.broadcasted_iota(jnp.int32, (q, gw), 1) // SSM_HEAD_DIM
    head_mask = [jnp.where(head_of_lane == r, 1.0, 0.0).astype(BF16) for r in range(SSM_HEADS_PER_GROUP)]

    for c in range(ts // q):
        rows = slice(c * q, (c + 1) * q)
        acs = a_all[rows, :]
        sh = 1
        while sh < q:
            acs = acs + jnp.where(row_i >= sh, pltpu.roll(acs, sh, axis=0), 0.0)
            sh *= 2
        acs_t = acs.T
        a_last = acs[q - 1:q, :]
        e_in = _dot(jnp.exp(acs).astype(BF16), expand)
        dec = _dot(jnp.exp(a_last - acs).astype(BF16), expand)
        dtx = _dot(dtv[rows, :].astype(BF16), expand)
        cd = jnp.broadcast_to(jnp.exp(a_last), (8, LANES))
        cd_hi = cd.astype(BF16)
        cd_lo = (cd - cd_hi.astype(F32)).astype(BF16)
        cdx = (_dot(cd_hi, expand) + _dot(cd_lo, expand))[0:1, :]
        xd = actx_scr[rows, :] * dtx
        xdd = xd * dec
        xdb = xd.astype(BF16)
        for g in range(SSM_N_GROUPS):
            gc = slice(g * gw, (g + 1) * gw)
            nc = slice(g * SSM_D_STATE, (g + 1) * SSM_D_STATE)
            bg = bmat_scr[rows, nc]
            cg = cmat_scr[rows, nc]
            cb = _dot_nt(cg, bg)
            xg = xdb[:, gc]
            mhs, xms = [], []
            for r in range(SSM_HEADS_PER_GROUP):
                hd = g * SSM_HEADS_PER_GROUP + r
                seg = acs[:, hd:hd + 1] - acs_t[hd:hd + 1, :]
                lmat = jnp.exp(jnp.where(causal, seg, NEG_BIG))
                mhs.append((cb * lmat).astype(BF16))
                xms.append(xg * head_mask[r])
            yg = (_dot(cg, state_scr[g].astype(BF16)) * e_in[:, gc]
                  + _dot(jnp.concatenate(mhs, axis=1), jnp.concatenate(xms, axis=0)))
            y_scr[rows, gc] = yg
            state_scr[g] = state_scr[g] * cdx[:, gc] + _dot_tn(bg, xdd[:, gc].astype(BF16))

    mix = None
    for g in range(SSM_N_GROUPS):
        gc = slice(g * gw, (g + 1) * gw)
        zz = z_scr[:, gc]
        yv = (y_scr[:, gc] + dexp_ref[:, gc] * actx_scr[:, gc]) * (zz * _sigmoid(zz))
        yv = yv * lax.rsqrt(jnp.mean(yv * yv, -1, keepdims=True) + SSM_NORM_EPS) * normg_ref[:, gc]
        part = _dot(yv.astype(BF16), wout_ref[gc, :])
        mix = part if mix is None else mix + part
    o_ref[...] = _layer_norm(DEEPNORM_ALPHA * x + mix, g_ref[...], b_ref[...])


def _ssd_layer(h, w_in, conv_w, conv_b, dt_bias, a_log, d_skip, norm_g, w_out, ln_g, ln_b):
    bsz, seq, d = h.shape
    n_heads = a_log.shape[0]
    d_inner = n_heads * SSM_HEAD_DIM
    gn = SSM_N_GROUPS * SSM_D_STATE
    conv_dim = d_inner + 2 * gn
    ts, q = SSD_TILE, SSD_CHUNK
    w_in = w_in.astype(BF16)
    wz = w_in[:, :d_inner]
    wx = w_in[:, d_inner:2 * d_inner]
    wb = w_in[:, 2 * d_inner:2 * d_inner + gn]
    wc = w_in[:, 2 * d_inner + gn:2 * d_inner + 2 * gn]
    pad = LANES - n_heads
    wdt = jnp.pad(w_in[:, d_inner + conv_dim:], ((0, 0), (0, pad)))
    dtb = jnp.pad(dt_bias.astype(F32), (0, pad)).reshape(1, LANES)
    alog = jnp.pad(a_log.astype(F32), (0, pad)).reshape(1, LANES)
    dexp = jnp.repeat(d_skip.astype(F32), SSM_HEAD_DIM).reshape(1, d_inner)
    expand = (jnp.arange(LANES)[:, None] == (jnp.arange(d_inner)[None, :] // SSM_HEAD_DIM)).astype(BF16)
    tile = pl.BlockSpec((None, ts, d), lambda b, s: (b, s, 0))
    return pl.pallas_call(
        functools.partial(_ssd_kernel, ts=ts, q=q, d_inner=d_inner, gn=gn),
        out_shape=jax.ShapeDtypeStruct((bsz, seq, d), F32),
        grid=(bsz, seq // ts),
        in_specs=[tile, _const_spec((d, d_inner)), _const_spec((d, d_inner)), _const_spec((d, gn)),
                  _const_spec((d, gn)), _const_spec((d, LANES)), _const_spec((SSM_CONV, conv_dim)),
                  _const_spec((1, conv_dim)), _const_spec((1, LANES)), _const_spec((1, LANES)),
                  _const_spec((1, d_inner)), _const_spec((1, d_inner)), _const_spec((LANES, d_inner)),
                  _const_spec((d_inner, d)), _const_spec((1, d)), _const_spec((1, d))],
        out_specs=tile,
        scratch_shapes=[pltpu.VMEM((conv_dim // LANES, CONV_HALO + ts, LANES), F32), pltpu.VMEM((ts, d_inner), F32),
                        pltpu.VMEM((ts, d_inner), F32), pltpu.VMEM((ts, gn), BF16), pltpu.VMEM((ts, gn), BF16),
                        pltpu.VMEM((ts, d_inner), F32),
                        pltpu.VMEM((SSM_N_GROUPS, SSM_D_STATE, d_inner // SSM_N_GROUPS), F32)],
        compiler_params=_params(2),
        name="ssd_mixer_ln",
    )(h, wz, wx, wb, wc, wdt, conv_w.astype(F32), conv_b.reshape(1, conv_dim), dtb, alog, dexp,
      norm_g.reshape(1, d_inner), expand, w_out.astype(BF16), ln_g.reshape(1, d), ln_b.reshape(1, d))


def _xattn_kernel(h_ref, mem_ref, wq_ref, wk_ref, wv_ref, wo_ref, g_ref, b_ref, wrh_ref, wrl_ref, br_ref,
                  o_ref, route_ref, cnt_ref, k_scr, v_scr, carry_scr, *, ts, hd):
    b = pl.program_id(0)
    s = pl.program_id(1)

    @pl.when(s == 0)
    def _():
        mb = mem_ref[...].astype(BF16)
        k_scr[...] = _dot(mb, wk_ref[...]).astype(BF16)
        v_scr[...] = _dot(mb, wv_ref[...]).astype(BF16)

    @pl.when((b == 0) & (s == 0))
    def _():
        carry_scr[...] = jnp.zeros(carry_scr.shape, F32)

    h = h_ref[...]
    qv = (_dot(h.astype(BF16), wq_ref[...]) * (hd ** -0.5)).astype(BF16)
    heads = []
    for hh in range(XA_HEADS):
        cols = slice(hh * hd, (hh + 1) * hd)
        sc = _dot_nt(qv[:, cols], k_scr[:, cols])
        p = jnp.exp(sc - jnp.max(sc, -1, keepdims=True))
        o = _dot(p.astype(BF16), v_scr[:, cols]) / jnp.sum(p, -1, keepdims=True)
        heads.append(o.astype(BF16))
    xa = _dot(jnp.concatenate(heads, axis=1), wo_ref[...])
    h2 =_layer_norm(DEEPNORM_ALPHA * h + xa, g_ref[...], b_ref[...])
    o_ref[...] = h2
    h2_hi = h2.astype(BF16)

    h2_lo = (h2 - h2_hi.astype(F32)).astype(BF16)
    logits = _dot(h2_hi, wrh_ref[...]) + _dot(h2_lo, wrh_ref[...]) + _dot(h2_hi, wrl_ref[...]) + br_ref[...]
    lane = lax.broadcasted_iota(jnp.int32, (ts, LANES), 1).astype(F32)
    work = logits
    vals, idxs, sels = [], [], []
    for _ in range(TOP_K):
        m = jnp.max(work, -1, keepdims=True)
        ik = jnp.min(jnp.where(work == m, lane, float(LANES)), -1, keepdims=True)
        sel = lane == ik
        vals.append(m)
        idxs.append(ik)
        sels.append(sel)
        work = jnp.where(sel, -jnp.inf, work)
    exps = [jnp.exp(v - vals[0]) for v in vals]
    den = exps[0]
    for e in exps[1:]:
        den = den + e
    onehot = jnp.zeros((ts, LANES), F32)
    for sel in sels:
        onehot = onehot + sel.astype(F32)
    below = (lax.broadcasted_iota(jnp.int32, (ts, ts), 0) > lax.broadcasted_iota(jnp.int32, (ts, ts), 1))
    before = _dot(jnp.where(below, 1.0, 0.0).astype(BF16), onehot.astype(BF16)) + carry_scr[0:1, :]
    route = jnp.zeros((ts, LANES), F32)
    for k in range(TOP_K):
        rank = jnp.sum(jnp.where(sels[k], before, 0.0), -1, keepdims=True)
        route = jnp.where(lane == float(k), idxs[k], route)
        route = jnp.where(lane == float(TOP_K + k), exps[k] / den, route)
        route = jnp.where(lane == float(2 * TOP_K + k), rank, route)
    route_ref[...] = route
    carry_scr[...] = carry_scr[...] + jnp.sum(onehot, 0, keepdims=True)
    cnt_ref[...] = carry_scr[...]


def _xattn_layer(h, mem, wq, wk, wv, wo, ln_g, ln_b, w_router, b_router):
    bsz, seq, d = h.shape
    mlen = mem.shape[1]
    ts = XATTN_TILE
    hd = d // XA_HEADS
    n_exp = w_router.shape[1]
    wr = jnp.pad(w_router.astype(F32), ((0, 0), (0, LANES - n_exp)))
    wr_hi = wr.astype(BF16)
    wr_lo = (wr - wr_hi.astype(F32)).astype(BF16)
    br = jnp.pad(b_router.astype(F32), (0, LANES - n_exp), constant_values=-jnp.inf).reshape(1, LANES)
    tile = pl.BlockSpec((None, ts, d), lambda b, s: (b, s, 0))
    rtile = pl.BlockSpec((None, ts, LANES), lambda b, s: (b, s, 0))
    return pl.pallas_call(
        functools.partial(_xattn_kernel, ts=ts, hd=hd),
        out_shape=(jax.ShapeDtypeStruct((bsz, seq, d), F32),
                   jax.ShapeDtypeStruct((bsz, seq, LANES), F32), jax.ShapeDtypeStruct((8, LANES), F32)),
        grid=(bsz, seq // ts),
        in_specs=[tile, pl.BlockSpec((None, mlen, d), lambda b, s: (b, 0, 0)),
                  _const_spec((d, d)), _const_spec((d, d)), _const_spec((d, d)), _const_spec((d, d)),
                  _const_spec((1, d)), _const_spec((1, d)), _const_spec((d, LANES)), _const_spec((d, LANES)),
                  _const_spec((1, LANES))],
        out_specs=(tile, rtile, pl.BlockSpec((8, LANES), lambda b, s: (0, 0))),
        scratch_shapes=[pltpu.VMEM((mlen, d), BF16), pltpu.VMEM((mlen, d), BF16), pltpu.VMEM((8, LANES), F32)],
        compiler_params=_params(2),
        name="xattn_ln_router",
    )(h, mem, wq.astype(BF16), wk.astype(BF16), wv.astype(BF16), wo.astype(BF16),
      ln_g.reshape(1, d), ln_b.reshape(1, d), wr_hi, wr_lo, br)


def _rows_from_tiles(ref, n):
    return jnp.concatenate([ref[pl.ds(c, n, stride=SUBLANES), :] for c in range(SUBLANES)], axis=1)


def _rows_to_tiles(ref, v, n):
    for c in range(SUBLANES):
        ref[pl.ds(c, n, stride=SUBLANES), :] = v[:, c * LANES:(c + 1) * LANES]


def _moe_kernel(be_ref, nu_ref, nv_ref, x_ref, wgu_ref, bg_ref, bu_ref, wd_ref, bd_ref, o_ref, wg_scr, wu_scr, wd_scr,
                *, tm):
    i = pl.program_id(0)
    active = i < nu_ref[0]
    new_expert = (i == 0) | (be_ref[i] != be_ref[jnp.maximum(i - 1, 0)])

    @pl.when(active & new_expert)
    def _():
        w2 = 2 * LANES
        src = lax.broadcasted_iota(jnp.int32, (w2, w2), 0)
        dst = lax.broadcasted_iota(jnp.int32, (w2, w2), 1)
        perm = jnp.where(src == jnp.where(dst < LANES, 2 * dst, 2 * (dst - LANES) + 1), 1.0, 0.0).astype(BF16)
        for c in range(wgu_ref.shape[1] // w2):
            res = _dot(wgu_ref[:, c * w2:(c + 1) * w2].astype(BF16), perm)
            wg_scr[:, c * LANES:(c + 1) * LANES] = res[:, :LANES].astype(BF16)
            wu_scr[:, c * LANES:(c + 1) * LANES] = res[:, LANES:].astype(BF16)
        wd_scr[...] = wd_ref[...].astype(BF16)

    nv = nv_ref[i]
    for r in range(MOE_ROW_GROUP, tm + 1, MOE_ROW_GROUP):
        @pl.when(active & (nv > r - MOE_ROW_GROUP) & (nv <= r))
        def _():
            x = _rows_from_tiles(x_ref, r).astype(BF16)
            gate = jnp.minimum(_dot(x, wg_scr[...]) + bg_ref[...], SWIGLU_LIMIT)
            up = jnp.clip(_dot(x, wu_scr[...]) + bu_ref[...], -SWIGLU_LIMIT, SWIGLU_LIMIT)
            act = (up + 1.0) * (gate * _sigmoid(SWIGLU_ALPHA * gate))
            _rows_to_tiles(o_ref, _dot(act.astype(BF16), wd_scr[...]) + bd_ref[...], r)
            if r < tm:
                o_ref[r * SUBLANES:, :] = jnp.zeros(((tm - r) * SUBLANES, LANES), F32)

    @pl.when(jnp.logical_not(active))
    def _():
        o_ref[...] = jnp.zeros(o_ref.shape, F32)


def _moe_experts(xs, block_e, n_used, n_valid, layer, w_gate_up, b_gate, b_up, w_down, b_down):
    _, n_exp, f, d = w_down.shape
    n_rows = xs.shape[0] // SUBLANES
    tm = MOE_TILE
    n_blocks = n_rows // tm

    def row_map(i, be, nu, nv):
        return (jnp.minimum(i, nu[0] - 1), 0)

    def w_map(i, be, nu, nv):
        return (be[i], 0, 0)

    def lw_map(i, be, nu, nv):
        return (layer, be[i], 0, 0)

    grid_spec = pltpu.PrefetchScalarGridSpec(
        num_scalar_prefetch=3,
        grid=(n_blocks,),
        in_specs=[pl.BlockSpec((tm * SUBLANES, LANES), row_map),
                  pl.BlockSpec((None, None, d, 2 * f), lw_map),
                  pl.BlockSpec((None, 1, f), w_map), pl.BlockSpec((None, 1, f), w_map),
                  pl.BlockSpec((None, None, f, d), lw_map), pl.BlockSpec((None, 1, d), w_map)],
        out_specs=pl.BlockSpec((tm * SUBLANES, LANES), lambda i, be, nu, nv: (i, 0)),
        scratch_shapes=[pltpu.VMEM((d, f), BF16), pltpu.VMEM((d, f), BF16), pltpu.VMEM((f, d), BF16)],
    )
    return pl.pallas_call(
        functools.partial(_moe_kernel, tm=tm),
        out_shape=jax.ShapeDtypeStruct((n_rows * SUBLANES, LANES), F32),
        grid_spec=grid_spec,
        compiler_params=_params(1),
        name="moe_experts",
    )(block_e, n_used, n_valid, xs, w_gate_up, b_gate.reshape(n_exp, 1, f), b_up.reshape(n_exp, 1, f),
      w_down, b_down.reshape(n_exp, 1, d))


def _row_copy(src, src_row, dst, dst_row, sem):
    return pltpu.make_async_copy(src.at[pl.ds(pl.multiple_of(src_row * SUBLANES, SUBLANES), SUBLANES)],
                                 dst.at[pl.ds(pl.multiple_of(dst_row * SUBLANES, SUBLANES), SUBLANES)], sem)


def _dispatch_kernel(dest_ref, zblk_ref, h_ref, xs_ref, stage0, stage1, zero_scr, sem, zsem, *, ts, n_tiles, tm):
    i = pl.program_id(0)

    @pl.when(i == 0)
    def _():
        zero_scr[...] = jnp.zeros(zero_scr.shape, F32)
        for j in range(zblk_ref.shape[0]):
            @pl.when(zblk_ref[j] >= 0)
            def _():
                start = pl.multiple_of(zblk_ref[j] * (tm * SUBLANES), tm * SUBLANES)
                pltpu.make_async_copy(zero_scr, xs_ref.at[pl.ds(start, tm * SUBLANES)], zsem).start()
        for j in range(zblk_ref.shape[0]):
            @pl.when(zblk_ref[j] >= 0)
            def _():
                pltpu.make_async_copy(zero_scr, xs_ref.at[pl.ds(0, tm * SUBLANES)], zsem).wait()

    def drain(stage, s):
        for _ in range(TOP_K):
            pltpu.make_async_copy(stage, xs_ref.at[pl.ds(0, ts * SUBLANES)], sem.at[s]).wait()

    def step(stage, s, other, so):
        @pl.when(i >= 2)
        def _():
            drain(stage, s)

        _rows_to_tiles(stage, h_ref[...], ts)

        def issue(j, carry):
            for u in range(ROWS_PER_ISSUE):
                tok = j * ROWS_PER_ISSUE + u
                for k in range(TOP_K):
                    row = dest_ref[(i * ts + tok) * TOP_K + k]
                    _row_copy(stage, tok, xs_ref, row, sem.at[s]).start(priority=k % 2)
            return carry

        lax.fori_loop(0, ts // ROWS_PER_ISSUE, issue, 0)

        @pl.when(i == n_tiles - 1)
        def _():
            drain(other, so)
            drain(stage, s)

    @pl.when(i % 2 == 0)
    def _():
        step(stage0, 0, stage1, 1)

    @pl.when(i % 2 == 1)
    def _():
        step(stage1, 1, stage0, 0)


def _dispatch_rows(h, dest, zero_blocks, n_rows):
    t, d = h.shape
    ts = DISPATCH_TILE
    tm = MOE_TILE
    n_tiles = t // ts
    assert d == SUBLANES * LANES and n_tiles >= 2
    grid_spec = pltpu.PrefetchScalarGridSpec(
        num_scalar_prefetch=2,
        grid=(n_tiles,),
        in_specs=[pl.BlockSpec((ts, d), lambda i, dest, zb: (i, 0))],
        out_specs=pl.BlockSpec(memory_space=pl.ANY),
        scratch_shapes=[pltpu.VMEM((ts * SUBLANES, LANES), F32), pltpu.VMEM((ts * SUBLANES, LANES), F32),
                        pltpu.VMEM((tm * SUBLANES, LANES), F32),
                        pltpu.SemaphoreType.DMA((2,)), pltpu.SemaphoreType.DMA(())],
    )
    return pl.pallas_call(
        functools.partial(_dispatch_kernel, ts=ts, n_tiles=n_tiles, tm=tm),
        out_shape=jax.ShapeDtypeStruct((n_rows * SUBLANES, LANES), F32),
        grid_spec=grid_spec,
        compiler_params=_params(1),
        name="moe_dispatch",
    )(dest, zero_blocks, h)


def _combine_kernel(dest_ref, h_ref, route_ref, g_ref, b_ref, y_ref, o_ref, gbuf0, gbuf1, sem, *, ts, n_tiles):
    i = pl.program_id(0)

    def issue(tile, gbuf, s):
        def body(j, carry):
            for u in range(ROWS_PER_ISSUE):
                tok = j * ROWS_PER_ISSUE + u
                for k in range(TOP_K):
                    row = dest_ref[(tile * ts + tok) * TOP_K + k]
                    _row_copy(y_ref, row, gbuf.at[k], tok, sem.at[s]).start(priority=k % 2)
            return carry

        lax.fori_loop(0, ts // ROWS_PER_ISSUE, body, 0)

    def drain(gbuf, s):
        for k in range(TOP_K):
            pltpu.make_async_copy(y_ref.at[pl.ds(0, ts * SUBLANES)], gbuf.at[k], sem.at[s]).wait()

    def step(gbuf, s, nxt, sn):
        @pl.when(i + 1 < n_tiles)
        def _():
            issue(i + 1, nxt, sn)

        drain(gbuf, s)
        parts = []
        for c in range(SUBLANES):
            acc = None
            for k in range(TOP_K):
                piece = gbuf[k, pl.ds(c, ts, stride=SUBLANES), :] * route_ref[:, TOP_K + k:TOP_K + k + 1]
                acc = piece if acc is None else acc + piece
            parts.append(acc)
        ff = jnp.concatenate(parts, axis=1)
        o_ref[...] = _layer_norm(DEEPNORM_ALPHA * h_ref[...] + ff, g_ref[...], b_ref[...])

    @pl.when(i == 0)
    def _():
        issue(0, gbuf0, 0)

    @pl.when(i % 2 == 0)
    def _():
        step(gbuf0, 0, gbuf1, 1)

    @pl.when(i % 2 == 1)
    def _():
        step(gbuf1, 1, gbuf0, 0)


def _combine_layer(h, y, dest, route, ln_g, ln_b):
    t, d = h.shape
    ts = COMBINE_TILE
    n_tiles = t // ts
    grid_spec = pltpu.PrefetchScalarGridSpec(
        num_scalar_prefetch=1,
        grid=(n_tiles,),
        in_specs=[pl.BlockSpec((ts, d), lambda i, dest: (i, 0)), pl.BlockSpec((ts, LANES), lambda i, dest: (i, 0)),
                  pl.BlockSpec((1, d), lambda i, dest: (0, 0)), pl.BlockSpec((1, d), lambda i, dest: (0, 0)),
                  pl.BlockSpec(memory_space=pl.ANY)],
        out_specs=pl.BlockSpec((ts, d), lambda i, dest: (i, 0)),
        scratch_shapes=[pltpu.VMEM((TOP_K, ts * SUBLANES, LANES), F32),
                        pltpu.VMEM((TOP_K, ts * SUBLANES, LANES), F32), pltpu.SemaphoreType.DMA((2,))],
    )
    return pl.pallas_call(
        functools.partial(_combine_kernel, ts=ts, n_tiles=n_tiles),
        out_shape=jax.ShapeDtypeStruct((t, d), F32),
        grid_spec=grid_spec,
        compiler_params=_params(1),
        name="moe_combine_ln",
    )(dest, h, route, ln_g.reshape(1, d), ln_b.reshape(1, d), y)


def _moe_layer(h2, route, cnt, layer, w_gate_up, b_gate_up, w_down, b_down, ln_g, ln_b):
    bsz, seq, d = h2.shape
    t = bsz * seq
    tm = MOE_TILE
    n_exp = w_gate_up.shape[1]
    h2 = h2.reshape(t, d)
    route = route.reshape(t, LANES)
    idx = route[:, 0:TOP_K].astype(jnp.int32)
    rank = route[:, 2 * TOP_K:3 * TOP_K].astype(jnp.int32)
    counts = cnt[0, :n_exp].astype(jnp.int32)
    padded = (counts + tm - 1) // tm * tm
    pend = jnp.cumsum(padded)
    pstart = pend - padded
    dest = (pstart[idx] + rank).reshape(-1)
    n_blocks = (t * TOP_K) // tm + n_exp
    n_used = (pend[-1] // tm).astype(jnp.int32)
    blk = jnp.arange(n_blocks, dtype=jnp.int32)
    block_e = jnp.sum((pend[None, :] <= (blk * tm)[:, None]).astype(jnp.int32), axis=1)
    block_e = jnp.minimum(block_e, n_exp - 1)
    block_e = jnp.where(blk < n_used, block_e, block_e[jnp.maximum(n_used - 1, 0)])
    last_blk = jnp.where(padded > 0, pend // tm - 1, -1)
    tail_blk = n_used + jnp.arange(n_exp, dtype=jnp.int32)
    zero_blocks = jnp.concatenate([last_blk, jnp.where(tail_blk < n_blocks, tail_blk, -1)]).astype(jnp.int32)
    xs = _dispatch_rows(h2, dest, zero_blocks, n_blocks * tm)
    n_valid = jnp.clip((pstart + counts)[block_e] - blk * tm, 0, tm).astype(jnp.int32)
    y = _moe_experts(xs, block_e, n_used.reshape(1), n_valid, layer, w_gate_up, b_gate_up[:, 0::2], b_gate_up[:, 1::2],
                     w_down, b_down)
    return _combine_layer(h2, y, dest, route, ln_g, ln_b).reshape(bsz, seq, d)


def kernel(x, mem, pool_w_in, pool_w_grp, pool_scale, pool_w_out, ssm_w_in, ssm_conv_w, ssm_conv_b, ssm_dt_bias, ssm_a_log, ssm_d, ssm_norm_g, ssm_w_out, xa_wq, xa_wk, xa_wv, xa_wo, moe_w_router, moe_b_router, moe_w_gate_up, moe_b_gate_up, moe_w_down, moe_b_down, ln_mix_g, ln_mix_b, ln_xa_g, ln_xa_b, ln_ffn_g, ln_ffn_b):
    h = x
    for i in range(DEPTH):
        j = i // 2
        if i % 2 == 0:
            h = _pool_layer(h, pool_w_in[j], pool_w_grp[j], pool_scale[j], pool_w_out[j], ln_mix_g[i], ln_mix_b[i])
        else:
            h = _ssd_layer(h, ssm_w_in[j], ssm_conv_w[j], ssm_conv_b[j], ssm_dt_bias[j], ssm_a_log[j], ssm_d[j],
                           ssm_norm_g[j], ssm_w_out[j], ln_mix_g[i], ln_mix_b[i])
        h2, route, cnt = _xattn_layer(h, mem, xa_wq[i], xa_wk[i], xa_wv[i], xa_wo[i], ln_xa_g[i], ln_xa_b[i],
                                      moe_w_router[i], moe_b_router[i])
        h = _moe_layer(h2, route, cnt, i, moe_w_gate_up, moe_b_gate_up[i], moe_w_down, moe_b_down[i],
                       ln_ffn_g[i], ln_ffn_b[i])
    return h
```

```python
import functools

import jax
import jax.numpy as jnp
from jax import lax
from jax.experimental import pallas as pl
from jax.experimental.pallas import tpu as pltpu

F32 = jnp.float32
BF16 = jnp.bfloat16

DEPTH = 2
DEEPNORM_ALPHA = (2 * DEPTH) ** 0.25
LN_EPS = 1e-5
POOL_WINDOWS = (2, 4, 8, 16)
POOL_HALO = 16
SSM_HEAD_DIM = 64
SSM_N_GROUPS = 8
SSM_HEADS_PER_GROUP = 4
SSM_D_STATE = 128
SSM_CONV = 4
CONV_HALO = 8
SSM_NORM_EPS = 1e-5
XA_HEADS = 4
N_EXPERTS = 32
TOP_K = 4
SWIGLU_LIMIT = 7.0
SWIGLU_ALPHA = 1.702

LANES = 128
SUBLANES = 8
VMEM_LIMIT_BYTES = 56 * 1024 * 1024

POOL_TILE = 512
XATTN_TILE = 512
SSD_TILE = 256
SSD_CHUNK = 128
MOE_TILE = 512
MOE_ROW_GROUP = 128
DISPATCH_TILE = 256
COMBINE_TILE = 256
ROWS_PER_ISSUE = 8

NEG_BIG = -1e30


def _layer_norm(v, g, b):
    mu = jnp.mean(v, -1, keepdims=True)
    d = v - mu
    var = jnp.mean(d * d, -1, keepdims=True)
    return d * lax.rsqrt(var + LN_EPS) * g + b


def _dot(a, b):
    return jnp.dot(a, b, preferred_element_type=F32)


def _dot_nt(a, b):
    return lax.dot_general(a, b, (((1,), (1,)), ((), ())), preferred_element_type=F32)


def _dot_tn(a, b):
    return lax.dot_general(a, b, (((0,), (0,)), ((), ())), preferred_element_type=F32)


def _const_spec(shape):
    nd = len(shape)
    return pl.BlockSpec(shape, lambda *_: (0,) * nd, pipeline_mode=pl.Buffered(1))


def _params(n_axes):
    return pltpu.CompilerParams(dimension_semantics=("arbitrary",) * n_axes,
                                vmem_limit_bytes=VMEM_LIMIT_BYTES)


def _pool_kernel(x_ref, win_ref, wgrp_ref, scale_ref, wout_ref, g_ref, b_ref, o_ref, ext_ref, *, ts, gd):
    s = pl.program_id(1)

    @pl.when(s == 0)
    def _():
        ext_ref[0:POOL_HALO, :] = jnp.zeros((POOL_HALO, ext_ref.shape[1]), F32)

    x = x_ref[...]
    ext_ref[POOL_HALO:, :] = _dot(x.astype(BF16), win_ref[...])
    pos = lax.broadcasted_iota(jnp.int32, (ts, 1), 0) + s * ts
    mix = None
    for g, w in enumerate(POOL_WINDOWS):
        cols = slice(g * gd, (g + 1) * gd)
        e = ext_ref[:, cols]
        acc = e
        sh = 1
        while sh < w:
            acc = acc + pltpu.roll(acc, sh, axis=0)
            sh *= 2
        cnt = jnp.minimum(pos + 1, w).astype(F32)
        m = acc[POOL_HALO:, :] / cnt - e[POOL_HALO:, :]
        yg = _dot(m.astype(BF16), wgrp_ref[g]) * scale_ref[:, cols]
        part = _dot(yg.astype(BF16), wout_ref[cols, :])
        mix = part if mix is None else mix + part
    ext_ref[0:POOL_HALO, :] = ext_ref[ts:ts + POOL_HALO, :]
    o_ref[...] = _layer_norm(DEEPNORM_ALPHA * x + mix, g_ref[...], b_ref[...])


def _pool_layer(h, w_in, w_grp, scale, w_out, ln_g, ln_b):
    bsz, seq, d = h.shape
    ts = POOL_TILE
    gd = d // len(POOL_WINDOWS)
    tile = pl.BlockSpec((None, ts, d), lambda b, s: (b, s, 0))
    return pl.pallas_call(
        functools.partial(_pool_kernel, ts=ts, gd=gd),
        out_shape=jax.ShapeDtypeStruct((bsz, seq, d), F32),
        grid=(bsz, seq // ts),
        in_specs=[tile, _const_spec((d, d)), _const_spec((len(POOL_WINDOWS), gd, gd)), _const_spec((1, d)),
                  _const_spec((d, d)), _const_spec((1, d)), _const_spec((1, d))],
        out_specs=tile,
        scratch_shapes=[pltpu.VMEM((POOL_HALO + ts, d), F32)],
        compiler_params=_params(2),
        name="pool_mixer_ln",
    )(h, w_in.astype(BF16), w_grp.astype(BF16), scale.reshape(1, d), w_out.astype(BF16),
      ln_g.reshape(1, d), ln_b.reshape(1, d))


def _sigmoid(v):
    return 1.0 / (1.0 + jnp.exp(-v))


def _ssd_kernel(x_ref, wz_ref, wx_ref, wb_ref, wc_ref, wdt_ref, convw_ref, convb_ref, dtb_ref, alog_ref,
                dexp_ref, normg_ref, expand_ref, wout_ref, g_ref, b_ref, o_ref,
                xbc_scr, z_scr, actx_scr, bmat_scr, cmat_scr, y_scr, state_scr, *, ts, q, d_inner, gn):
    s = pl.program_id(1)
    conv_dim = d_inner + 2 * gn
    gw = d_inner // SSM_N_GROUPS

    @pl.when(s == 0)
    def _():
        xbc_scr[:, 0:CONV_HALO, :] = jnp.zeros((conv_dim // LANES, CONV_HALO, LANES), F32)
        state_scr[...] = jnp.zeros(state_scr.shape, F32)

    x = x_ref[...]
    xb = x.astype(BF16)
    z_scr[...] = _dot(xb, wz_ref[...])
    strip = 512
    for w_ref, base in ((wx_ref, 0), (wb_ref, d_inner), (wc_ref, d_inner + gn)):
        for c in range(0, w_ref.shape[1], strip):
            res = _dot(xb, w_ref[:, c:c + strip])
            for j in range(strip // LANES):
                xbc_scr[(base + c) // LANES + j, CONV_HALO:, :] = res[:, j * LANES:(j + 1) * LANES]
    dt_raw = _dot(xb, wdt_ref[...]) + dtb_ref[...]
    dtv = jnp.maximum(dt_raw, 0.0) + jnp.log1p(jnp.exp(-jnp.abs(dt_raw)))
    a_all = dtv * (-jnp.exp(alog_ref[...]))

    for j in range(conv_dim // LANES):
        cs = slice(j * LANES, (j + 1) * LANES)
        acc = convb_ref[:, cs]
        for k in range(SSM_CONV):
            r0 = CONV_HALO - (SSM_CONV - 1) + k
            acc = acc + convw_ref[k:k + 1, cs] * xbc_scr[j, r0:r0 + ts, :]
        act = acc * _sigmoid(acc)
        c = j * LANES
        if c < d_inner:
            actx_scr[:, cs] = act
        elif c < d_inner + gn:
            bmat_scr[:, c - d_inner:c - d_inner + LANES] = act.astype(BF16)
        else:
            cmat_scr[:, c - d_inner - gn:c - d_inner - gn + LANES] = act.astype(BF16)
    xbc_scr[:, 0:CONV_HALO, :] = xbc_scr[:, ts:ts + CONV_HALO, :]

    expand = expand_ref[...]
    row_i = lax.broadcasted_iota(jnp.int32, (q, LANES), 0)
    causal = lax.broadcasted_iota(jnp.int32, (q, q), 0) >= lax.broadcasted_iota(jnp.int32, (q, q), 1)
    head_of_lane = lax.broadcasted_iota(jnp.int32, (q, gw), 1) // SSM_HEAD_DIM
    head_mask = [jnp.where(head_of_lane == r, 1.0, 0.0).astype(BF16) for r in range(SSM_HEADS_PER_GROUP)]

    for c in range(ts // q):
        rows = slice(c * q, (c + 1) * q)
        acs = a_all[rows, :]
        sh = 1
        while sh < q:
            acs = acs + jnp.where(row_i >= sh, pltpu.roll(acs, sh, axis=0), 0.0)
            sh *= 2
        acs_t = acs.T
        a_last = acs[q - 1:q, :]
        e_in = _dot(jnp.exp(acs).astype(BF16), expand)
        dec = _dot(jnp.exp(a_last - acs).astype(BF16), expand)
        dtx = _dot(dtv[rows, :].astype(BF16), expand)
        cd = jnp.broadcast_to(jnp.exp(a_last), (8, LANES))
        cd_hi = cd.astype(BF16)
        cd_lo = (cd - cd_hi.astype(F32)).astype(BF16)
        cdx = (_dot(cd_hi, expand) + _dot(cd_lo, expand))[0:1, :]
        xd = actx_scr[rows, :] * dtx
        xdd = xd * dec
        xdb = xd.astype(BF16)
        for g in range(SSM_N_GROUPS):
            gc = slice(g * gw, (g + 1) * gw)
            nc = slice(g * SSM_D_STATE, (g + 1) * SSM_D_STATE)
            bg = bmat_scr[rows, nc]
            cg = cmat_scr[rows, nc]
            cb = _dot_nt(cg, bg)
            xg = xdb[:, gc]
            mhs, xms = [], []
            for r in range(SSM_HEADS_PER_GROUP):
                hd = g * SSM_HEADS_PER_GROUP + r
                seg = acs[:, hd:hd + 1] - acs_t[hd:hd + 1, :]
                lmat = jnp.exp(jnp.where(causal, seg, NEG_BIG))
                mhs.append((cb * lmat).astype(BF16))
                xms.append(xg * head_mask[r])
            yg = (_dot(cg, state_scr[g].astype(BF16)) * e_in[:, gc]
                  + _dot(jnp.concatenate(mhs, axis=1), jnp.concatenate(xms, axis=0)))
            y_scr[rows, gc] = yg
            state_scr[g] = state_scr[g] * cdx[:, gc] + _dot_tn(bg, xdd[:, gc].astype(BF16))

    mix = None
    for g in range(SSM_N_GROUPS):
        gc = slice(g * gw, (g + 1) * gw)
        zz = z_scr[:, gc]
        yv = (y_scr[:, gc] + dexp_ref[:, gc] * actx_scr[:, gc]) * (zz * _sigmoid(zz))
        yv = yv * lax.rsqrt(jnp.mean(yv * yv, -1, keepdims=True) + SSM_NORM_EPS) * normg_ref[:, gc]
        part = _dot(yv.astype(BF16), wout_ref[gc, :])
        mix = part if mix is None else mix + part
    o_ref[...] = _layer_norm(DEEPNORM_ALPHA * x + mix, g_ref[...], b_ref[...])


def _ssd_layer(h, w_in, conv_w, conv_b, dt_bias, a_log, d_skip, norm_g, w_out, ln_g, ln_b):
    bsz, seq, d = h.shape
    n_heads = a_log.shape[0]
    d_inner = n_heads * SSM_HEAD_DIM
    gn = SSM_N_GROUPS * SSM_D_STATE
    conv_dim = d_inner + 2 * gn
    ts, q = SSD_TILE, SSD_CHUNK
    w_in = w_in.astype(BF16)
    wz = w_in[:, :d_inner]
    wx = w_in[:, d_inner:2 * d_inner]
    wb = w_in[:, 2 * d_inner:2 * d_inner + gn]
    wc = w_in[:, 2 * d_inner + gn:2 * d_inner + 2 * gn]
    pad = LANES - n_heads
    wdt = jnp.pad(w_in[:, d_inner + conv_dim:], ((0, 0), (0, pad)))
    dtb = jnp.pad(dt_bias.astype(F32), (0, pad)).reshape(1, LANES)
    alog = jnp.pad(a_log.astype(F32), (0, pad)).reshape(1, LANES)
    dexp = jnp.repeat(d_skip.astype(F32), SSM_HEAD_DIM).reshape(1, d_inner)
    expand = (jnp.arange(LANES)[:, None] == (jnp.arange(d_inner)[None, :] // SSM_HEAD_DIM)).astype(BF16)
    tile = pl.BlockSpec((None, ts, d), lambda b, s: (b, s, 0))
    return pl.pallas_call(
        functools.partial(_ssd_kernel, ts=ts, q=q, d_inner=d_inner, gn=gn),
        out_shape=jax.ShapeDtypeStruct((bsz, seq, d), F32),
        grid=(bsz, seq // ts),
        in_specs=[tile, _const_spec((d, d_inner)), _const_spec((d, d_inner)), _const_spec((d, gn)),
                  _const_spec((d, gn)), _const_spec((d, LANES)), _const_spec((SSM_CONV, conv_dim)),
                  _const_spec((1, conv_dim)), _const_spec((1, LANES)), _const_spec((1, LANES)),
                  _const_spec((1, d_inner)), _const_spec((1, d_inner)), _const_spec((LANES, d_inner)),
                  _const_spec((d_inner, d)), _const_spec((1, d)), _const_spec((1, d))],
        out_specs=tile,
        scratch_shapes=[pltpu.VMEM((conv_dim // LANES, CONV_HALO + ts, LANES), F32), pltpu.VMEM((ts, d_inner), F32),
                        pltpu.VMEM((ts, d_inner), F32), pltpu.VMEM((ts, gn), BF16), pltpu.VMEM((ts, gn), BF16),
                        pltpu.VMEM((ts, d_inner), F32),
                        pltpu.VMEM((SSM_N_GROUPS, SSM_D_STATE, d_inner // SSM_N_GROUPS), F32)],
        compiler_params=_params(2),
        name="ssd_mixer_ln",
    )(h, wz, wx, wb, wc, wdt, conv_w.astype(F32), conv_b.reshape(1, conv_dim), dtb, alog, dexp,
      norm_g.reshape(1, d_inner), expand, w_out.astype(BF16), ln_g.reshape(1, d), ln_b.reshape(1, d))


def _xattn_kernel(h_ref, mem_ref, wq_ref, wk_ref, wv_ref, wo_ref, g_ref, b_ref, wrh_ref, wrl_ref, br_ref,
                  o_ref, route_ref, cnt_ref, k_scr, v_scr, carry_scr, *, ts, hd):
    b = pl.program_id(0)
    s = pl.program_id(1)

    @pl.when(s == 0)
    def _():
        mb = mem_ref[...].astype(BF16)
        k_scr[...] = _dot(mb, wk_ref[...]).astype(BF16)
        v_scr[...] = _dot(mb, wv_ref[...]).astype(BF16)

    @pl.when((b == 0) & (s == 0))
    def _():
        carry_scr[...] = jnp.zeros(carry_scr.shape, F32)

    h = h_ref[...]
    qv = (_dot(h.astype(BF16), wq_ref[...]) * (hd ** -0.5)).astype(BF16)
    heads = []
    for hh in range(XA_HEADS):
        cols = slice(hh * hd, (hh + 1) * hd)
        sc = _dot_nt(qv[:, cols], k_scr[:, cols])
        p = jnp.exp(sc - jnp.max(sc, -1, keepdims=True))
        o = _dot(p.astype(BF16), v_scr[:, cols]) / jnp.sum(p, -1, keepdims=True)
        heads.append(o.astype(BF16))
    xa = _dot(jnp.concatenate(heads, axis=1), wo_ref[...])
    h2 =_layer_norm(DEEPNORM_ALPHA * h + xa, g_ref[...], b_ref[...])
    o_ref[...] = h2
    h2_hi = h2.astype(BF16)

    h2_lo = (h2 - h2_hi.astype(F32)).astype(BF16)
    logits = _dot(h2_hi, wrh_ref[...]) + _dot(h2_lo, wrh_ref[...]) + _dot(h2_hi, wrl_ref[...]) + br_ref[...]
    lane = lax.broadcasted_iota(jnp.int32, (ts, LANES), 1).astype(F32)
    work = logits
    vals, idxs, sels = [], [], []
    for _ in range(TOP_K):
        m = jnp.max(work, -1, keepdims=True)
        ik = jnp.min(jnp.where(work == m, lane, float(LANES)), -1, keepdims=True)
        sel = lane == ik
        vals.append(m)
        idxs.append(ik)
        sels.append(sel)
        work = jnp.where(sel, -jnp.inf, work)
    exps = [jnp.exp(v - vals[0]) for v in vals]
    den = exps[0]
    for e in exps[1:]:
        den = den + e
    onehot = jnp.zeros((ts, LANES), F32)
    for sel in sels:
        onehot = onehot + sel.astype(F32)
    below = (lax.broadcasted_iota(jnp.int32, (ts, ts), 0) > lax.broadcasted_iota(jnp.int32, (ts, ts), 1))
    before = _dot(jnp.where(below, 1.0, 0.0).astype(BF16), onehot.astype(BF16)) + carry_scr[0:1, :]
    route = jnp.zeros((ts, LANES), F32)
    for k in range(TOP_K):
        rank = jnp.sum(jnp.where(sels[k], before, 0.0), -1, keepdims=True)
        route = jnp.where(lane == float(k), idxs[k], route)
        route = jnp.where(lane == float(TOP_K + k), exps[k] / den, route)
        route = jnp.where(lane == float(2 * TOP_K + k), rank, route)
    route_ref[...] = route
    carry_scr[...] = carry_scr[...] + jnp.sum(onehot, 0, keepdims=True)
    cnt_ref[...] = carry_scr[...]


def _xattn_layer(h, mem, wq, wk, wv, wo, ln_g, ln_b, w_router, b_router):
    bsz, seq, d = h.shape
    mlen = mem.shape[1]
    ts = XATTN_TILE
    hd = d // XA_HEADS
    n_exp = w_router.shape[1]
    wr = jnp.pad(w_router.astype(F32), ((0, 0), (0, LANES - n_exp)))
    wr_hi = wr.astype(BF16)
    wr_lo = (wr - wr_hi.astype(F32)).astype(BF16)
    br = jnp.pad(b_router.astype(F32), (0, LANES - n_exp), constant_values=-jnp.inf).reshape(1, LANES)
    tile = pl.BlockSpec((None, ts, d), lambda b, s: (b, s, 0))
    rtile = pl.BlockSpec((None, ts, LANES), lambda b, s: (b, s, 0))
    return pl.pallas_call(
        functools.partial(_xattn_kernel, ts=ts, hd=hd),
        out_shape=(jax.ShapeDtypeStruct((bsz, seq, d), F32),
                   jax.ShapeDtypeStruct((bsz, seq, LANES), F32), jax.ShapeDtypeStruct((8, LANES), F32)),
        grid=(bsz, seq // ts),
        in_specs=[tile, pl.BlockSpec((None, mlen, d), lambda b, s: (b, 0, 0)),
                  _const_spec((d, d)), _const_spec((d, d)), _const_spec((d, d)), _const_spec((d, d)),
                  _const_spec((1, d)), _const_spec((1, d)), _const_spec((d, LANES)), _const_spec((d, LANES)),
                  _const_spec((1, LANES))],
        out_specs=(tile, rtile, pl.BlockSpec((8, LANES), lambda b, s: (0, 0))),
        scratch_shapes=[pltpu.VMEM((mlen, d), BF16), pltpu.VMEM((mlen, d), BF16), pltpu.VMEM((8, LANES), F32)],
        compiler_params=_params(2),
        name="xattn_ln_router",
    )(h, mem, wq.astype(BF16), wk.astype(BF16), wv.astype(BF16), wo.astype(BF16),
      ln_g.reshape(1, d), ln_b.reshape(1, d), wr_hi, wr_lo, br)


def _rows_from_tiles(ref, n):
    return jnp.concatenate([ref[pl.ds(c, n, stride=SUBLANES), :] for c in range(SUBLANES)], axis=1)


def _rows_to_tiles(ref, v, n):
    for c in range(SUBLANES):
        ref[pl.ds(c, n, stride=SUBLANES), :] = v[:, c * LANES:(c + 1) * LANES]


def _moe_kernel(be_ref, nu_ref, nv_ref, x_ref, wgu_ref, bg_ref, bu_ref, wd_ref, bd_ref, o_ref, wg_scr, wu_scr, wd_scr,
                *, tm):
    i = pl.program_id(0)
    active = i < nu_ref[0]
    new_expert = (i == 0) | (be_ref[i] != be_ref[jnp.maximum(i - 1, 0)])

    @pl.when(active & new_expert)
    def _():
        w2 = 2 * LANES
        src = lax.broadcasted_iota(jnp.int32, (w2, w2), 0)
        dst = lax.broadcasted_iota(jnp.int32, (w2, w2), 1)
        perm = jnp.where(src == jnp.where(dst < LANES, 2 * dst, 2 * (dst - LANES) + 1), 1.0, 0.0).astype(BF16)
        for c in range(wgu_ref.shape[1] // w2):
            res = _dot(wgu_ref[:, c * w2:(c + 1) * w2].astype(BF16), perm)
            wg_scr[:, c * LANES:(c + 1) * LANES] = res[:, :LANES].astype(BF16)
            wu_scr[:, c * LANES:(c + 1) * LANES] = res[:, LANES:].astype(BF16)
        wd_scr[...] = wd_ref[...].astype(BF16)

    nv = nv_ref[i]
    for r in range(MOE_ROW_GROUP, tm + 1, MOE_ROW_GROUP):
        @pl.when(active & (nv > r - MOE_ROW_GROUP) & (nv <= r))
        def _():
            x = _rows_from_tiles(x_ref, r).astype(BF16)
            gate = jnp.minimum(_dot(x, wg_scr[...]) + bg_ref[...], SWIGLU_LIMIT)
            up = jnp.clip(_dot(x, wu_scr[...]) + bu_ref[...], -SWIGLU_LIMIT, SWIGLU_LIMIT)
            act = (up + 1.0) * (gate * _sigmoid(SWIGLU_ALPHA * gate))
            _rows_to_tiles(o_ref, _dot(act.astype(BF16), wd_scr[...]) + bd_ref[...], r)
            if r < tm:
                o_ref[r * SUBLANES:, :] = jnp.zeros(((tm - r) * SUBLANES, LANES), F32)

    @pl.when(jnp.logical_not(active))
    def _():
        o_ref[...] = jnp.zeros(o_ref.shape, F32)


def _moe_experts(xs, block_e, n_used, n_valid, layer, w_gate_up, b_gate, b_up, w_down, b_down):
    _, n_exp, f, d = w_down.shape
    n_rows = xs.shape[0] // SUBLANES
    tm = MOE_TILE
    n_blocks = n_rows // tm

    def row_map(i, be, nu, nv):
        return (jnp.minimum(i, nu[0] - 1), 0)

    def w_map(i, be, nu, nv):
        return (be[i], 0, 0)

    def lw_map(i, be, nu, nv):
        return (layer, be[i], 0, 0)

    grid_spec = pltpu.PrefetchScalarGridSpec(
        num_scalar_prefetch=3,
        grid=(n_blocks,),
        in_specs=[pl.BlockSpec((tm * SUBLANES, LANES), row_map),
                  pl.BlockSpec((None, None, d, 2 * f), lw_map),
                  pl.BlockSpec((None, 1, f), w_map), pl.BlockSpec((None, 1, f), w_map),
                  pl.BlockSpec((None, None, f, d), lw_map), pl.BlockSpec((None, 1, d), w_map)],
        out_specs=pl.BlockSpec((tm * SUBLANES, LANES), lambda i, be, nu, nv: (i, 0)),
        scratch_shapes=[pltpu.VMEM((d, f), BF16), pltpu.VMEM((d, f), BF16), pltpu.VMEM((f, d), BF16)],
    )
    return pl.pallas_call(
        functools.partial(_moe_kernel, tm=tm),
        out_shape=jax.ShapeDtypeStruct((n_rows * SUBLANES, LANES), F32),
        grid_spec=grid_spec,
        compiler_params=_params(1),
        name="moe_experts",
    )(block_e, n_used, n_valid, xs, w_gate_up, b_gate.reshape(n_exp, 1, f), b_up.reshape(n_exp, 1, f),
      w_down, b_down.reshape(n_exp, 1, d))


def _row_copy(src, src_row, dst, dst_row, sem):
    return pltpu.make_async_copy(src.at[pl.ds(pl.multiple_of(src_row * SUBLANES, SUBLANES), SUBLANES)],
                                 dst.at[pl.ds(pl.multiple_of(dst_row * SUBLANES, SUBLANES), SUBLANES)], sem)


def _dispatch_kernel(dest_ref, zblk_ref, h_ref, xs_ref, stage0, stage1, zero_scr, sem, zsem, *, ts, n_tiles, tm):
    i = pl.program_id(0)

    @pl.when(i == 0)
    def _():
        zero_scr[...] = jnp.zeros(zero_scr.shape, F32)
        for j in range(zblk_ref.shape[0]):
            @pl.when(zblk_ref[j] >= 0)
            def _():
                start = pl.multiple_of(zblk_ref[j] * (tm * SUBLANES), tm * SUBLANES)
                pltpu.make_async_copy(zero_scr, xs_ref.at[pl.ds(start, tm * SUBLANES)], zsem).start()
        for j in range(zblk_ref.shape[0]):
            @pl.when(zblk_ref[j] >= 0)
            def _():
                pltpu.make_async_copy(zero_scr, xs_ref.at[pl.ds(0, tm * SUBLANES)], zsem).wait()

    def drain(stage, s):
        for _ in range(TOP_K):
            pltpu.make_async_copy(stage, xs_ref.at[pl.ds(0, ts * SUBLANES)], sem.at[s]).wait()

    def step(stage, s, other, so):
        @pl.when(i >= 2)
        def _():
            drain(stage, s)

        _rows_to_tiles(stage, h_ref[...], ts)

        def issue(j, carry):
            for u in range(ROWS_PER_ISSUE):
                tok = j * ROWS_PER_ISSUE + u
                for k in range(TOP_K):
                    row = dest_ref[(i * ts + tok) * TOP_K + k]
                    _row_copy(stage, tok, xs_ref, row, sem.at[s]).start(priority=k % 2)
            return carry

        lax.fori_loop(0, ts // ROWS_PER_ISSUE, issue, 0)

        @pl.when(i == n_tiles - 1)
        def _():
            drain(other, so)
            drain(stage, s)

    @pl.when(i % 2 == 0)
    def _():
        step(stage0, 0, stage1, 1)

    @pl.when(i % 2 == 1)
    def _():
        step(stage1, 1, stage0, 0)


def _dispatch_rows(h, dest, zero_blocks, n_rows):
    t, d = h.shape
    ts = DISPATCH_TILE
    tm = MOE_TILE
    n_tiles = t // ts
    assert d == SUBLANES * LANES and n_tiles >= 2
    grid_spec = pltpu.PrefetchScalarGridSpec(
        num_scalar_prefetch=2,
        grid=(n_tiles,),
        in_specs=[pl.BlockSpec((ts, d), lambda i, dest, zb: (i, 0))],
        out_specs=pl.BlockSpec(memory_space=pl.ANY),
        scratch_shapes=[pltpu.VMEM((ts * SUBLANES, LANES), F32), pltpu.VMEM((ts * SUBLANES, LANES), F32),
                        pltpu.VMEM((tm * SUBLANES, LANES), F32),
                        pltpu.SemaphoreType.DMA((2,)), pltpu.SemaphoreType.DMA(())],
    )
    return pl.pallas_call(
        functools.partial(_dispatch_kernel, ts=ts, n_tiles=n_tiles, tm=tm),
        out_shape=jax.ShapeDtypeStruct((n_rows * SUBLANES, LANES), F32),
        grid_spec=grid_spec,
        compiler_params=_params(1),
        name="moe_dispatch",
    )(dest, zero_blocks, h)


def _combine_kernel(dest_ref, h_ref, route_ref, g_ref, b_ref, y_ref, o_ref, gbuf0, gbuf1, sem, *, ts, n_tiles):
    i = pl.program_id(0)

    def issue(tile, gbuf, s):
        def body(j, carry):
            for u in range(ROWS_PER_ISSUE):
                tok = j * ROWS_PER_ISSUE + u
                for k in range(TOP_K):
                    row = dest_ref[(tile * ts + tok) * TOP_K + k]
                    _row_copy(y_ref, row, gbuf.at[k], tok, sem.at[s]).start(priority=k % 2)
            return carry

        lax.fori_loop(0, ts // ROWS_PER_ISSUE, body, 0)

    def drain(gbuf, s):
        for k in range(TOP_K):
            pltpu.make_async_copy(y_ref.at[pl.ds(0, ts * SUBLANES)], gbuf.at[k], sem.at[s]).wait()

    def step(gbuf, s, nxt, sn):
        @pl.when(i + 1 < n_tiles)
        def _():
            issue(i + 1, nxt, sn)

        drain(gbuf, s)
        parts = []
        for c in range(SUBLANES):
            acc = None
            for k in range(TOP_K):
                piece = gbuf[k, pl.ds(c, ts, stride=SUBLANES), :] * route_ref[:, TOP_K + k:TOP_K + k + 1]
                acc = piece if acc is None else acc + piece
            parts.append(acc)
        ff = jnp.concatenate(parts, axis=1)
        o_ref[...] = _layer_norm(DEEPNORM_ALPHA * h_ref[...] + ff, g_ref[...], b_ref[...])

    @pl.when(i == 0)
    def _():
        issue(0, gbuf0, 0)

    @pl.when(i % 2 == 0)
    def _():
        step(gbuf0, 0, gbuf1, 1)

    @pl.when(i % 2 == 1)
    def _():
        step(gbuf1, 1, gbuf0, 0)


def _combine_layer(h, y, dest, route, ln_g, ln_b):
    t, d = h.shape
    ts = COMBINE_TILE
    n_tiles = t // ts
    grid_spec = pltpu.PrefetchScalarGridSpec(
        num_scalar_prefetch=1,
        grid=(n_tiles,),
        in_specs=[pl.BlockSpec((ts, d), lambda i, dest: (i, 0)), pl.BlockSpec((ts, LANES), lambda i, dest: (i, 0)),
                  pl.BlockSpec((1, d), lambda i, dest: (0, 0)), pl.BlockSpec((1, d), lambda i, dest: (0, 0)),
                  pl.BlockSpec(memory_space=pl.ANY)],
        out_specs=pl.BlockSpec((ts, d), lambda i, dest: (i, 0)),
        scratch_shapes=[pltpu.VMEM((TOP_K, ts * SUBLANES, LANES), F32),
                        pltpu.VMEM((TOP_K, ts * SUBLANES, LANES), F32), pltpu.SemaphoreType.DMA((2,))],
    )
    return pl.pallas_call(
        functools.partial(_combine_kernel, ts=ts, n_tiles=n_tiles),
        out_shape=jax.ShapeDtypeStruct((t, d), F32),
        grid_spec=grid_spec,
        compiler_params=_params(1),
        name="moe_combine_ln",
    )(dest, h, route, ln_g.reshape(1, d), ln_b.reshape(1, d), y)


def _moe_layer(h2, route, cnt, layer, w_gate_up, b_gate_up, w_down, b_down, ln_g, ln_b):
    bsz, seq, d = h2.shape
    t = bsz * seq
    tm = MOE_TILE
    n_exp = w_gate_up.shape[1]
    h2 = h2.reshape(t, d)
    route = route.reshape(t, LANES)
    idx = route[:, 0:TOP_K].astype(jnp.int32)
    rank = route[:, 2 * TOP_K:3 * TOP_K].astype(jnp.int32)
    counts = cnt[0, :n_exp].astype(jnp.int32)
    padded = (counts + tm - 1) // tm * tm
    pend = jnp.cumsum(padded)
    pstart = pend - padded
    part = counts % tm
    skip = jnp.where(part > 0, tm - part, 0)
    dest = (pstart[idx] + rank + jnp.where(rank >= part[idx], skip[idx], 0)).reshape(-1)
    n_blocks = (t * TOP_K) // tm + n_exp
    n_used = (pend[-1] // tm).astype(jnp.int32)
    blk = jnp.arange(n_blocks, dtype=jnp.int32)
    block_e = jnp.sum((pend[None, :] <= (blk * tm)[:, None]).astype(jnp.int32), axis=1)
    block_e = jnp.minimum(block_e, n_exp - 1)
    block_e = jnp.where(blk < n_used, block_e, block_e[jnp.maximum(n_used - 1, 0)])
    first_blk = jnp.where(part > 0, pstart // tm, -1)
    tail_blk = n_used + jnp.arange(n_exp, dtype=jnp.int32)
    zero_blocks = jnp.concatenate([first_blk, jnp.where(tail_blk < n_blocks, tail_blk, -1)]).astype(jnp.int32)
    xs = _dispatch_rows(h2, dest, zero_blocks, n_blocks * tm)
    n_valid = jnp.where((blk == first_blk[block_e]) & (blk < n_used), part[block_e], tm).astype(jnp.int32)
    y = _moe_experts(xs, block_e, n_used.reshape(1), n_valid, layer, w_gate_up, b_gate_up[:, 0::2], b_gate_up[:, 1::2],
                     w_down, b_down)
    return _combine_layer(h2, y, dest, route, ln_g, ln_b).reshape(bsz, seq, d)


def kernel(x, mem, pool_w_in, pool_w_grp, pool_scale, pool_w_out, ssm_w_in, ssm_conv_w, ssm_conv_b, ssm_dt_bias, ssm_a_log, ssm_d, ssm_norm_g, ssm_w_out, xa_wq, xa_wk, xa_wv, xa_wo, moe_w_router, moe_b_router, moe_w_gate_up, moe_b_gate_up, moe_w_down, moe_b_down, ln_mix_g, ln_mix_b, ln_xa_g, ln_xa_b, ln_ffn_g, ln_ffn_b):
    h = x
    for i in range(DEPTH):
        j = i // 2
        if i % 2 == 0:
            h = _pool_layer(h, pool_w_in[j], pool_w_grp[j], pool_scale[j], pool_w_out[j], ln_mix_g[i], ln_mix_b[i])
        else:
            h = _ssd_layer(h, ssm_w_in[j], ssm_conv_w[j], ssm_conv_b[j], ssm_dt_bias[j], ssm_a_log[j], ssm_d[j],
                           ssm_norm_g[j], ssm_w_out[j], ln_mix_g[i], ln_mix_b[i])
        h2, route, cnt = _xattn_layer(h, mem, xa_wq[i], xa_wk[i], xa_wv[i], xa_wo[i], ln_xa_g[i], ln_xa_b[i],
                                      moe_w_router[i], moe_b_router[i])
        h = _moe_layer(h2, route, cnt, i, moe_w_gate_up, moe_b_gate_up[i], moe_w_down, moe_b_down[i],
                       ln_ffn_g[i], ln_ffn_b[i])
    return h
```

```python
import functools

import jax
import jax.numpy as jnp
from jax import lax
from jax.experimental import pallas as pl
from jax.experimental.pallas import tpu as pltpu

F32 = jnp.float32
BF16 = jnp.bfloat16

DEPTH = 2
DEEPNORM_ALPHA = (2 * DEPTH) ** 0.25
LN_EPS = 1e-5
POOL_WINDOWS = (2, 4, 8, 16)
POOL_HALO = 16
SSM_HEAD_DIM = 64
SSM_N_GROUPS = 8
SSM_HEADS_PER_GROUP = 4
SSM_D_STATE = 128
SSM_CONV = 4
CONV_HALO = 8
SSM_NORM_EPS = 1e-5
XA_HEADS = 4
N_EXPERTS = 32
TOP_K = 4
SWIGLU_LIMIT = 7.0
SWIGLU_ALPHA = 1.702

LANES = 128
SUBLANES = 8
VMEM_LIMIT_BYTES = 56 * 1024 * 1024

POOL_TILE = 512
XATTN_TILE = 512
SSD_TILE = 256
SSD_CHUNK = 128
MOE_TILE = 512
MOE_ROW_GROUP = 128
DISPATCH_TILE = 256
COMBINE_TILE = 256
ROWS_PER_ISSUE = 8

NEG_BIG = -1e30


def _layer_norm(v, g, b):
    mu = jnp.mean(v, -1, keepdims=True)
    d = v - mu
    var = jnp.mean(d * d, -1, keepdims=True)
    return d * lax.rsqrt(var + LN_EPS) * g + b


def _dot(a, b):
    return jnp.dot(a, b, preferred_element_type=F32)


def _dot_nt(a, b):
    return lax.dot_general(a, b, (((1,), (1,)), ((), ())), preferred_element_type=F32)


def _dot_tn(a, b):
    return lax.dot_general(a, b, (((0,), (0,)), ((), ())), preferred_element_type=F32)


def _const_spec(shape):
    nd = len(shape)
    return pl.BlockSpec(shape, lambda *_: (0,) * nd, pipeline_mode=pl.Buffered(1))


def _params(n_axes):
    return pltpu.CompilerParams(dimension_semantics=("arbitrary",) * n_axes,
                                vmem_limit_bytes=VMEM_LIMIT_BYTES)


def _pool_kernel(x_ref, win_ref, wgrp_ref, scale_ref, wout_ref, g_ref, b_ref, o_ref, ext_ref, *, ts, gd):
    s = pl.program_id(1)

    @pl.when(s == 0)
    def _():
        ext_ref[0:POOL_HALO, :] = jnp.zeros((POOL_HALO, ext_ref.shape[1]), F32)

    x = x_ref[...]
    ext_ref[POOL_HALO:, :] = _dot(x.astype(BF16), win_ref[...])
    pos = lax.broadcasted_iota(jnp.int32, (ts, 1), 0) + s * ts
    mix = None
    for g, w in enumerate(POOL_WINDOWS):
        cols = slice(g * gd, (g + 1) * gd)
        e = ext_ref[:, cols]
        acc = e
        sh = 1
        while sh < w:
            acc = acc + pltpu.roll(acc, sh, axis=0)
            sh *= 2
        cnt = jnp.minimum(pos + 1, w).astype(F32)
        m = acc[POOL_HALO:, :] / cnt - e[POOL_HALO:, :]
        yg = _dot(m.astype(BF16), wgrp_ref[g]) * scale_ref[:, cols]
        part = _dot(yg.astype(BF16), wout_ref[cols, :])
        mix = part if mix is None else mix + part
    ext_ref[0:POOL_HALO, :] = ext_ref[ts:ts + POOL_HALO, :]
    o_ref[...] = _layer_norm(DEEPNORM_ALPHA * x + mix, g_ref[...], b_ref[...])


def _pool_layer(h, w_in, w_grp, scale, w_out, ln_g, ln_b):
    bsz, seq, d = h.shape
    ts = POOL_TILE
    gd = d // len(POOL_WINDOWS)
    tile = pl.BlockSpec((None, ts, d), lambda b, s: (b, s, 0))
    return pl.pallas_call(
        functools.partial(_pool_kernel, ts=ts, gd=gd),
        out_shape=jax.ShapeDtypeStruct((bsz, seq, d), F32),
        grid=(bsz, seq // ts),
        in_specs=[tile, _const_spec((d, d)), _const_spec((len(POOL_WINDOWS), gd, gd)), _const_spec((1, d)),
                  _const_spec((d, d)), _const_spec((1, d)), _const_spec((1, d))],
        out_specs=tile,
        scratch_shapes=[pltpu.VMEM((POOL_HALO + ts, d), F32)],
        compiler_params=_params(2),
        name="pool_mixer_ln",
    )(h, w_in.astype(BF16), w_grp.astype(BF16), scale.reshape(1, d), w_out.astype(BF16),
      ln_g.reshape(1, d), ln_b.reshape(1, d))


def _sigmoid(v):
    return 1.0 / (1.0 + jnp.exp(-v))


def _ssd_kernel(x_ref, wz_ref, wx_ref, wb_ref, wc_ref, wdt_ref, convw_ref, convb_ref, dtb_ref, alog_ref,
                dexp_ref, normg_ref, expand_ref, wout_ref, g_ref, b_ref, o_ref,
                xbc_scr, z_scr, actx_scr, bmat_scr, cmat_scr, y_scr, state_scr, *, ts, q, d_inner, gn):
    s = pl.program_id(1)
    conv_dim = d_inner + 2 * gn
    gw = d_inner // SSM_N_GROUPS

    @pl.when(s == 0)
    def _():
        xbc_scr[:, 0:CONV_HALO, :] = jnp.zeros((conv_dim // LANES, CONV_HALO, LANES), F32)
        state_scr[...] = jnp.zeros(state_scr.shape, F32)

    x = x_ref[...]
    xb = x.astype(BF16)
    z_scr[...] = _dot(xb, wz_ref[...])
    strip = 512
    for w_ref, base in ((wx_ref, 0), (wb_ref, d_inner), (wc_ref, d_inner + gn)):
        for c in range(0, w_ref.shape[1], strip):
            res = _dot(xb, w_ref[:, c:c + strip])
            for j in range(strip // LANES):
                xbc_scr[(base + c) // LANES + j, CONV_HALO:, :] = res[:, j * LANES:(j + 1) * LANES]
    dt_raw = _dot(xb, wdt_ref[...]) + dtb_ref[...]
    dtv = jnp.maximum(dt_raw, 0.0) + jnp.log1p(jnp.exp(-jnp.abs(dt_raw)))
    a_all = dtv * (-jnp.exp(alog_ref[...]))

    for j in range(conv_dim // LANES):
        cs = slice(j * LANES, (j + 1) * LANES)
        acc = convb_ref[:, cs]
        for k in range(SSM_CONV):
            r0 = CONV_HALO - (SSM_CONV - 1) + k
            acc = acc + convw_ref[k:k + 1, cs] * xbc_scr[j, r0:r0 + ts, :]
        act = acc * _sigmoid(acc)
        c = j * LANES
        if c < d_inner:
            actx_scr[:, cs] = act
        elif c < d_inner + gn:
            bmat_scr[:, c - d_inner:c - d_inner + LANES] = act.astype(BF16)
        else:
            cmat_scr[:, c - d_inner - gn:c - d_inner - gn + LANES] = act.astype(BF16)
    xbc_scr[:, 0:CONV_HALO, :] = xbc_scr[:, ts:ts + CONV_HALO, :]

    expand = expand_ref[...]
    row_i = lax.broadcasted_iota(jnp.int32, (q, LANES), 0)
    causal = lax.broadcasted_iota(jnp.int32, (q, q), 0) >= lax.broadcasted_iota(jnp.int32, (q, q), 1)
    head_of_lane = lax.broadcasted_iota(jnp.int32, (q, gw), 1) // SSM_HEAD_DIM
    head_mask = [jnp.where(head_of_lane == r, 1.0, 0.0).astype(BF16) for r in range(SSM_HEADS_PER_GROUP)]

    for c in range(ts // q):
        rows = slice(c * q, (c + 1) * q)
        acs = a_all[rows, :]
        sh = 1
        while sh < q:
            acs = acs + jnp.where(row_i >= sh, pltpu.roll(acs, sh, axis=0), 0.0)
            sh *= 2
        acs_t = acs.T
        a_last = acs[q - 1:q, :]
        e_in = _dot(jnp.exp(acs).astype(BF16), expand)
        dec = _dot(jnp.exp(a_last - acs).astype(BF16), expand)
        dtx = _dot(dtv[rows, :].astype(BF16), expand)
        cd = jnp.broadcast_to(jnp.exp(a_last), (8, LANES))
        cd_hi = cd.astype(BF16)
        cd_lo = (cd - cd_hi.astype(F32)).astype(BF16)
        cdx = (_dot(cd_hi, expand) + _dot(cd_lo, expand))[0:1, :]
        xd = actx_scr[rows, :] * dtx
        xdd = xd * dec
        xdb = xd.astype(BF16)
        for g in range(SSM_N_GROUPS):
            gc = slice(g * gw, (g + 1) * gw)
            nc = slice(g * SSM_D_STATE, (g + 1) * SSM_D_STATE)
            bg = bmat_scr[rows, nc]
            cg = cmat_scr[rows, nc]
            cb = _dot_nt(cg, bg)
            xg = xdb[:, gc]
            mhs, xms = [], []
            for r in range(SSM_HEADS_PER_GROUP):
                hd = g * SSM_HEADS_PER_GROUP + r
                seg = acs[:, hd:hd + 1] - acs_t[hd:hd + 1, :]
                lmat = jnp.exp(jnp.where(causal, seg, NEG_BIG))
                mhs.append((cb * lmat).astype(BF16))
                xms.append(xg * head_mask[r])
            yg = (_dot(cg, state_scr[g].astype(BF16)) * e_in[:, gc]
                  + _dot(jnp.concatenate(mhs, axis=1), jnp.concatenate(xms, axis=0)))
            y_scr[rows, gc] = yg
            state_scr[g] = state_scr[g] * cdx[:, gc] + _dot_tn(bg, xdd[:, gc].astype(BF16))

    mix = None
    for g in range(SSM_N_GROUPS):
        gc = slice(g * gw, (g + 1) * gw)
        zz = z_scr[:, gc]
        yv = (y_scr[:, gc] + dexp_ref[:, gc] * actx_scr[:, gc]) * (zz * _sigmoid(zz))
        yv = yv * lax.rsqrt(jnp.mean(yv * yv, -1, keepdims=True) + SSM_NORM_EPS) * normg_ref[:, gc]
        part = _dot(yv.astype(BF16), wout_ref[gc, :])
        mix = part if mix is None else mix + part
    o_ref[...] = _layer_norm(DEEPNORM_ALPHA * x + mix, g_ref[...], b_ref[...])


def _ssd_layer(h, w_in, conv_w, conv_b, dt_bias, a_log, d_skip, norm_g, w_out, ln_g, ln_b):
    bsz, seq, d = h.shape
    n_heads = a_log.shape[0]
    d_inner = n_heads * SSM_HEAD_DIM
    gn = SSM_N_GROUPS * SSM_D_STATE
    conv_dim = d_inner + 2 * gn
    ts, q = SSD_TILE, SSD_CHUNK
    w_in = w_in.astype(BF16)
    wz = w_in[:, :d_inner]
    wx = w_in[:, d_inner:2 * d_inner]
    wb = w_in[:, 2 * d_inner:2 * d_inner + gn]
    wc = w_in[:, 2 * d_inner + gn:2 * d_inner + 2 * gn]
    pad = LANES - n_heads
    wdt = jnp.pad(w_in[:, d_inner + conv_dim:], ((0, 0), (0, pad)))
    dtb = jnp.pad(dt_bias.astype(F32), (0, pad)).reshape(1, LANES)
    alog = jnp.pad(a_log.astype(F32), (0, pad)).reshape(1, LANES)
    dexp = jnp.repeat(d_skip.astype(F32), SSM_HEAD_DIM).reshape(1, d_inner)
    expand = (jnp.arange(LANES)[:, None] == (jnp.arange(d_inner)[None, :] // SSM_HEAD_DIM)).astype(BF16)
    tile = pl.BlockSpec((None, ts, d), lambda b, s: (b, s, 0))
    return pl.pallas_call(
        functools.partial(_ssd_kernel, ts=ts, q=q, d_inner=d_inner, gn=gn),
        out_shape=jax.ShapeDtypeStruct((bsz, seq, d), F32),
        grid=(bsz, seq // ts),
        in_specs=[tile, _const_spec((d, d_inner)), _const_spec((d, d_inner)), _const_spec((d, gn)),
                  _const_spec((d, gn)), _const_spec((d, LANES)), _const_spec((SSM_CONV, conv_dim)),
                  _const_spec((1, conv_dim)), _const_spec((1, LANES)), _const_spec((1, LANES)),
                  _const_spec((1, d_inner)), _const_spec((1, d_inner)), _const_spec((LANES, d_inner)),
                  _const_spec((d_inner, d)), _const_spec((1, d)), _const_spec((1, d))],
        out_specs=tile,
        scratch_shapes=[pltpu.VMEM((conv_dim // LANES, CONV_HALO + ts, LANES), F32), pltpu.VMEM((ts, d_inner), F32),
                        pltpu.VMEM((ts, d_inner), F32), pltpu.VMEM((ts, gn), BF16), pltpu.VMEM((ts, gn), BF16),
                        pltpu.VMEM((ts, d_inner), F32),
                        pltpu.VMEM((SSM_N_GROUPS, SSM_D_STATE, d_inner // SSM_N_GROUPS), F32)],
        compiler_params=_params(2),
        name="ssd_mixer_ln",
    )(h, wz, wx, wb, wc, wdt, conv_w.astype(F32), conv_b.reshape(1, conv_dim), dtb, alog, dexp,
      norm_g.reshape(1, d_inner), expand, w_out.astype(BF16), ln_g.reshape(1, d), ln_b.reshape(1, d))


def _xattn_kernel(h_ref, mem_ref, wq_ref, wk_ref, wv_ref, wo_ref, g_ref, b_ref, wrh_ref, wrl_ref, br_ref,
                  o_ref, route_ref, cnt_ref, k_scr, v_scr, carry_scr, *, ts, hd):
    b = pl.program_id(0)
    s = pl.program_id(1)

    @pl.when(s == 0)
    def _():
        mb = mem_ref[...].astype(BF16)
        k_scr[...] = _dot(mb, wk_ref[...]).astype(BF16)
        v_scr[...] = _dot(mb, wv_ref[...]).astype(BF16)

    @pl.when((b == 0) & (s == 0))
    def _():
        carry_scr[...] = jnp.zeros(carry_scr.shape, F32)

    h = h_ref[...]
    qv = (_dot(h.astype(BF16), wq_ref[...]) * (hd ** -0.5)).astype(BF16)
    heads = []
    for hh in range(XA_HEADS):
        cols = slice(hh * hd, (hh + 1) * hd)
        sc = _dot_nt(qv[:, cols], k_scr[:, cols])
        p = jnp.exp(sc - jnp.max(sc, -1, keepdims=True))
        o = _dot(p.astype(BF16), v_scr[:, cols]) / jnp.sum(p, -1, keepdims=True)
        heads.append(o.astype(BF16))
    xa = _dot(jnp.concatenate(heads, axis=1), wo_ref[...])
    h2 =_layer_norm(DEEPNORM_ALPHA * h + xa, g_ref[...], b_ref[...])
    o_ref[...] = h2
    h2_hi = h2.astype(BF16)

    h2_lo = (h2 - h2_hi.astype(F32)).astype(BF16)
    logits = _dot(h2_hi, wrh_ref[...]) + _dot(h2_lo, wrh_ref[...]) + _dot(h2_hi, wrl_ref[...]) + br_ref[...]
    lane = lax.broadcasted_iota(jnp.int32, (ts, LANES), 1).astype(F32)
    work = logits
    vals, idxs, sels = [], [], []
    for _ in range(TOP_K):
        m = jnp.max(work, -1, keepdims=True)
        ik = jnp.min(jnp.where(work == m, lane, float(LANES)), -1, keepdims=True)
        sel = lane == ik
        vals.append(m)
        idxs.append(ik)
        sels.append(sel)
        work = jnp.where(sel, -jnp.inf, work)
    exps = [jnp.exp(v - vals[0]) for v in vals]
    den = exps[0]
    for e in exps[1:]:
        den = den + e
    onehot = jnp.zeros((ts, LANES), F32)
    for sel in sels:
        onehot = onehot + sel.astype(F32)
    below = (lax.broadcasted_iota(jnp.int32, (ts, ts), 0) > lax.broadcasted_iota(jnp.int32, (ts, ts), 1))
    before = _dot(jnp.where(below, 1.0, 0.0).astype(BF16), onehot.astype(BF16)) + carry_scr[0:1, :]
    route = jnp.zeros((ts, LANES), F32)
    for k in range(TOP_K):
        rank = jnp.sum(jnp.where(sels[k], before, 0.0), -1, keepdims=True)
        route = jnp.where(lane == float(k), idxs[k], route)
        route = jnp.where(lane == float(TOP_K + k), exps[k] / den, route)
        route = jnp.where(lane == float(2 * TOP_K + k), rank, route)
    route_ref[...] = route
    carry_scr[...] = carry_scr[...] + jnp.sum(onehot, 0, keepdims=True)
    cnt_ref[...] = carry_scr[...]


def _xattn_layer(h, mem, wq, wk, wv, wo, ln_g, ln_b, w_router, b_router):
    bsz, seq, d = h.shape
    mlen = mem.shape[1]
    ts = XATTN_TILE
    hd = d // XA_HEADS
    n_exp = w_router.shape[1]
    wr = jnp.pad(w_router.astype(F32), ((0, 0), (0, LANES - n_exp)))
    wr_hi = wr.astype(BF16)
    wr_lo = (wr - wr_hi.astype(F32)).astype(BF16)
    br = jnp.pad(b_router.astype(F32), (0, LANES - n_exp), constant_values=-jnp.inf).reshape(1, LANES)
    tile = pl.BlockSpec((None, ts, d), lambda b, s: (b, s, 0))
    rtile = pl.BlockSpec((None, ts, LANES), lambda b, s: (b, s, 0))
    return pl.pallas_call(
        functools.partial(_xattn_kernel, ts=ts, hd=hd),
        out_shape=(jax.ShapeDtypeStruct((bsz, seq, d), F32),
                   jax.ShapeDtypeStruct((bsz, seq, LANES), F32), jax.ShapeDtypeStruct((8, LANES), F32)),
        grid=(bsz, seq // ts),
        in_specs=[tile, pl.BlockSpec((None, mlen, d), lambda b, s: (b, 0, 0)),
                  _const_spec((d, d)), _const_spec((d, d)), _const_spec((d, d)), _const_spec((d, d)),
                  _const_spec((1, d)), _const_spec((1, d)), _const_spec((d, LANES)), _const_spec((d, LANES)),
                  _const_spec((1, LANES))],
        out_specs=(tile, rtile, pl.BlockSpec((8, LANES), lambda b, s: (0, 0))),
        scratch_shapes=[pltpu.VMEM((mlen, d), BF16), pltpu.VMEM((mlen, d), BF16), pltpu.VMEM((8, LANES), F32)],
        compiler_params=_params(2),
        name="xattn_ln_router",
    )(h, mem, wq.astype(BF16), wk.astype(BF16), wv.astype(BF16), wo.astype(BF16),
      ln_g.reshape(1, d), ln_b.reshape(1, d), wr_hi, wr_lo, br)


def _rows_from_tiles(ref, n):
    return jnp.concatenate([ref[pl.ds(c, n, stride=SUBLANES), :] for c in range(SUBLANES)], axis=1)


def _rows_to_tiles(ref, v, n):
    for c in range(SUBLANES):
        ref[pl.ds(c, n, stride=SUBLANES), :] = v[:, c * LANES:(c + 1) * LANES]


def _moe_kernel(be_ref, nu_ref, nv_ref, x_ref, wgu_ref, bg_ref, bu_ref, wd_ref, bd_ref, o_ref, wg_scr, wu_scr, wd_scr,
                *, tm):
    i = pl.program_id(0)
    active = i < nu_ref[0]
    new_expert = (i == 0) | (be_ref[i] != be_ref[jnp.maximum(i - 1, 0)])

    @pl.when(active & new_expert)
    def _():
        w2 = 2 * LANES
        src = lax.broadcasted_iota(jnp.int32, (w2, w2), 0)
        dst = lax.broadcasted_iota(jnp.int32, (w2, w2), 1)
        perm = jnp.where(src == jnp.where(dst < LANES, 2 * dst, 2 * (dst - LANES) + 1), 1.0, 0.0).astype(BF16)
        for c in range(wgu_ref.shape[1] // w2):
            res = _dot(wgu_ref[:, c * w2:(c + 1) * w2].astype(BF16), perm)
            wg_scr[:, c * LANES:(c + 1) * LANES] = res[:, :LANES].astype(BF16)
            wu_scr[:, c * LANES:(c + 1) * LANES] = res[:, LANES:].astype(BF16)
        wd_scr[...] = wd_ref[...].astype(BF16)

    nv = nv_ref[i]
    for r in range(MOE_ROW_GROUP, tm + 1, MOE_ROW_GROUP):
        @pl.when(active & (nv > r - MOE_ROW_GROUP) & (nv <= r))
        def _():
            x = _rows_from_tiles(x_ref, r).astype(BF16)
            gate = jnp.minimum(_dot(x, wg_scr[...]) + bg_ref[...], SWIGLU_LIMIT)
            up = jnp.clip(_dot(x, wu_scr[...]) + bu_ref[...], -SWIGLU_LIMIT, SWIGLU_LIMIT)
            act = (up + 1.0) * (gate * _sigmoid(SWIGLU_ALPHA * gate))
            _rows_to_tiles(o_ref, _dot(act.astype(BF16), wd_scr[...]) + bd_ref[...], r)
            if r < tm:
                o_ref[r * SUBLANES:, :] = jnp.zeros(((tm - r) * SUBLANES, LANES), F32)

    @pl.when(jnp.logical_not(active))
    def _():
        o_ref[...] = jnp.zeros(o_ref.shape, F32)


def _moe_experts(xs, block_e, n_used, n_valid, layer, w_gate_up, b_gate, b_up, w_down, b_down):
    _, n_exp, f, d = w_down.shape
    n_rows = xs.shape[0] // SUBLANES
    tm = MOE_TILE
    n_blocks = n_rows // tm

    def row_map(i, be, nu, nv):
        return (jnp.minimum(i, nu[0] - 1), 0)

    def w_map(i, be, nu, nv):
        return (be[i], 0, 0)

    def lw_map(i, be, nu, nv):
        return (layer, be[i], 0, 0)

    grid_spec = pltpu.PrefetchScalarGridSpec(
        num_scalar_prefetch=3,
        grid=(n_blocks,),
        in_specs=[pl.BlockSpec((tm * SUBLANES, LANES), row_map),
                  pl.BlockSpec((None, None, d, 2 * f), lw_map),
                  pl.BlockSpec((None, 1, f), w_map), pl.BlockSpec((None, 1, f), w_map),
                  pl.BlockSpec((None, None, f, d), lw_map), pl.BlockSpec((None, 1, d), w_map)],
        out_specs=pl.BlockSpec((tm * SUBLANES, LANES), lambda i, be, nu, nv: (i, 0)),
        scratch_shapes=[pltpu.VMEM((d, f), BF16), pltpu.VMEM((d, f), BF16), pltpu.VMEM((f, d), BF16)],
    )
    return pl.pallas_call(
        functools.partial(_moe_kernel, tm=tm),
        out_shape=jax.ShapeDtypeStruct((n_rows * SUBLANES, LANES), F32),
        grid_spec=grid_spec,
        compiler_params=_params(1),
        name="moe_experts",
    )(block_e, n_used, n_valid, xs, w_gate_up, b_gate.reshape(n_exp, 1, f), b_up.reshape(n_exp, 1, f),
      w_down, b_down.reshape(n_exp, 1, d))


def _row_copy(src, src_row, dst, dst_row, sem):
    return pltpu.make_async_copy(src.at[pl.ds(pl.multiple_of(src_row * SUBLANES, SUBLANES), SUBLANES)],
                                 dst.at[pl.ds(pl.multiple_of(dst_row * SUBLANES, SUBLANES), SUBLANES)], sem)


def _dispatch_kernel(dest_ref, zblk_ref, h_ref, xs_ref, stage0, stage1, zero_scr, sem, zsem, *, ts, n_tiles, tm):
    i = pl.program_id(0)

    @pl.when(i == 0)
    def _():
        zero_scr[...] = jnp.zeros(zero_scr.shape, F32)
        for j in range(zblk_ref.shape[0]):
            @pl.when(zblk_ref[j] >= 0)
            def _():
                start = pl.multiple_of(zblk_ref[j] * (tm * SUBLANES), tm * SUBLANES)
                pltpu.make_async_copy(zero_scr, xs_ref.at[pl.ds(start, tm * SUBLANES)], zsem).start()
        for j in range(zblk_ref.shape[0]):
            @pl.when(zblk_ref[j] >= 0)
            def _():
                pltpu.make_async_copy(zero_scr, xs_ref.at[pl.ds(0, tm * SUBLANES)], zsem).wait()

    def drain(stage, s):
        for _ in range(TOP_K):
            pltpu.make_async_copy(stage, xs_ref.at[pl.ds(0, ts * SUBLANES)], sem.at[s]).wait()

    def step(stage, s, other, so):
        @pl.when(i >= 2)
        def _():
            drain(stage, s)

        _rows_to_tiles(stage, h_ref[...], ts)

        def issue(j, carry):
            for u in range(ROWS_PER_ISSUE):
                tok = j * ROWS_PER_ISSUE + u
                for k in range(TOP_K):
                    row = dest_ref[(i * ts + tok) * TOP_K + k]
                    _row_copy(stage, tok, xs_ref, row, sem.at[s]).start(priority=k % 2)
            return carry

        lax.fori_loop(0, ts // ROWS_PER_ISSUE, issue, 0)

        @pl.when(i == n_tiles - 1)
        def _():
            drain(other, so)
            drain(stage, s)

    @pl.when(i % 2 == 0)
    def _():
        step(stage0, 0, stage1, 1)

    @pl.when(i % 2 == 1)
    def _():
        step(stage1, 1, stage0, 0)


def _dispatch_rows(h, dest, zero_blocks, n_rows):
    t, d = h.shape
    ts = DISPATCH_TILE
    tm = MOE_TILE
    n_tiles = t // ts
    assert d == SUBLANES * LANES and n_tiles >= 2
    grid_spec = pltpu.PrefetchScalarGridSpec(
        num_scalar_prefetch=2,
        grid=(n_tiles,),
        in_specs=[pl.BlockSpec((ts, d), lambda i, dest, zb: (i, 0))],
        out_specs=pl.BlockSpec(memory_space=pl.ANY),
        scratch_shapes=[pltpu.VMEM((ts * SUBLANES, LANES), F32), pltpu.VMEM((ts * SUBLANES, LANES), F32),
                        pltpu.VMEM((tm * SUBLANES, LANES), F32),
                        pltpu.SemaphoreType.DMA((2,)), pltpu.SemaphoreType.DMA(())],
    )
    return pl.pallas_call(
        functools.partial(_dispatch_kernel, ts=ts, n_tiles=n_tiles, tm=tm),
        out_shape=jax.ShapeDtypeStruct((n_rows * SUBLANES, LANES), F32),
        grid_spec=grid_spec,
        compiler_params=_params(1),
        name="moe_dispatch",
    )(dest, zero_blocks, h)


def _combine_kernel(dest_ref, h_ref, route_ref, g_ref, b_ref, y_ref, o_ref, gbuf0, gbuf1, sem, *, ts, n_tiles):
    i = pl.program_id(0)

    def issue(tile, gbuf, s):
        def body(j, carry):
            for u in range(ROWS_PER_ISSUE):
                tok = j * ROWS_PER_ISSUE + u
                for k in range(TOP_K):
                    row = dest_ref[(tile * ts + tok) * TOP_K + k]
                    _row_copy(y_ref, row, gbuf.at[k], tok, sem.at[s]).start(priority=k % 2)
            return carry

        lax.fori_loop(0, ts // ROWS_PER_ISSUE, body, 0)

    def drain(gbuf, s):
        for k in range(TOP_K):
            pltpu.make_async_copy(y_ref.at[pl.ds(0, ts * SUBLANES)], gbuf.at[k], sem.at[s]).wait()

    def step(gbuf, s, nxt, sn):
        @pl.when(i + 1 < n_tiles)
        def _():
            issue(i + 1, nxt, sn)

        drain(gbuf, s)
        parts = []
        for c in range(SUBLANES):
            acc = None
            for k in range(TOP_K):
                piece = gbuf[k, pl.ds(c, ts, stride=SUBLANES), :] * route_ref[:, TOP_K + k:TOP_K + k + 1]
                acc = piece if acc is None else acc + piece
            parts.append(acc)
        ff = jnp.concatenate(parts, axis=1)
        o_ref[...] = _layer_norm(DEEPNORM_ALPHA * h_ref[...] + ff, g_ref[...], b_ref[...])

    @pl.when(i == 0)
    def _():
        issue(0, gbuf0, 0)

    @pl.when(i % 2 == 0)
    def _():
        step(gbuf0, 0, gbuf1, 1)

    @pl.when(i % 2 == 1)
    def _():
        step(gbuf1, 1, gbuf0, 0)


def _combine_layer(h, y, dest, route, ln_g, ln_b):
    t, d = h.shape
    ts = COMBINE_TILE
    n_tiles = t // ts
    grid_spec = pltpu.PrefetchScalarGridSpec(
        num_scalar_prefetch=1,
        grid=(n_tiles,),
        in_specs=[pl.BlockSpec((ts, d), lambda i, dest: (i, 0)), pl.BlockSpec((ts, LANES), lambda i, dest: (i, 0)),
                  pl.BlockSpec((1, d), lambda i, dest: (0, 0)), pl.BlockSpec((1, d), lambda i, dest: (0, 0)),
                  pl.BlockSpec(memory_space=pl.ANY)],
        out_specs=pl.BlockSpec((ts, d), lambda i, dest: (i, 0)),
        scratch_shapes=[pltpu.VMEM((TOP_K, ts * SUBLANES, LANES), F32),
                        pltpu.VMEM((TOP_K, ts * SUBLANES, LANES), F32), pltpu.SemaphoreType.DMA((2,))],
    )
    return pl.pallas_call(
        functools.partial(_combine_kernel, ts=ts, n_tiles=n_tiles),
        out_shape=jax.ShapeDtypeStruct((t, d), F32),
        grid_spec=grid_spec,
        compiler_params=_params(1),
        name="moe_combine_ln",
    )(dest, h, route, ln_g.reshape(1, d), ln_b.reshape(1, d), y)


def _moe_layer(h2, route, cnt, layer, w_gate_up, b_gate_up, w_down, b_down, ln_g, ln_b):
    bsz, seq, d = h2.shape
    t = bsz * seq
    tm = MOE_TILE
    n_exp = w_gate_up.shape[1]
    h2 = h2.reshape(t, d)
    route = route.reshape(t, LANES)
    idx = route[:, 0:TOP_K].astype(jnp.int32)
    rank = route[:, 2 * TOP_K:3 * TOP_K].astype(jnp.int32)
    counts = cnt[0, :n_exp].astype(jnp.int32)
    padded = (counts + tm - 1) // tm * tm
    pend = jnp.cumsum(padded)
    pstart = pend - padded
    part = counts % tm
    skip = jnp.where(part > 0, tm - part, 0)

    def per_expert(table, e):
        return jnp.sum(jnp.where(e[..., None] == jnp.arange(n_exp, dtype=jnp.int32), table, 0), axis=-1)

    dest = (per_expert(pstart, idx) + rank
            + jnp.where(rank >= per_expert(part, idx), per_expert(skip, idx), 0)).reshape(-1)
    n_blocks = (t * TOP_K) // tm + n_exp
    n_used = (pend[-1] // tm).astype(jnp.int32)
    blk = jnp.arange(n_blocks, dtype=jnp.int32)
    block_e = jnp.sum((pend[None, :] <= (blk * tm)[:, None]).astype(jnp.int32), axis=1)
    block_e = jnp.minimum(block_e, n_exp - 1)
    block_e = jnp.where(blk < n_used, block_e, block_e[jnp.maximum(n_used - 1, 0)])
    first_blk = jnp.where(part > 0, pstart // tm, -1)
    tail_blk = n_used + jnp.arange(n_exp, dtype=jnp.int32)
    zero_blocks = jnp.concatenate([first_blk, jnp.where(tail_blk < n_blocks, tail_blk, -1)]).astype(jnp.int32)
    xs = _dispatch_rows(h2, dest, zero_blocks, n_blocks * tm)
    n_valid = jnp.where((blk == per_expert(first_blk, block_e)) & (blk < n_used), per_expert(part, block_e),
                        tm).astype(jnp.int32)
    y = _moe_experts(xs, block_e, n_used.reshape(1), n_valid, layer, w_gate_up, b_gate_up[:, 0::2], b_gate_up[:, 1::2],
                     w_down, b_down)
    return _combine_layer(h2, y, dest, route, ln_g, ln_b).reshape(bsz, seq, d)


def kernel(x, mem, pool_w_in, pool_w_grp, pool_scale, pool_w_out, ssm_w_in, ssm_conv_w, ssm_conv_b, ssm_dt_bias, ssm_a_log, ssm_d, ssm_norm_g, ssm_w_out, xa_wq, xa_wk, xa_wv, xa_wo, moe_w_router, moe_b_router, moe_w_gate_up, moe_b_gate_up, moe_w_down, moe_b_down, ln_mix_g, ln_mix_b, ln_xa_g, ln_xa_b, ln_ffn_g, ln_ffn_b):
    h = x
    for i in range(DEPTH):
        j = i // 2
        if i % 2 == 0:
            h = _pool_layer(h, pool_w_in[j], pool_w_grp[j], pool_scale[j], pool_w_out[j], ln_mix_g[i], ln_mix_b[i])
        else:
            h = _ssd_layer(h, ssm_w_in[j], ssm_conv_w[j], ssm_conv_b[j], ssm_dt_bias[j], ssm_a_log[j], ssm_d[j],
                           ssm_norm_g[j], ssm_w_out[j], ln_mix_g[i], ln_mix_b[i])
        h2, route, cnt = _xattn_layer(h, mem, xa_wq[i], xa_wk[i], xa_wv[i], xa_wo[i], ln_xa_g[i], ln_xa_b[i],
                                      moe_w_router[i], moe_b_router[i])
        h = _moe_layer(h2, route, cnt, i, moe_w_gate_up, moe_b_gate_up[i], moe_w_down, moe_b_down[i],
                       ln_ffn_g[i], ln_ffn_b[i])
    return h
```

```python
import functools

import jax
import jax.numpy as jnp
from jax import lax
from jax.experimental import pallas as pl
from jax.experimental.pallas import tpu as pltpu

F32 = jnp.float32
BF16 = jnp.bfloat16

DEPTH = 2
DEEPNORM_ALPHA = (2 * DEPTH) ** 0.25
LN_EPS = 1e-5
POOL_WINDOWS = (2, 4, 8, 16)
POOL_HALO = 16
SSM_HEAD_DIM = 64
SSM_N_GROUPS = 8
SSM_HEADS_PER_GROUP = 4
SSM_D_STATE = 128
SSM_CONV = 4
CONV_HALO = 8
SSM_NORM_EPS = 1e-5
XA_HEADS = 4
N_EXPERTS = 32
TOP_K = 4
SWIGLU_LIMIT = 7.0
SWIGLU_ALPHA = 1.702

LANES = 128
SUBLANES = 8
VMEM_LIMIT_BYTES = 56 * 1024 * 1024

POOL_TILE = 512
XATTN_TILE = 1024
SSD_TILE = 256
SSD_CHUNK = 128
MOE_TILE = 512
MOE_ROW_GROUP = 128
DISPATCH_TILE = 512
COMBINE_TILE = 512
ROWS_PER_ISSUE = 8

NEG_BIG = -1e30


def _layer_norm(v, g, b):
    mu = jnp.mean(v, -1, keepdims=True)
    d = v - mu
    var = jnp.mean(d * d, -1, keepdims=True)
    return d * lax.rsqrt(var + LN_EPS) * g + b


def _dot(a, b):
    return jnp.dot(a, b, preferred_element_type=F32)


def _dot_nt(a, b):
    return lax.dot_general(a, b, (((1,), (1,)), ((), ())), preferred_element_type=F32)


def _dot_tn(a, b):
    return lax.dot_general(a, b, (((0,), (0,)), ((), ())), preferred_element_type=F32)


def _const_spec(shape):
    nd = len(shape)
    return pl.BlockSpec(shape, lambda *_: (0,) * nd, pipeline_mode=pl.Buffered(1))


def _params(n_axes):
    return pltpu.CompilerParams(dimension_semantics=("arbitrary",) * n_axes,
                                vmem_limit_bytes=VMEM_LIMIT_BYTES)


def _pool_kernel(x_ref, win_ref, wgrp_ref, scale_ref, wout_ref, g_ref, b_ref, o_ref, ext_ref, *, ts, gd):
    s = pl.program_id(1)

    @pl.when(s == 0)
    def _():
        ext_ref[0:POOL_HALO, :] = jnp.zeros((POOL_HALO, ext_ref.shape[1]), F32)

    x = x_ref[...]
    ext_ref[POOL_HALO:, :] = _dot(x.astype(BF16), win_ref[...])
    pos = lax.broadcasted_iota(jnp.int32, (ts, 1), 0) + s * ts
    mix = None
    for g, w in enumerate(POOL_WINDOWS):
        cols = slice(g * gd, (g + 1) * gd)
        e = ext_ref[:, cols]
        acc = e
        sh = 1
        while sh < w:
            acc = acc + pltpu.roll(acc, sh, axis=0)
            sh *= 2
        cnt = jnp.minimum(pos + 1, w).astype(F32)
        m = acc[POOL_HALO:, :] / cnt - e[POOL_HALO:, :]
        yg = _dot(m.astype(BF16), wgrp_ref[g]) * scale_ref[:, cols]
        part = _dot(yg.astype(BF16), wout_ref[cols, :])
        mix = part if mix is None else mix + part
    ext_ref[0:POOL_HALO, :] = ext_ref[ts:ts + POOL_HALO, :]
    o_ref[...] = _layer_norm(DEEPNORM_ALPHA * x + mix, g_ref[...], b_ref[...])


def _pool_layer(h, w_in, w_grp, scale, w_out, ln_g, ln_b):
    bsz, seq, d = h.shape
    ts = POOL_TILE
    gd = d // len(POOL_WINDOWS)
    tile = pl.BlockSpec((None, ts, d), lambda b, s: (b, s, 0))
    return pl.pallas_call(
        functools.partial(_pool_kernel, ts=ts, gd=gd),
        out_shape=jax.ShapeDtypeStruct((bsz, seq, d), F32),
        grid=(bsz, seq // ts),
        in_specs=[tile, _const_spec((d, d)), _const_spec((len(POOL_WINDOWS), gd, gd)), _const_spec((1, d)),
                  _const_spec((d, d)), _const_spec((1, d)), _const_spec((1, d))],
        out_specs=tile,
        scratch_shapes=[pltpu.VMEM((POOL_HALO + ts, d), F32)],
        compiler_params=_params(2),
        name="pool_mixer_ln",
    )(h, w_in.astype(BF16), w_grp.astype(BF16), scale.reshape(1, d), w_out.astype(BF16),
      ln_g.reshape(1, d), ln_b.reshape(1, d))


def _sigmoid(v):
    return 1.0 / (1.0 + jnp.exp(-v))


def _ssd_kernel(x_ref, wz_ref, wx_ref, wb_ref, wc_ref, wdt_ref, convw_ref, convb_ref, dtb_ref, alog_ref,
                dexp_ref, normg_ref, expand_ref, wout_ref, g_ref, b_ref, o_ref,
                xbc_scr, z_scr, actx_scr, bmat_scr, cmat_scr, y_scr, state_scr, *, ts, q, d_inner, gn):
    s = pl.program_id(1)
    conv_dim = d_inner + 2 * gn
    gw = d_inner // SSM_N_GROUPS

    @pl.when(s == 0)
    def _():
        xbc_scr[:, 0:CONV_HALO, :] = jnp.zeros((conv_dim // LANES, CONV_HALO, LANES), F32)
        state_scr[...] = jnp.zeros(state_scr.shape, F32)

    x = x_ref[...]
    xb = x.astype(BF16)
    z_scr[...] = _dot(xb, wz_ref[...])
    strip = 512
    for w_ref, base in ((wx_ref, 0), (wb_ref, d_inner), (wc_ref, d_inner + gn)):
        for c in range(0, w_ref.shape[1], strip):
            res = _dot(xb, w_ref[:, c:c + strip])
            for j in range(strip // LANES):
                xbc_scr[(base + c) // LANES + j, CONV_HALO:, :] = res[:, j * LANES:(j + 1) * LANES]
    dt_raw = _dot(xb, wdt_ref[...]) + dtb_ref[...]
    dtv = jnp.maximum(dt_raw, 0.0) + jnp.log1p(jnp.exp(-jnp.abs(dt_raw)))
    a_all = dtv * (-jnp.exp(alog_ref[...]))

    for j in range(conv_dim // LANES):
        cs = slice(j * LANES, (j + 1) * LANES)
        acc = convb_ref[:, cs]
        for k in range(SSM_CONV):
            r0 = CONV_HALO - (SSM_CONV - 1) + k
            acc = acc + convw_ref[k:k + 1, cs] * xbc_scr[j, r0:r0 + ts, :]
        act = acc * _sigmoid(acc)
        c = j * LANES
        if c < d_inner:
            actx_scr[:, cs] = act
        elif c < d_inner + gn:
            bmat_scr[:, c - d_inner:c - d_inner + LANES] = act.astype(BF16)
        else:
            cmat_scr[:, c - d_inner - gn:c - d_inner - gn + LANES] = act.astype(BF16)
    xbc_scr[:, 0:CONV_HALO, :] = xbc_scr[:, ts:ts + CONV_HALO, :]

    expand = expand_ref[...]
    row_i = lax.broadcasted_iota(jnp.int32, (q, LANES), 0)
    causal = lax.broadcasted_iota(jnp.int32, (q, q), 0) >= lax.broadcasted_iota(jnp.int32, (q, q), 1)
    head_of_lane = lax.broadcasted_iota(jnp.int32, (q, gw), 1) // SSM_HEAD_DIM
    head_mask = [jnp.where(head_of_lane == r, 1.0, 0.0).astype(BF16) for r in range(SSM_HEADS_PER_GROUP)]

    for c in range(ts // q):
        rows = slice(c * q, (c + 1) * q)
        acs = a_all[rows, :]
        sh = 1
        while sh < q:
            acs = acs + jnp.where(row_i >= sh, pltpu.roll(acs, sh, axis=0), 0.0)
            sh *= 2
        acs_t = acs.T
        a_last = acs[q - 1:q, :]
        e_in = _dot(jnp.exp(acs).astype(BF16), expand)
        dec = _dot(jnp.exp(a_last - acs).astype(BF16), expand)
        dtx = _dot(dtv[rows, :].astype(BF16), expand)
        cd = jnp.broadcast_to(jnp.exp(a_last), (8, LANES))
        cd_hi = cd.astype(BF16)
        cd_lo = (cd - cd_hi.astype(F32)).astype(BF16)
        cdx = (_dot(cd_hi, expand) + _dot(cd_lo, expand))[0:1, :]
        xd = actx_scr[rows, :] * dtx
        xdd = xd * dec
        xdb = xd.astype(BF16)
        for g in range(SSM_N_GROUPS):
            gc = slice(g * gw, (g + 1) * gw)
            nc = slice(g * SSM_D_STATE, (g + 1) * SSM_D_STATE)
            bg = bmat_scr[rows, nc]
            cg = cmat_scr[rows, nc]
            cb = _dot_nt(cg, bg)
            xg = xdb[:, gc]
            mhs, xms = [], []
            for r in range(SSM_HEADS_PER_GROUP):
                hd = g * SSM_HEADS_PER_GROUP + r
                seg = acs[:, hd:hd + 1] - acs_t[hd:hd + 1, :]
                lmat = jnp.exp(jnp.where(causal, seg, NEG_BIG))
                mhs.append((cb * lmat).astype(BF16))
                xms.append(xg * head_mask[r])
            yg = (_dot(cg, state_scr[g].astype(BF16)) * e_in[:, gc]
                  + _dot(jnp.concatenate(mhs, axis=1), jnp.concatenate(xms, axis=0)))
            y_scr[rows, gc] = yg
            state_scr[g] = state_scr[g] * cdx[:, gc] + _dot_tn(bg, xdd[:, gc].astype(BF16))

    mix = None
    for g in range(SSM_N_GROUPS):
        gc = slice(g * gw, (g + 1) * gw)
        zz = z_scr[:, gc]
        yv = (y_scr[:, gc] + dexp_ref[:, gc] * actx_scr[:, gc]) * (zz * _sigmoid(zz))
        yv = yv * lax.rsqrt(jnp.mean(yv * yv, -1, keepdims=True) + SSM_NORM_EPS) * normg_ref[:, gc]
        part = _dot(yv.astype(BF16), wout_ref[gc, :])
        mix = part if mix is None else mix + part
    o_ref[...] = _layer_norm(DEEPNORM_ALPHA * x + mix, g_ref[...], b_ref[...])


def _ssd_layer(h, w_in, conv_w, conv_b, dt_bias, a_log, d_skip, norm_g, w_out, ln_g, ln_b):
    bsz, seq, d = h.shape
    n_heads = a_log.shape[0]
    d_inner = n_heads * SSM_HEAD_DIM
    gn = SSM_N_GROUPS * SSM_D_STATE
    conv_dim = d_inner + 2 * gn
    ts, q = SSD_TILE, SSD_CHUNK
    w_in = w_in.astype(BF16)
    wz = w_in[:, :d_inner]
    wx = w_in[:, d_inner:2 * d_inner]
    wb = w_in[:, 2 * d_inner:2 * d_inner + gn]
    wc = w_in[:, 2 * d_inner + gn:2 * d_inner + 2 * gn]
    pad = LANES - n_heads
    wdt = jnp.pad(w_in[:, d_inner + conv_dim:], ((0, 0), (0, pad)))
    dtb = jnp.pad(dt_bias.astype(F32), (0, pad)).reshape(1, LANES)
    alog = jnp.pad(a_log.astype(F32), (0, pad)).reshape(1, LANES)
    dexp = jnp.repeat(d_skip.astype(F32), SSM_HEAD_DIM).reshape(1, d_inner)
    expand = (jnp.arange(LANES)[:, None] == (jnp.arange(d_inner)[None, :] // SSM_HEAD_DIM)).astype(BF16)
    tile = pl.BlockSpec((None, ts, d), lambda b, s: (b, s, 0))
    return pl.pallas_call(
        functools.partial(_ssd_kernel, ts=ts, q=q, d_inner=d_inner, gn=gn),
        out_shape=jax.ShapeDtypeStruct((bsz, seq, d), F32),
        grid=(bsz, seq // ts),
        in_specs=[tile, _const_spec((d, d_inner)), _const_spec((d, d_inner)), _const_spec((d, gn)),
                  _const_spec((d, gn)), _const_spec((d, LANES)), _const_spec((SSM_CONV, conv_dim)),
                  _const_spec((1, conv_dim)), _const_spec((1, LANES)), _const_spec((1, LANES)),
                  _const_spec((1, d_inner)), _const_spec((1, d_inner)), _const_spec((LANES, d_inner)),
                  _const_spec((d_inner, d)), _const_spec((1, d)), _const_spec((1, d))],
        out_specs=tile,
        scratch_shapes=[pltpu.VMEM((conv_dim // LANES, CONV_HALO + ts, LANES), F32), pltpu.VMEM((ts, d_inner), F32),
                        pltpu.VMEM((ts, d_inner), F32), pltpu.VMEM((ts, gn), BF16), pltpu.VMEM((ts, gn), BF16),
                        pltpu.VMEM((ts, d_inner), F32),
                        pltpu.VMEM((SSM_N_GROUPS, SSM_D_STATE, d_inner // SSM_N_GROUPS), F32)],
        compiler_params=_params(2),
        name="ssd_mixer_ln",
    )(h, wz, wx, wb, wc, wdt, conv_w.astype(F32), conv_b.reshape(1, conv_dim), dtb, alog, dexp,
      norm_g.reshape(1, d_inner), expand, w_out.astype(BF16), ln_g.reshape(1, d), ln_b.reshape(1, d))


def _xattn_kernel(h_ref, mem_ref, wq_ref, wk_ref, wv_ref, wo_ref, g_ref, b_ref, wrh_ref, wrl_ref, br_ref,
                  o_ref, route_ref, cnt_ref, k_scr, v_scr, carry_scr, *, ts, hd):
    b = pl.program_id(0)
    s = pl.program_id(1)

    @pl.when(s == 0)
    def _():
        mb = mem_ref[...].astype(BF16)
        k_scr[...] = _dot(mb, wk_ref[...]).astype(BF16)
        v_scr[...] = _dot(mb, wv_ref[...]).astype(BF16)

    @pl.when((b == 0) & (s == 0))
    def _():
        carry_scr[...] = jnp.zeros(carry_scr.shape, F32)

    h = h_ref[...]
    qv = (_dot(h.astype(BF16), wq_ref[...]) * (hd ** -0.5)).astype(BF16)
    heads = []
    for hh in range(XA_HEADS):
        cols = slice(hh * hd, (hh + 1) * hd)
        sc = _dot_nt(qv[:, cols], k_scr[:, cols])
        p = jnp.exp(sc - jnp.max(sc, -1, keepdims=True))
        o = _dot(p.astype(BF16), v_scr[:, cols]) / jnp.sum(p, -1, keepdims=True)
        heads.append(o.astype(BF16))
    xa = _dot(jnp.concatenate(heads, axis=1), wo_ref[...])
    h2 =_layer_norm(DEEPNORM_ALPHA * h + xa, g_ref[...], b_ref[...])
    o_ref[...] = h2
    h2_hi = h2.astype(BF16)

    h2_lo = (h2 - h2_hi.astype(F32)).astype(BF16)
    logits = _dot(h2_hi, wrh_ref[...]) + _dot(h2_lo, wrh_ref[...]) + _dot(h2_hi, wrl_ref[...]) + br_ref[...]
    lane = lax.broadcasted_iota(jnp.int32, (ts, LANES), 1).astype(F32)
    work = logits
    vals, idxs, sels = [], [], []
    for _ in range(TOP_K):
        m = jnp.max(work, -1, keepdims=True)
        ik = jnp.min(jnp.where(work == m, lane, float(LANES)), -1, keepdims=True)
        sel = lane == ik
        vals.append(m)
        idxs.append(ik)
        sels.append(sel)
        work = jnp.where(sel, -jnp.inf, work)
    exps = [jnp.exp(v - vals[0]) for v in vals]
    den = exps[0]
    for e in exps[1:]:
        den = den + e
    onehot = jnp.zeros((ts, LANES), F32)
    for sel in sels:
        onehot = onehot + sel.astype(F32)
    below = (lax.broadcasted_iota(jnp.int32, (ts, ts), 0) > lax.broadcasted_iota(jnp.int32, (ts, ts), 1))
    before = _dot(jnp.where(below, 1.0, 0.0).astype(BF16), onehot.astype(BF16)) + carry_scr[0:1, :]
    route = jnp.zeros((ts, LANES), F32)
    for k in range(TOP_K):
        rank = jnp.sum(jnp.where(sels[k], before, 0.0), -1, keepdims=True)
        route = jnp.where(lane == float(k), idxs[k], route)
        route = jnp.where(lane == float(TOP_K + k), exps[k] / den, route)
        route = jnp.where(lane == float(2 * TOP_K + k), rank, route)
    route_ref[...] = route
    carry_scr[...] = carry_scr[...] + jnp.sum(onehot, 0, keepdims=True)
    cnt_ref[...] = carry_scr[...]


def _xattn_layer(h, mem, wq, wk, wv, wo, ln_g, ln_b, w_router, b_router):
    bsz, seq, d = h.shape
    mlen = mem.shape[1]
    ts = XATTN_TILE
    hd = d // XA_HEADS
    n_exp = w_router.shape[1]
    wr = jnp.pad(w_router.astype(F32), ((0, 0), (0, LANES - n_exp)))
    wr_hi = wr.astype(BF16)
    wr_lo = (wr - wr_hi.astype(F32)).astype(BF16)
    br = jnp.pad(b_router.astype(F32), (0, LANES - n_exp), constant_values=-jnp.inf).reshape(1, LANES)
    tile = pl.BlockSpec((None, ts, d), lambda b, s: (b, s, 0))
    rtile = pl.BlockSpec((None, ts, LANES), lambda b, s: (b, s, 0))
    return pl.pallas_call(
        functools.partial(_xattn_kernel, ts=ts, hd=hd),
        out_shape=(jax.ShapeDtypeStruct((bsz, seq, d), F32),
                   jax.ShapeDtypeStruct((bsz, seq, LANES), F32), jax.ShapeDtypeStruct((8, LANES), F32)),
        grid=(bsz, seq // ts),
        in_specs=[tile, pl.BlockSpec((None, mlen, d), lambda b, s: (b, 0, 0)),
                  _const_spec((d, d)), _const_spec((d, d)), _const_spec((d, d)), _const_spec((d, d)),
                  _const_spec((1, d)), _const_spec((1, d)), _const_spec((d, LANES)), _const_spec((d, LANES)),
                  _const_spec((1, LANES))],
        out_specs=(tile, rtile, pl.BlockSpec((8, LANES), lambda b, s: (0, 0))),
        scratch_shapes=[pltpu.VMEM((mlen, d), BF16), pltpu.VMEM((mlen, d), BF16), pltpu.VMEM((8, LANES), F32)],
        compiler_params=_params(2),
        name="xattn_ln_router",
    )(h, mem, wq.astype(BF16), wk.astype(BF16), wv.astype(BF16), wo.astype(BF16),
      ln_g.reshape(1, d), ln_b.reshape(1, d), wr_hi, wr_lo, br)


def _rows_from_tiles(ref, n):
    return jnp.concatenate([ref[pl.ds(c, n, stride=SUBLANES), :] for c in range(SUBLANES)], axis=1)


def _rows_to_tiles(ref, v, n):
    for c in range(SUBLANES):
        ref[pl.ds(c, n, stride=SUBLANES), :] = v[:, c * LANES:(c + 1) * LANES]


def _moe_kernel(be_ref, nu_ref, nv_ref, x_ref, wgu_ref, bg_ref, bu_ref, wd_ref, bd_ref, o_ref, wg_scr, wu_scr, wd_scr,
                *, tm):
    i = pl.program_id(0)
    active = i < nu_ref[0]
    new_expert = (i == 0) | (be_ref[i] != be_ref[jnp.maximum(i - 1, 0)])

    @pl.when(active & new_expert)
    def _():
        w2 = 2 * LANES
        src = lax.broadcasted_iota(jnp.int32, (w2, w2), 0)
        dst = lax.broadcasted_iota(jnp.int32, (w2, w2), 1)
        perm = jnp.where(src == jnp.where(dst < LANES, 2 * dst, 2 * (dst - LANES) + 1), 1.0, 0.0).astype(BF16)
        for c in range(wgu_ref.shape[1] // w2):
            res = _dot(wgu_ref[:, c * w2:(c + 1) * w2].astype(BF16), perm)
            wg_scr[:, c * LANES:(c + 1) * LANES] = res[:, :LANES].astype(BF16)
            wu_scr[:, c * LANES:(c + 1) * LANES] = res[:, LANES:].astype(BF16)
        wd_scr[...] = wd_ref[...].astype(BF16)

    nv = nv_ref[i]
    for r in range(MOE_ROW_GROUP, tm + 1, MOE_ROW_GROUP):
        @pl.when(active & (nv > r - MOE_ROW_GROUP) & (nv <= r))
        def _():
            x = _rows_from_tiles(x_ref, r).astype(BF16)
            gate = jnp.minimum(_dot(x, wg_scr[...]) + bg_ref[...], SWIGLU_LIMIT)
            up = jnp.clip(_dot(x, wu_scr[...]) + bu_ref[...], -SWIGLU_LIMIT, SWIGLU_LIMIT)
            act = (up + 1.0) * (gate * _sigmoid(SWIGLU_ALPHA * gate))
            _rows_to_tiles(o_ref, _dot(act.astype(BF16), wd_scr[...]) + bd_ref[...], r)
            if r < tm:
                o_ref[r * SUBLANES:, :] = jnp.zeros(((tm - r) * SUBLANES, LANES), F32)

    @pl.when(jnp.logical_not(active))
    def _():
        o_ref[...] = jnp.zeros(o_ref.shape, F32)


def _moe_experts(xs, block_e, n_used, n_valid, layer, w_gate_up, b_gate, b_up, w_down, b_down):
    _, n_exp, f, d = w_down.shape
    n_rows = xs.shape[0] // SUBLANES
    tm = MOE_TILE
    n_blocks = n_rows // tm

    def row_map(i, be, nu, nv):
        return (jnp.minimum(i, nu[0] - 1), 0)

    def w_map(i, be, nu, nv):
        return (be[i], 0, 0)

    def lw_map(i, be, nu, nv):
        return (layer, be[i], 0, 0)

    grid_spec = pltpu.PrefetchScalarGridSpec(
        num_scalar_prefetch=3,
        grid=(n_blocks,),
        in_specs=[pl.BlockSpec((tm * SUBLANES, LANES), row_map),
                  pl.BlockSpec((None, None, d, 2 * f), lw_map),
                  pl.BlockSpec((None, 1, f), w_map), pl.BlockSpec((None, 1, f), w_map),
                  pl.BlockSpec((None, None, f, d), lw_map), pl.BlockSpec((None, 1, d), w_map)],
        out_specs=pl.BlockSpec((tm * SUBLANES, LANES), lambda i, be, nu, nv: (i, 0)),
        scratch_shapes=[pltpu.VMEM((d, f), BF16), pltpu.VMEM((d, f), BF16), pltpu.VMEM((f, d), BF16)],
    )
    return pl.pallas_call(
        functools.partial(_moe_kernel, tm=tm),
        out_shape=jax.ShapeDtypeStruct((n_rows * SUBLANES, LANES), F32),
        grid_spec=grid_spec,
        compiler_params=_params(1),
        name="moe_experts",
    )(block_e, n_used, n_valid, xs, w_gate_up, b_gate.reshape(n_exp, 1, f), b_up.reshape(n_exp, 1, f),
      w_down, b_down.reshape(n_exp, 1, d))


def _row_copy(src, src_row, dst, dst_row, sem):
    return pltpu.make_async_copy(src.at[pl.ds(pl.multiple_of(src_row * SUBLANES, SUBLANES), SUBLANES)],
                                 dst.at[pl.ds(pl.multiple_of(dst_row * SUBLANES, SUBLANES), SUBLANES)], sem)


def _dispatch_kernel(dest_ref, zblk_ref, h_ref, xs_ref, stage0, stage1, zero_scr, sem, zsem, *, ts, n_tiles, tm):
    i = pl.program_id(0)

    @pl.when(i == 0)
    def _():
        zero_scr[...] = jnp.zeros(zero_scr.shape, F32)
        for j in range(zblk_ref.shape[0]):
            @pl.when(zblk_ref[j] >= 0)
            def _():
                start = pl.multiple_of(zblk_ref[j] * (tm * SUBLANES), tm * SUBLANES)
                pltpu.make_async_copy(zero_scr, xs_ref.at[pl.ds(start, tm * SUBLANES)], zsem).start()
        for j in range(zblk_ref.shape[0]):
            @pl.when(zblk_ref[j] >= 0)
            def _():
                pltpu.make_async_copy(zero_scr, xs_ref.at[pl.ds(0, tm * SUBLANES)], zsem).wait()

    def drain(stage, s):
        for _ in range(TOP_K):
            pltpu.make_async_copy(stage, xs_ref.at[pl.ds(0, ts * SUBLANES)], sem.at[s]).wait()

    def step(stage, s, other, so):
        @pl.when(i >= 2)
        def _():
            drain(stage, s)

        _rows_to_tiles(stage, h_ref[...], ts)

        def issue(j, carry):
            for u in range(ROWS_PER_ISSUE):
                tok = j * ROWS_PER_ISSUE + u
                for k in range(TOP_K):
                    row = dest_ref[(i * ts + tok) * TOP_K + k]
                    _row_copy(stage, tok, xs_ref, row, sem.at[s]).start(priority=k % 2)
            return carry

        lax.fori_loop(0, ts // ROWS_PER_ISSUE, issue, 0)

        @pl.when(i == n_tiles - 1)
        def _():
            drain(other, so)
            drain(stage, s)

    @pl.when(i % 2 == 0)
    def _():
        step(stage0, 0, stage1, 1)

    @pl.when(i % 2 == 1)
    def _():
        step(stage1, 1, stage0, 0)


def _dispatch_rows(h, dest, zero_blocks, n_rows):
    t, d = h.shape
    ts = DISPATCH_TILE
    tm = MOE_TILE
    n_tiles = t // ts
    assert d == SUBLANES * LANES and n_tiles >= 2
    grid_spec = pltpu.PrefetchScalarGridSpec(
        num_scalar_prefetch=2,
        grid=(n_tiles,),
        in_specs=[pl.BlockSpec((ts, d), lambda i, dest, zb: (i, 0))],
        out_specs=pl.BlockSpec(memory_space=pl.ANY),
        scratch_shapes=[pltpu.VMEM((ts * SUBLANES, LANES), F32), pltpu.VMEM((ts * SUBLANES, LANES), F32),
                        pltpu.VMEM((tm * SUBLANES, LANES), F32),
                        pltpu.SemaphoreType.DMA((2,)), pltpu.SemaphoreType.DMA(())],
    )
    return pl.pallas_call(
        functools.partial(_dispatch_kernel, ts=ts, n_tiles=n_tiles, tm=tm),
        out_shape=jax.ShapeDtypeStruct((n_rows * SUBLANES, LANES), F32),
        grid_spec=grid_spec,
        compiler_params=_params(1),
        name="moe_dispatch",
    )(dest, zero_blocks, h)


def _combine_kernel(dest_ref, h_ref, route_ref, g_ref, b_ref, y_ref, o_ref, gbuf0, gbuf1, sem, *, ts, n_tiles):
    i = pl.program_id(0)

    def issue(tile, gbuf, s):
        def body(j, carry):
            for u in range(ROWS_PER_ISSUE):
                tok = j * ROWS_PER_ISSUE + u
                for k in range(TOP_K):
                    row = dest_ref[(tile * ts + tok) * TOP_K + k]
                    _row_copy(y_ref, row, gbuf.at[k], tok, sem.at[s]).start(priority=k % 2)
            return carry

        lax.fori_loop(0, ts // ROWS_PER_ISSUE, body, 0)

    def drain(gbuf, s):
        for k in range(TOP_K):
            pltpu.make_async_copy(y_ref.at[pl.ds(0, ts * SUBLANES)], gbuf.at[k], sem.at[s]).wait()

    def step(gbuf, s, nxt, sn):
        @pl.when(i + 1 < n_tiles)
        def _():
            issue(i + 1, nxt, sn)

        drain(gbuf, s)
        parts = []
        for c in range(SUBLANES):
            acc = None
            for k in range(TOP_K):
                piece = gbuf[k, pl.ds(c, ts, stride=SUBLANES), :] * route_ref[:, TOP_K + k:TOP_K + k + 1]
                acc = piece if acc is None else acc + piece
            parts.append(acc)
        ff = jnp.concatenate(parts, axis=1)
        o_ref[...] = _layer_norm(DEEPNORM_ALPHA * h_ref[...] + ff, g_ref[...], b_ref[...])

    @pl.when(i == 0)
    def _():
        issue(0, gbuf0, 0)

    @pl.when(i % 2 == 0)
    def _():
        step(gbuf0, 0, gbuf1, 1)

    @pl.when(i % 2 == 1)
    def _():
        step(gbuf1, 1, gbuf0, 0)


def _combine_layer(h, y, dest, route, ln_g, ln_b):
    t, d = h.shape
    ts = COMBINE_TILE
    n_tiles = t // ts
    grid_spec = pltpu.PrefetchScalarGridSpec(
        num_scalar_prefetch=1,
        grid=(n_tiles,),
        in_specs=[pl.BlockSpec((ts, d), lambda i, dest: (i, 0)), pl.BlockSpec((ts, LANES), lambda i, dest: (i, 0)),
                  pl.BlockSpec((1, d), lambda i, dest: (0, 0)), pl.BlockSpec((1, d), lambda i, dest: (0, 0)),
                  pl.BlockSpec(memory_space=pl.ANY)],
        out_specs=pl.BlockSpec((ts, d), lambda i, dest: (i, 0)),
        scratch_shapes=[pltpu.VMEM((TOP_K, ts * SUBLANES, LANES), F32),
                        pltpu.VMEM((TOP_K, ts * SUBLANES, LANES), F32), pltpu.SemaphoreType.DMA((2,))],
    )
    return pl.pallas_call(
        functools.partial(_combine_kernel, ts=ts, n_tiles=n_tiles),
        out_shape=jax.ShapeDtypeStruct((t, d), F32),
        grid_spec=grid_spec,
        compiler_params=_params(1),
        name="moe_combine_ln",
    )(dest, h, route, ln_g.reshape(1, d), ln_b.reshape(1, d), y)


def _moe_layer(h2, route, cnt, layer, w_gate_up, b_gate_up, w_down, b_down, ln_g, ln_b):
    bsz, seq, d = h2.shape
    t = bsz * seq
    tm = MOE_TILE
    n_exp = w_gate_up.shape[1]
    h2 = h2.reshape(t, d)
    route = route.reshape(t, LANES)
    idx = route[:, 0:TOP_K].astype(jnp.int32)
    rank = route[:, 2 * TOP_K:3 * TOP_K].astype(jnp.int32)
    counts = cnt[0, :n_exp].astype(jnp.int32)
    padded = (counts + tm - 1) // tm * tm
    pend = jnp.cumsum(padded)
    pstart = pend - padded
    part = counts % tm
    skip = jnp.where(part > 0, tm - part, 0)

    def per_expert(table, e):
        return jnp.sum(jnp.where(e[..., None] == jnp.arange(n_exp, dtype=jnp.int32), table, 0), axis=-1)

    dest = (per_expert(pstart, idx) + rank
            + jnp.where(rank >= per_expert(part, idx), per_expert(skip, idx), 0)).reshape(-1)
    n_blocks = (t * TOP_K) // tm + n_exp
    n_used = (pend[-1] // tm).astype(jnp.int32)
    blk = jnp.arange(n_blocks, dtype=jnp.int32)
    block_e = jnp.sum((pend[None, :] <= (blk * tm)[:, None]).astype(jnp.int32), axis=1)
    block_e = jnp.minimum(block_e, n_exp - 1)
    block_e = jnp.where(blk < n_used, block_e, block_e[jnp.maximum(n_used - 1, 0)])
    first_blk = jnp.where(part > 0, pstart // tm, -1)
    tail_blk = n_used + jnp.arange(n_exp, dtype=jnp.int32)
    zero_blocks = jnp.concatenate([first_blk, jnp.where(tail_blk < n_blocks, tail_blk, -1)]).astype(jnp.int32)
    xs = _dispatch_rows(h2, dest, zero_blocks, n_blocks * tm)
    n_valid = jnp.where((blk == per_expert(first_blk, block_e)) & (blk < n_used), per_expert(part, block_e),
                        tm).astype(jnp.int32)
    y = _moe_experts(xs, block_e, n_used.reshape(1), n_valid, layer, w_gate_up, b_gate_up[:, 0::2], b_gate_up[:, 1::2],
                     w_down, b_down)
    return _combine_layer(h2, y, dest, route, ln_g, ln_b).reshape(bsz, seq, d)


def kernel(x, mem, pool_w_in, pool_w_grp, pool_scale, pool_w_out, ssm_w_in, ssm_conv_w, ssm_conv_b, ssm_dt_bias, ssm_a_log, ssm_d, ssm_norm_g, ssm_w_out, xa_wq, xa_wk, xa_wv, xa_wo, moe_w_router, moe_b_router, moe_w_gate_up, moe_b_gate_up, moe_w_down, moe_b_down, ln_mix_g, ln_mix_b, ln_xa_g, ln_xa_b, ln_ffn_g, ln_ffn_b):
    h = x
    for i in range(DEPTH):
        j = i // 2
        if i % 2 == 0:
            h = _pool_layer(h, pool_w_in[j], pool_w_grp[j], pool_scale[j], pool_w_out[j], ln_mix_g[i], ln_mix_b[i])
        else:
            h = _ssd_layer(h, ssm_w_in[j], ssm_conv_w[j], ssm_conv_b[j], ssm_dt_bias[j], ssm_a_log[j], ssm_d[j],
                           ssm_norm_g[j], ssm_w_out[j], ln_mix_g[i], ln_mix_b[i])
        h2, route, cnt = _xattn_layer(h, mem, xa_wq[i], xa_wk[i], xa_wv[i], xa_wo[i], ln_xa_g[i], ln_xa_b[i],
                                      moe_w_router[i], moe_b_router[i])
        h = _moe_layer(h2, route, cnt, i, moe_w_gate_up, moe_b_gate_up[i], moe_w_down, moe_b_down[i],
                       ln_ffn_g[i], ln_ffn_b[i])
    return h
```

```python
import functools

import jax
import jax.numpy as jnp
from jax import lax
from jax.experimental import pallas as pl
from jax.experimental.pallas import tpu as pltpu

F32 = jnp.float32
BF16 = jnp.bfloat16

DEPTH = 2
DEEPNORM_ALPHA = (2 * DEPTH) ** 0.25
LN_EPS = 1e-5
POOL_WINDOWS = (2, 4, 8, 16)
POOL_HALO = 16
SSM_HEAD_DIM = 64
SSM_N_GROUPS = 8
SSM_HEADS_PER_GROUP = 4
SSM_D_STATE = 128
SSM_CONV = 4
CONV_HALO = 8
SSM_NORM_EPS = 1e-5
XA_HEADS = 4
N_EXPERTS = 32
TOP_K = 4
SWIGLU_LIMIT = 7.0
SWIGLU_ALPHA = 1.702

LANES = 128
SUBLANES = 8
VMEM_LIMIT_BYTES = 56 * 1024 * 1024

POOL_TILE = 512
XATTN_TILE = 1024
SSD_TILE = 256
SSD_CHUNK = 128
MOE_TILE = 512
MOE_ROW_GROUP = 128
DISPATCH_TILE = 512
COMBINE_TILE = 512
ROWS_PER_ISSUE = 8

ROUTE_ROWS = 16

NEG_BIG = -1e30


def _layer_norm(v, g, b):
    mu = jnp.mean(v, -1, keepdims=True)
    d = v - mu
    var = jnp.mean(d * d, -1, keepdims=True)
    return d * lax.rsqrt(var + LN_EPS) * g + b


def _dot(a, b):
    return jnp.dot(a, b, preferred_element_type=F32)


def _dot_nt(a, b):
    return lax.dot_general(a, b, (((1,), (1,)), ((), ())), preferred_element_type=F32)


def _dot_tn(a, b):
    return lax.dot_general(a, b, (((0,), (0,)), ((), ())), preferred_element_type=F32)


def _const_spec(shape):
    nd = len(shape)
    return pl.BlockSpec(shape, lambda *_: (0,) * nd, pipeline_mode=pl.Buffered(1))


def _params(n_axes):
    return pltpu.CompilerParams(dimension_semantics=("arbitrary",) * n_axes,
                                vmem_limit_bytes=VMEM_LIMIT_BYTES)


def _pool_kernel(x_ref, win_ref, wgrp_ref, scale_ref, wout_ref, g_ref, b_ref, o_ref, ext_ref, *, ts, gd):
    s = pl.program_id(1)

    @pl.when(s == 0)
    def _():
        ext_ref[0:POOL_HALO, :] = jnp.zeros((POOL_HALO, ext_ref.shape[1]), F32)

    x = x_ref[...]
    ext_ref[POOL_HALO:, :] = _dot(x.astype(BF16), win_ref[...])
    pos = lax.broadcasted_iota(jnp.int32, (ts, 1), 0) + s * ts
    mix = None
    for g, w in enumerate(POOL_WINDOWS):
        cols = slice(g * gd, (g + 1) * gd)
        e = ext_ref[:, cols]
        acc = e
        sh = 1
        while sh < w:
            acc = acc + pltpu.roll(acc, sh, axis=0)
            sh *= 2
        cnt = jnp.minimum(pos + 1, w).astype(F32)
        m = acc[POOL_HALO:, :] / cnt - e[POOL_HALO:, :]
        yg = _dot(m.astype(BF16), wgrp_ref[g]) * scale_ref[:, cols]
        part = _dot(yg.astype(BF16), wout_ref[cols, :])
        mix = part if mix is None else mix + part
    ext_ref[0:POOL_HALO, :] = ext_ref[ts:ts + POOL_HALO, :]
    o_ref[...] = _layer_norm(DEEPNORM_ALPHA * x + mix, g_ref[...], b_ref[...])


def _pool_layer(h, w_in, w_grp, scale, w_out, ln_g, ln_b):
    bsz, seq, d = h.shape
    ts = POOL_TILE
    gd = d // len(POOL_WINDOWS)
    tile = pl.BlockSpec((None, ts, d), lambda b, s: (b, s, 0))
    return pl.pallas_call(
        functools.partial(_pool_kernel, ts=ts, gd=gd),
        out_shape=jax.ShapeDtypeStruct((bsz, seq, d), F32),
        grid=(bsz, seq // ts),
        in_specs=[tile, _const_spec((d, d)), _const_spec((len(POOL_WINDOWS), gd, gd)), _const_spec((1, d)),
                  _const_spec((d, d)), _const_spec((1, d)), _const_spec((1, d))],
        out_specs=tile,
        scratch_shapes=[pltpu.VMEM((POOL_HALO + ts, d), F32)],
        compiler_params=_params(2),
        name="pool_mixer_ln",
    )(h, w_in.astype(BF16), w_grp.astype(BF16), scale.reshape(1, d), w_out.astype(BF16),
      ln_g.reshape(1, d), ln_b.reshape(1, d))


def _sigmoid(v):
    return 1.0 / (1.0 + jnp.exp(-v))


def _ssd_kernel(x_ref, wz_ref, wx_ref, wb_ref, wc_ref, wdt_ref, convw_ref, convb_ref, dtb_ref, alog_ref,
                dexp_ref, normg_ref, expand_ref, wout_ref, g_ref, b_ref, o_ref,
                xbc_scr, z_scr, actx_scr, bmat_scr, cmat_scr, y_scr, state_scr, *, ts, q, d_inner, gn):
    s = pl.program_id(1)
    conv_dim = d_inner + 2 * gn
    gw = d_inner // SSM_N_GROUPS

    @pl.when(s == 0)
    def _():
        xbc_scr[:, 0:CONV_HALO, :] = jnp.zeros((conv_dim // LANES, CONV_HALO, LANES), F32)
        state_scr[...] = jnp.zeros(state_scr.shape, F32)

    x = x_ref[...]
    xb = x.astype(BF16)
    z_scr[...] = _dot(xb, wz_ref[...])
    strip = 512
    for w_ref, base in ((wx_ref, 0), (wb_ref, d_inner), (wc_ref, d_inner + gn)):
        for c in range(0, w_ref.shape[1], strip):
            res = _dot(xb, w_ref[:, c:c + strip])
            for j in range(strip // LANES):
                xbc_scr[(base + c) // LANES + j, CONV_HALO:, :] = res[:, j * LANES:(j + 1) * LANES]
    dt_raw = _dot(xb, wdt_ref[...]) + dtb_ref[...]
    dtv = jnp.maximum(dt_raw, 0.0) + jnp.log1p(jnp.exp(-jnp.abs(dt_raw)))
    a_all = dtv * (-jnp.exp(alog_ref[...]))

    for j in range(conv_dim // LANES):
        cs = slice(j * LANES, (j + 1) * LANES)
        acc = convb_ref[:, cs]
        for k in range(SSM_CONV):
            r0 = CONV_HALO - (SSM_CONV - 1) + k
            acc = acc + convw_ref[k:k + 1, cs] * xbc_scr[j, r0:r0 + ts, :]
        act = acc * _sigmoid(acc)
        c = j * LANES
        if c < d_inner:
            actx_scr[:, cs] = act
        elif c < d_inner + gn:
            bmat_scr[:, c - d_inner:c - d_inner + LANES] = act.astype(BF16)
        else:
            cmat_scr[:, c - d_inner - gn:c - d_inner - gn + LANES] = act.astype(BF16)
    xbc_scr[:, 0:CONV_HALO, :] = xbc_scr[:, ts:ts + CONV_HALO, :]

    expand = expand_ref[...]
    row_i = lax.broadcasted_iota(jnp.int32, (q, LANES), 0)
    causal = lax.broadcasted_iota(jnp.int32, (q, q), 0) >= lax.broadcasted_iota(jnp.int32, (q, q), 1)
    head_of_lane = lax.broadcasted_iota(jnp.int32, (q, gw), 1) // SSM_HEAD_DIM
    head_mask = [jnp.where(head_of_lane == r, 1.0, 0.0).astype(BF16) for r in range(SSM_HEADS_PER_GROUP)]

    for c in range(ts // q):
        rows = slice(c * q, (c + 1) * q)
        acs = a_all[rows, :]
        sh = 1
        while sh < q:
            acs = acs + jnp.where(row_i >= sh, pltpu.roll(acs, sh, axis=0), 0.0)
            sh *= 2
        acs_t = acs.T
        a_last = acs[q - 1:q, :]
        e_in = _dot(jnp.exp(acs).astype(BF16), expand)
        dec = _dot(jnp.exp(a_last - acs).astype(BF16), expand)
        dtx = _dot(dtv[rows, :].astype(BF16), expand)
        cd = jnp.broadcast_to(jnp.exp(a_last), (SUBLANES, LANES))
        cd_hi = cd.astype(BF16)
        cd_lo = (cd - cd_hi.astype(F32)).astype(BF16)
        cdx = (_dot(cd_hi, expand) + _dot(cd_lo, expand))[0:1, :]
        xd = actx_scr[rows, :] * dtx
        xdd = xd * dec
        xdb = xd.astype(BF16)
        for g in range(SSM_N_GROUPS):
            gc = slice(g * gw, (g + 1) * gw)
            nc = slice(g * SSM_D_STATE, (g + 1) * SSM_D_STATE)
            bg = bmat_scr[rows, nc]
            cg = cmat_scr[rows, nc]
            cb = _dot_nt(cg, bg)
            xg = xdb[:, gc]
            mhs, xms = [], []
            for r in range(SSM_HEADS_PER_GROUP):
                hd = g * SSM_HEADS_PER_GROUP + r
                seg = acs[:, hd:hd + 1] - acs_t[hd:hd + 1, :]
                lmat = jnp.exp(jnp.where(causal, seg, NEG_BIG))
                mhs.append((cb * lmat).astype(BF16))
                xms.append(xg * head_mask[r])
            yg = (_dot(cg, state_scr[g].astype(BF16)) * e_in[:, gc]
                  + _dot(jnp.concatenate(mhs, axis=1), jnp.concatenate(xms, axis=0)))
            y_scr[rows, gc] = yg
            state_scr[g] = state_scr[g] * cdx[:, gc] + _dot_tn(bg, xdd[:, gc].astype(BF16))

    mix = None
    for g in range(SSM_N_GROUPS):
        gc = slice(g * gw, (g + 1) * gw)
        zz = z_scr[:, gc]
        yv = (y_scr[:, gc] + dexp_ref[:, gc] * actx_scr[:, gc]) * (zz * _sigmoid(zz))
        yv = yv * lax.rsqrt(jnp.mean(yv * yv, -1, keepdims=True) + SSM_NORM_EPS) * normg_ref[:, gc]
        part = _dot(yv.astype(BF16), wout_ref[gc, :])
        mix = part if mix is None else mix + part
    o_ref[...] = _layer_norm(DEEPNORM_ALPHA * x + mix, g_ref[...], b_ref[...])


def _ssd_layer(h, w_in, conv_w, conv_b, dt_bias, a_log, d_skip, norm_g, w_out, ln_g, ln_b):
    bsz, seq, d = h.shape
    n_heads = a_log.shape[0]
    d_inner = n_heads * SSM_HEAD_DIM
    gn = SSM_N_GROUPS * SSM_D_STATE
    conv_dim = d_inner + 2 * gn
    ts, q = SSD_TILE, SSD_CHUNK
    w_in = w_in.astype(BF16)
    wz = w_in[:, :d_inner]
    wx = w_in[:, d_inner:2 * d_inner]
    wb = w_in[:, 2 * d_inner:2 * d_inner + gn]
    wc = w_in[:, 2 * d_inner + gn:2 * d_inner + 2 * gn]
    pad = LANES - n_heads
    wdt = jnp.pad(w_in[:, d_inner + conv_dim:], ((0, 0), (0, pad)))
    dtb = jnp.pad(dt_bias.astype(F32), (0, pad)).reshape(1, LANES)
    alog = jnp.pad(a_log.astype(F32), (0, pad)).reshape(1, LANES)
    dexp = jnp.repeat(d_skip.astype(F32), SSM_HEAD_DIM).reshape(1, d_inner)
    expand = (jnp.arange(LANES)[:, None] == (jnp.arange(d_inner)[None, :] // SSM_HEAD_DIM)).astype(BF16)
    tile = pl.BlockSpec((None, ts, d), lambda b, s: (b, s, 0))
    return pl.pallas_call(
        functools.partial(_ssd_kernel, ts=ts, q=q, d_inner=d_inner, gn=gn),
        out_shape=jax.ShapeDtypeStruct((bsz, seq, d), F32),
        grid=(bsz, seq // ts),
        in_specs=[tile, _const_spec((d, d_inner)), _const_spec((d, d_inner)), _const_spec((d, gn)),
                  _const_spec((d, gn)), _const_spec((d, LANES)), _const_spec((SSM_CONV, conv_dim)),
                  _const_spec((1, conv_dim)), _const_spec((1, LANES)), _const_spec((1, LANES)),
                  _const_spec((1, d_inner)), _const_spec((1, d_inner)), _const_spec((LANES, d_inner)),
                  _const_spec((d_inner, d)), _const_spec((1, d)), _const_spec((1, d))],
        out_specs=tile,
        scratch_shapes=[pltpu.VMEM((conv_dim // LANES, CONV_HALO + ts, LANES), F32), pltpu.VMEM((ts, d_inner), F32),
                        pltpu.VMEM((ts, d_inner), F32), pltpu.VMEM((ts, gn), BF16), pltpu.VMEM((ts, gn), BF16),
                        pltpu.VMEM((ts, d_inner), F32),
                        pltpu.VMEM((SSM_N_GROUPS, SSM_D_STATE, d_inner // SSM_N_GROUPS), F32)],
        compiler_params=_params(2),
        name="ssd_mixer_ln",
    )(h, wz, wx, wb, wc, wdt, conv_w.astype(F32), conv_b.reshape(1, conv_dim), dtb, alog, dexp,
      norm_g.reshape(1, d_inner), expand, w_out.astype(BF16), ln_g.reshape(1, d), ln_b.reshape(1, d))


def _xattn_kernel(h_ref, mem_ref, wq_ref, wk_ref, wv_ref, wo_ref, g_ref, b_ref, wrh_ref, wrl_ref, br_ref,
                  o_ref, route_ref, route_t_ref, cnt_ref, k_scr, v_scr, carry_scr, *, ts, hd):
    b = pl.program_id(0)
    s = pl.program_id(1)

    @pl.when(s == 0)
    def _():
        mb = mem_ref[...].astype(BF16)
        k_scr[...] = _dot(mb, wk_ref[...]).astype(BF16)
        v_scr[...] = _dot(mb, wv_ref[...]).astype(BF16)

    @pl.when((b == 0) & (s == 0))
    def _():
        carry_scr[...] = jnp.zeros(carry_scr.shape, F32)

    h = h_ref[...]
    qv = (_dot(h.astype(BF16), wq_ref[...]) * (hd ** -0.5)).astype(BF16)
    heads = []
    for hh in range(XA_HEADS):
        cols = slice(hh * hd, (hh + 1) * hd)
        sc = _dot_nt(qv[:, cols], k_scr[:, cols])
        p = jnp.exp(sc - jnp.max(sc, -1, keepdims=True))
        o = _dot(p.astype(BF16), v_scr[:, cols]) / jnp.sum(p, -1, keepdims=True)
        heads.append(o.astype(BF16))
    xa = _dot(jnp.concatenate(heads, axis=1), wo_ref[...])
    h2 =_layer_norm(DEEPNORM_ALPHA * h + xa, g_ref[...], b_ref[...])
    o_ref[...] = h2
    h2_hi = h2.astype(BF16)

    h2_lo = (h2 - h2_hi.astype(F32)).astype(BF16)
    logits = _dot(h2_hi, wrh_ref[...]) + _dot(h2_lo, wrh_ref[...]) + _dot(h2_hi, wrl_ref[...]) + br_ref[...]
    lane = lax.broadcasted_iota(jnp.int32, (ts, LANES), 1).astype(F32)
    work = logits
    vals, idxs, sels = [], [], []
    for _ in range(TOP_K):
        m = jnp.max(work, -1, keepdims=True)
        ik = jnp.min(jnp.where(work == m, lane, float(LANES)), -1, keepdims=True)
        sel = lane == ik
        vals.append(m)
        idxs.append(ik)
        sels.append(sel)
        work = jnp.where(sel, -jnp.inf, work)
    exps = [jnp.exp(v - vals[0]) for v in vals]
    den = exps[0]
    for e in exps[1:]:
        den = den + e
    onehot = jnp.zeros((ts, LANES), F32)
    for sel in sels:
        onehot = onehot + sel.astype(F32)
    below = (lax.broadcasted_iota(jnp.int32, (ts, ts), 0) > lax.broadcasted_iota(jnp.int32, (ts, ts), 1))
    before = _dot(jnp.where(below, 1.0, 0.0).astype(BF16), onehot.astype(BF16)) + carry_scr[0:1, :]
    route = jnp.zeros((ts, LANES), F32)
    for k in range(TOP_K):
        rank = jnp.sum(jnp.where(sels[k], before, 0.0), -1, keepdims=True)
        route = jnp.where(lane == float(k), idxs[k], route)
        route = jnp.where(lane == float(TOP_K + k), exps[k] / den, route)
        route = jnp.where(lane == float(2 * TOP_K + k), rank, route)
    route_ref[...] = route
    route_t_ref[...] = route.T[0:ROUTE_ROWS, :]
    carry_scr[...] = carry_scr[...] + jnp.sum(onehot, 0, keepdims=True)
    cnt_ref[...] = carry_scr[...]


def _xattn_layer(h, mem, wq, wk, wv, wo, ln_g, ln_b, w_router, b_router):
    bsz, seq, d = h.shape
    mlen = mem.shape[1]
    ts = XATTN_TILE
    hd = d // XA_HEADS
    n_exp = w_router.shape[1]
    wr = jnp.pad(w_router.astype(F32), ((0, 0), (0, LANES - n_exp)))
    wr_hi = wr.astype(BF16)
    wr_lo = (wr - wr_hi.astype(F32)).astype(BF16)
    br = jnp.pad(b_router.astype(F32), (0, LANES - n_exp), constant_values=-jnp.inf).reshape(1, LANES)
    tile = pl.BlockSpec((None, ts, d), lambda b, s: (b, s, 0))
    rtile = pl.BlockSpec((None, ts, LANES), lambda b, s: (b, s, 0))
    return pl.pallas_call(
        functools.partial(_xattn_kernel, ts=ts, hd=hd),
        out_shape=(jax.ShapeDtypeStruct((bsz, seq, d), F32), jax.ShapeDtypeStruct((bsz, seq, LANES), F32),
                   jax.ShapeDtypeStruct((ROUTE_ROWS, bsz * seq), F32), jax.ShapeDtypeStruct((SUBLANES, LANES), F32)),
        grid=(bsz, seq // ts),
        in_specs=[tile, pl.BlockSpec((None, mlen, d), lambda b, s: (b, 0, 0)),
                  _const_spec((d, d)), _const_spec((d, d)), _const_spec((d, d)), _const_spec((d, d)),
                  _const_spec((1, d)), _const_spec((1, d)), _const_spec((d, LANES)), _const_spec((d, LANES)),
                  _const_spec((1, LANES))],
        out_specs=(tile, rtile, pl.BlockSpec((ROUTE_ROWS, ts), lambda b, s: (0, b * (seq // ts) + s)),
                   pl.BlockSpec((SUBLANES, LANES), lambda b, s: (0, 0))),
        scratch_shapes=[pltpu.VMEM((mlen, d), BF16), pltpu.VMEM((mlen, d), BF16), pltpu.VMEM((SUBLANES, LANES), F32)],
        compiler_params=_params(2),
        name="xattn_ln_router",
    )(h, mem, wq.astype(BF16), wk.astype(BF16), wv.astype(BF16), wo.astype(BF16),
      ln_g.reshape(1, d), ln_b.reshape(1, d), wr_hi, wr_lo, br)


def _rows_from_tiles(ref, n):
    return jnp.concatenate([ref[pl.ds(c, n, stride=SUBLANES), :] for c in range(SUBLANES)], axis=1)


def _rows_to_tiles(ref, v, n):
    for c in range(SUBLANES):
        ref[pl.ds(c, n, stride=SUBLANES), :] = v[:, c * LANES:(c + 1) * LANES]


def _moe_kernel(be_ref, nu_ref, nv_ref, x_ref, wgu_ref, bg_ref, bu_ref, wd_ref, bd_ref, o_ref, wg_scr, wu_scr, wd_scr,
                *, tm):
    i = pl.program_id(0)
    active = i < nu_ref[0]
    new_expert = (i == 0) | (be_ref[i] != be_ref[jnp.maximum(i - 1, 0)])

    @pl.when(active & new_expert)
    def _():
        w2 = 2 * LANES
        src = lax.broadcasted_iota(jnp.int32, (w2, w2), 0)
        dst = lax.broadcasted_iota(jnp.int32, (w2, w2), 1)
        perm = jnp.where(src == jnp.where(dst < LANES, 2 * dst, 2 * (dst - LANES) + 1), 1.0, 0.0).astype(BF16)
        for c in range(wgu_ref.shape[1] // w2):
            res = _dot(wgu_ref[:, c * w2:(c + 1) * w2].astype(BF16), perm)
            wg_scr[:, c * LANES:(c + 1) * LANES] = res[:, :LANES].astype(BF16)
            wu_scr[:, c * LANES:(c + 1) * LANES] = res[:, LANES:].astype(BF16)
        wd_scr[...] = wd_ref[...].astype(BF16)

    nv = nv_ref[i]
    for r in range(MOE_ROW_GROUP, tm + 1, MOE_ROW_GROUP):
        @pl.when(active & (nv > r - MOE_ROW_GROUP) & (nv <= r))
        def _():
            x = _rows_from_tiles(x_ref, r).astype(BF16)
            gate = jnp.minimum(_dot(x, wg_scr[...]) + bg_ref[...], SWIGLU_LIMIT)
            up = jnp.clip(_dot(x, wu_scr[...]) + bu_ref[...], -SWIGLU_LIMIT, SWIGLU_LIMIT)
            act = (up + 1.0) * (gate * _sigmoid(SWIGLU_ALPHA * gate))
            _rows_to_tiles(o_ref, _dot(act.astype(BF16), wd_scr[...]) + bd_ref[...], r)
            if r < tm:
                o_ref[r * SUBLANES:, :] = jnp.zeros(((tm - r) * SUBLANES, LANES), F32)

    @pl.when(jnp.logical_not(active))
    def _():
        o_ref[...] = jnp.zeros(o_ref.shape, F32)


def _moe_experts(xs, block_e, n_used, n_valid, layer, w_gate_up, b_gate, b_up, w_down, b_down):
    _, n_exp, f, d = w_down.shape
    n_rows = xs.shape[0] // SUBLANES
    tm = MOE_TILE
    n_blocks = n_rows // tm

    def row_map(i, be, nu, nv):
        return (jnp.minimum(i, nu[0] - 1), 0)

    def w_map(i, be, nu, nv):
        return (be[i], 0, 0)

    def lw_map(i, be, nu, nv):
        return (layer, be[i], 0, 0)

    grid_spec = pltpu.PrefetchScalarGridSpec(
        num_scalar_prefetch=3,
        grid=(n_blocks,),
        in_specs=[pl.BlockSpec((tm * SUBLANES, LANES), row_map),
                  pl.BlockSpec((None, None, d, 2 * f), lw_map),
                  pl.BlockSpec((None, 1, f), w_map), pl.BlockSpec((None, 1, f), w_map),
                  pl.BlockSpec((None, None, f, d), lw_map), pl.BlockSpec((None, 1, d), w_map)],
        out_specs=pl.BlockSpec((tm * SUBLANES, LANES), lambda i, be, nu, nv: (i, 0)),
        scratch_shapes=[pltpu.VMEM((d, f), BF16), pltpu.VMEM((d, f), BF16), pltpu.VMEM((f, d), BF16)],
    )
    return pl.pallas_call(
        functools.partial(_moe_kernel, tm=tm),
        out_shape=jax.ShapeDtypeStruct((n_rows * SUBLANES, LANES), F32),
        grid_spec=grid_spec,
        compiler_params=_params(1),
        name="moe_experts",
    )(block_e, n_used, n_valid, xs, w_gate_up, b_gate.reshape(n_exp, 1, f), b_up.reshape(n_exp, 1, f),
      w_down, b_down.reshape(n_exp, 1, d))


def _row_copy(src, src_row, dst, dst_row, sem):
    return pltpu.make_async_copy(src.at[pl.ds(pl.multiple_of(src_row * SUBLANES, SUBLANES), SUBLANES)],
                                 dst.at[pl.ds(pl.multiple_of(dst_row * SUBLANES, SUBLANES), SUBLANES)], sem)


def _dispatch_kernel(dest_ref, zblk_ref, h_ref, xs_ref, stage0, stage1, zero_scr, sem, zsem, *, ts, n_tiles, tm):
    i = pl.program_id(0)

    @pl.when(i == 0)
    def _():
        zero_scr[...] = jnp.zeros(zero_scr.shape, F32)
        for j in range(zblk_ref.shape[0]):
            @pl.when(zblk_ref[j] >= 0)
            def _():
                start = pl.multiple_of(zblk_ref[j] * (tm * SUBLANES), tm * SUBLANES)
                pltpu.make_async_copy(zero_scr, xs_ref.at[pl.ds(start, tm * SUBLANES)], zsem).start()
        for j in range(zblk_ref.shape[0]):
            @pl.when(zblk_ref[j] >= 0)
            def _():
                pltpu.make_async_copy(zero_scr, xs_ref.at[pl.ds(0, tm * SUBLANES)], zsem).wait()

    def drain(stage, s):
        for _ in range(TOP_K):
            pltpu.make_async_copy(stage, xs_ref.at[pl.ds(0, ts * SUBLANES)], sem.at[s]).wait()

    def step(stage, s, other, so):
        @pl.when(i >= 2)
        def _():
            drain(stage, s)

        _rows_to_tiles(stage, h_ref[...], ts)

        def issue(j, carry):
            for u in range(ROWS_PER_ISSUE):
                tok = j * ROWS_PER_ISSUE + u
                for k in range(TOP_K):
                    row = dest_ref[k * (n_tiles * ts) + i * ts + tok]
                    _row_copy(stage, tok, xs_ref, row, sem.at[s]).start(priority=k % 2)
            return carry

        lax.fori_loop(0, ts // ROWS_PER_ISSUE, issue, 0)

        @pl.when(i == n_tiles - 1)
        def _():
            drain(other, so)
            drain(stage, s)

    @pl.when(i % 2 == 0)
    def _():
        step(stage0, 0, stage1, 1)

    @pl.when(i % 2 == 1)
    def _():
        step(stage1, 1, stage0, 0)


def _dispatch_rows(h, dest, zero_blocks, n_rows):
    t, d = h.shape
    ts = DISPATCH_TILE
    tm = MOE_TILE
    n_tiles = t // ts
    assert d == SUBLANES * LANES and n_tiles >= 2
    grid_spec = pltpu.PrefetchScalarGridSpec(
        num_scalar_prefetch=2,
        grid=(n_tiles,),
        in_specs=[pl.BlockSpec((ts, d), lambda i, dest, zb: (i, 0))],
        out_specs=pl.BlockSpec(memory_space=pl.ANY),
        scratch_shapes=[pltpu.VMEM((ts * SUBLANES, LANES), F32), pltpu.VMEM((ts * SUBLANES, LANES), F32),
                        pltpu.VMEM((tm * SUBLANES, LANES), F32),
                        pltpu.SemaphoreType.DMA((2,)), pltpu.SemaphoreType.DMA(())],
    )
    return pl.pallas_call(
        functools.partial(_dispatch_kernel, ts=ts, n_tiles=n_tiles, tm=tm),
        out_shape=jax.ShapeDtypeStruct((n_rows * SUBLANES, LANES), F32),
        grid_spec=grid_spec,
        compiler_params=_params(1),
        name="moe_dispatch",
    )(dest, zero_blocks, h)


def _combine_kernel(dest_ref, h_ref, route_ref, g_ref, b_ref, y_ref, o_ref, gbuf0, gbuf1, sem, *, ts, n_tiles):
    i = pl.program_id(0)

    def issue(tile, gbuf, s):
        def body(j, carry):
            for u in range(ROWS_PER_ISSUE):
                tok = j * ROWS_PER_ISSUE + u
                for k in range(TOP_K):
                    row = dest_ref[k * (n_tiles * ts) + tile * ts + tok]
                    _row_copy(y_ref, row, gbuf.at[k], tok, sem.at[s]).start(priority=k % 2)
            return carry

        lax.fori_loop(0, ts // ROWS_PER_ISSUE, body, 0)

    def drain(gbuf, s):
        for k in range(TOP_K):
            pltpu.make_async_copy(y_ref.at[pl.ds(0, ts * SUBLANES)], gbuf.at[k], sem.at[s]).wait()

    def step(gbuf, s, nxt, sn):
        @pl.when(i + 1 < n_tiles)
        def _():
            issue(i + 1, nxt, sn)

        drain(gbuf, s)
        parts = []
        for c in range(SUBLANES):
            acc = None
            for k in range(TOP_K):
                piece = gbuf[k, pl.ds(c, ts, stride=SUBLANES), :] * route_ref[:, TOP_K + k:TOP_K + k + 1]
                acc = piece if acc is None else acc + piece
            parts.append(acc)
        ff = jnp.concatenate(parts, axis=1)
        o_ref[...] = _layer_norm(DEEPNORM_ALPHA * h_ref[...] + ff, g_ref[...], b_ref[...])

    @pl.when(i == 0)
    def _():
        issue(0, gbuf0, 0)

    @pl.when(i % 2 == 0)
    def _():
        step(gbuf0, 0, gbuf1, 1)

    @pl.when(i % 2 == 1)
    def _():
        step(gbuf1, 1, gbuf0, 0)


def _combine_layer(h, y, dest, route, ln_g, ln_b):
    t, d = h.shape
    ts = COMBINE_TILE
    n_tiles = t // ts
    grid_spec = pltpu.PrefetchScalarGridSpec(
        num_scalar_prefetch=1,
        grid=(n_tiles,),
        in_specs=[pl.BlockSpec((ts, d), lambda i, dest: (i, 0)), pl.BlockSpec((ts, LANES), lambda i, dest: (i, 0)),
                  pl.BlockSpec((1, d), lambda i, dest: (0, 0)), pl.BlockSpec((1, d), lambda i, dest: (0, 0)),
                  pl.BlockSpec(memory_space=pl.ANY)],
        out_specs=pl.BlockSpec((ts, d), lambda i, dest: (i, 0)),
        scratch_shapes=[pltpu.VMEM((TOP_K, ts * SUBLANES, LANES), F32),
                        pltpu.VMEM((TOP_K, ts * SUBLANES, LANES), F32), pltpu.SemaphoreType.DMA((2,))],
    )
    return pl.pallas_call(
        functools.partial(_combine_kernel, ts=ts, n_tiles=n_tiles),
        out_shape=jax.ShapeDtypeStruct((t, d), F32),
        grid_spec=grid_spec,
        compiler_params=_params(1),
        name="moe_combine_ln",
    )(dest, h, route, ln_g.reshape(1, d), ln_b.reshape(1, d), y)


def _moe_layer(h2, route, route_t, cnt, layer, w_gate_up, b_gate_up, w_down, b_down, ln_g, ln_b):
    bsz, seq, d = h2.shape
    t = bsz * seq
    tm = MOE_TILE
    n_exp = w_gate_up.shape[1]
    h2 = h2.reshape(t, d)
    route = route.reshape(t, LANES)
    idx = route_t[0:TOP_K].astype(jnp.int32)
    rank = route_t[2 * TOP_K:3 * TOP_K].astype(jnp.int32)
    counts = cnt[0, :n_exp].astype(jnp.int32)
    padded = (counts + tm - 1) // tm * tm
    pend = jnp.cumsum(padded)
    pstart = pend - padded
    part = counts % tm
    skip = jnp.where(part > 0, tm - part, 0)

    def per_expert(table, e):
        return jnp.sum(jnp.where(e[..., None] == jnp.arange(n_exp, dtype=jnp.int32), table, 0), axis=-1)

    dest = (per_expert(pstart, idx) + rank
            + jnp.where(rank >= per_expert(part, idx), per_expert(skip, idx), 0)).reshape(-1)
    n_blocks = (t * TOP_K) // tm + n_exp
    n_used = (pend[-1] // tm).astype(jnp.int32)
    blk = jnp.arange(n_blocks, dtype=jnp.int32)
    block_e = jnp.sum((pend[None, :] <= (blk * tm)[:, None]).astype(jnp.int32), axis=1)
    block_e = jnp.minimum(block_e, n_exp - 1)
    block_e = jnp.where(blk < n_used, block_e, block_e[jnp.maximum(n_used - 1, 0)])
    first_blk = jnp.where(part > 0, pstart // tm, -1)
    tail_blk = n_used + jnp.arange(n_exp, dtype=jnp.int32)
    zero_blocks = jnp.concatenate([first_blk, jnp.where(tail_blk < n_blocks, tail_blk, -1)]).astype(jnp.int32)
    xs = _dispatch_rows(h2, dest, zero_blocks, n_blocks * tm)
    n_valid = jnp.where((blk == per_expert(first_blk, block_e)) & (blk < n_used), per_expert(part, block_e),
                        tm).astype(jnp.int32)
    y = _moe_experts(xs, block_e, n_used.reshape(1), n_valid, layer, w_gate_up, b_gate_up[:, 0::2], b_gate_up[:, 1::2],
                     w_down, b_down)
    return _combine_layer(h2, y, dest, route, ln_g, ln_b).reshape(bsz, seq, d)


def kernel(x, mem, pool_w_in, pool_w_grp, pool_scale, pool_w_out, ssm_w_in, ssm_conv_w, ssm_conv_b, ssm_dt_bias, ssm_a_log, ssm_d, ssm_norm_g, ssm_w_out, xa_wq, xa_wk, xa_wv, xa_wo, moe_w_router, moe_b_router, moe_w_gate_up, moe_b_gate_up, moe_w_down, moe_b_down, ln_mix_g, ln_mix_b, ln_xa_g, ln_xa_b, ln_ffn_g, ln_ffn_b):
    h = x
    for i in range(DEPTH):
        j = i // 2
        if i % 2 == 0:
            h = _pool_layer(h, pool_w_in[j], pool_w_grp[j], pool_scale[j], pool_w_out[j], ln_mix_g[i], ln_mix_b[i])
        else:
            h = _ssd_layer(h, ssm_w_in[j], ssm_conv_w[j], ssm_conv_b[j], ssm_dt_bias[j], ssm_a_log[j], ssm_d[j],
                           ssm_norm_g[j], ssm_w_out[j], ln_mix_g[i], ln_mix_b[i])
        h2, route, route_t, cnt = _xattn_layer(h, mem, xa_wq[i], xa_wk[i], xa_wv[i], xa_wo[i], ln_xa_g[i], ln_xa_b[i],
                                      moe_w_router[i], moe_b_router[i])
        h = _moe_layer(h2, route, route_t, cnt, i, moe_w_gate_up, moe_b_gate_up[i], moe_w_down, moe_b_down[i],
                       ln_ffn_g[i], ln_ffn_b[i])
    return h
```

```python
import functools

import jax
import jax.numpy as jnp
from jax import lax
from jax.experimental import pallas as pl
from jax.experimental.pallas import tpu as pltpu

F32 = jnp.float32
BF16 = jnp.bfloat16

DEPTH = 2
DEEPNORM_ALPHA = (2 * DEPTH) ** 0.25
LN_EPS = 1e-5
POOL_WINDOWS = (2, 4, 8, 16)
POOL_HALO = 16
SSM_HEAD_DIM = 64
SSM_N_GROUPS = 8
SSM_HEADS_PER_GROUP = 4
SSM_D_STATE = 128
SSM_CONV = 4
CONV_HALO = 8
SSM_NORM_EPS = 1e-5
XA_HEADS = 4
N_EXPERTS = 32
TOP_K = 4
SWIGLU_LIMIT = 7.0
SWIGLU_ALPHA = 1.702

LANES = 128
SUBLANES = 8
VMEM_LIMIT_BYTES = 56 * 1024 * 1024

POOL_TILE = 512
XATTN_TILE = 1024
XATTN_SPLIT = 2
SSD_TILE = 256
SSD_CHUNK = 128
SSD_STRIP = 512
MOE_TILE = 512
MOE_ROW_GROUP = 128
DISPATCH_TILE = 512
COMBINE_TILE = 512
ROWS_PER_ISSUE = 8

ROUTE_ROWS = 16

NEG_BIG = -1e30


def _layer_norm(v, g, b):
    mu = jnp.mean(v, -1, keepdims=True)
    d = v - mu
    var = jnp.mean(d * d, -1, keepdims=True)
    return d * lax.rsqrt(var + LN_EPS) * g + b


def _dot(a, b):
    return jnp.dot(a, b, preferred_element_type=F32)


def _dot_nt(a, b):
    return lax.dot_general(a, b, (((1,), (1,)), ((), ())), preferred_element_type=F32)


def _dot_tn(a, b):
    return lax.dot_general(a, b, (((0,), (0,)), ((), ())), preferred_element_type=F32)


def _const_spec(shape):
    nd = len(shape)
    return pl.BlockSpec(shape, lambda *_: (0,) * nd, pipeline_mode=pl.Buffered(1))


def _params(n_axes):
    return pltpu.CompilerParams(dimension_semantics=("arbitrary",) * n_axes,
                                vmem_limit_bytes=VMEM_LIMIT_BYTES)


def _pool_kernel(x_ref, win_ref, wgrp_ref, scale_ref, wout_ref, g_ref, b_ref, o_ref, ext_ref, *, ts, gd):
    s = pl.program_id(1)

    @pl.when(s == 0)
    def _():
        ext_ref[0:POOL_HALO, :] = jnp.zeros((POOL_HALO, ext_ref.shape[1]), F32)

    x = x_ref[...]
    ext_ref[POOL_HALO:, :] = _dot(x.astype(BF16), win_ref[...])
    pos = lax.broadcasted_iota(jnp.int32, (ts, 1), 0) + s * ts
    mix = None
    for g, w in enumerate(POOL_WINDOWS):
        cols = slice(g * gd, (g + 1) * gd)
        e = ext_ref[:, cols]
        acc = e
        sh = 1
        while sh < w:
            acc = acc + pltpu.roll(acc, sh, axis=0)
            sh *= 2
        cnt = jnp.minimum(pos + 1, w).astype(F32)
        m = acc[POOL_HALO:, :] / cnt - e[POOL_HALO:, :]
        yg = _dot(m.astype(BF16), wgrp_ref[g]) * scale_ref[:, cols]
        part = _dot(yg.astype(BF16), wout_ref[cols, :])
        mix = part if mix is None else mix + part
    ext_ref[0:POOL_HALO, :] = ext_ref[ts:ts + POOL_HALO, :]
    o_ref[...] = _layer_norm(DEEPNORM_ALPHA * x + mix, g_ref[...], b_ref[...])


def _pool_layer(h, w_in, w_grp, scale, w_out, ln_g, ln_b):
    bsz, seq, d = h.shape
    ts = POOL_TILE
    gd = d // len(POOL_WINDOWS)
    tile = pl.BlockSpec((None, ts, d), lambda b, s: (b, s, 0))
    return pl.pallas_call(
        functools.partial(_pool_kernel, ts=ts, gd=gd),
        out_shape=jax.ShapeDtypeStruct((bsz, seq, d), F32),
        grid=(bsz, seq // ts),
        in_specs=[tile, _const_spec((d, d)), _const_spec((len(POOL_WINDOWS), gd, gd)), _const_spec((1, d)),
                  _const_spec((d, d)), _const_spec((1, d)), _const_spec((1, d))],
        out_specs=tile,
        scratch_shapes=[pltpu.VMEM((POOL_HALO + ts, d), F32)],
        compiler_params=_params(2),
        name="pool_mixer_ln",
    )(h, w_in.astype(BF16), w_grp.astype(BF16), scale.reshape(1, d), w_out.astype(BF16),
      ln_g.reshape(1, d), ln_b.reshape(1, d))


def _sigmoid(v):
    return 1.0 / (1.0 + jnp.exp(-v))


def _ssd_kernel(x_ref, wz_ref, wx_ref, wb_ref, wc_ref, wdt_ref, convw_ref, convb_ref, dtb_ref, alog_ref,
                dexp_ref, normg_ref, expand_ref, wout_ref, g_ref, b_ref, o_ref,
                xbc_scr, z_scr, actx_scr, bmat_scr, cmat_scr, y_scr, state_scr, *, ts, q, d_inner, gn):
    s = pl.program_id(1)
    conv_dim = d_inner + 2 * gn
    gw = d_inner // SSM_N_GROUPS

    @pl.when(s == 0)
    def _():
        xbc_scr[:, 0:CONV_HALO, :] = jnp.zeros((conv_dim // LANES, CONV_HALO, LANES), F32)
        state_scr[...] = jnp.zeros(state_scr.shape, F32)

    x = x_ref[...]
    xb = x.astype(BF16)
    dt_raw = _dot(xb, wdt_ref[...]) + dtb_ref[...]
    dtv = jnp.maximum(dt_raw, 0.0) + jnp.log1p(jnp.exp(-jnp.abs(dt_raw)))
    a_all = dtv * (-jnp.exp(alog_ref[...]))

    for w_ref, base in ((wb_ref, d_inner), (wc_ref, d_inner + gn), (wx_ref, 0)):
        for c in range(0, w_ref.shape[1], SSD_STRIP):
            res = _dot(xb, w_ref[:, c:c + SSD_STRIP])
            for jj in range(SSD_STRIP // LANES):
                j = (base + c) // LANES + jj
                cs = slice(j * LANES, (j + 1) * LANES)
                xbc_scr[j, CONV_HALO:, :] = res[:, jj * LANES:(jj + 1) * LANES]
                acc = convb_ref[:, cs]
                for k in range(SSM_CONV):
                    r0 = CONV_HALO - (SSM_CONV - 1) + k
                    acc = acc + convw_ref[k:k + 1, cs] * xbc_scr[j, r0:r0 + ts, :]
                act = acc * _sigmoid(acc)
                col = j * LANES
                if col < d_inner:
                    actx_scr[:, cs] = act
                elif col < d_inner + gn:
                    bmat_scr[:, col - d_inner:col - d_inner + LANES] = act.astype(BF16)
                else:
                    cmat_scr[:, col - d_inner - gn:col - d_inner - gn + LANES] = act.astype(BF16)
                xbc_scr[j, 0:CONV_HALO, :] = xbc_scr[j, ts:ts + CONV_HALO, :]

    expand = expand_ref[...]
    row_i = lax.broadcasted_iota(jnp.int32, (q, LANES), 0)
    causal = lax.broadcasted_iota(jnp.int32, (q, q), 0) >= lax.broadcasted_iota(jnp.int32, (q, q), 1)
    head_of_lane = lax.broadcasted_iota(jnp.int32, (q, gw), 1) // SSM_HEAD_DIM
    head_mask = [jnp.where(head_of_lane == r, 1.0, 0.0).astype(BF16) for r in range(SSM_HEADS_PER_GROUP)]

    for c in range(ts // q):
        rows = slice(c * q, (c + 1) * q)
        acs = a_all[rows, :]
        sh = 1
        while sh < q:
            acs = acs + jnp.where(row_i >= sh, pltpu.roll(acs, sh, axis=0), 0.0)
            sh *= 2
        acs_t = acs.T
        a_last = acs[q - 1:q, :]
        e_in = _dot(jnp.exp(acs).astype(BF16), expand)
        dec = _dot(jnp.exp(a_last - acs).astype(BF16), expand)
        dtx = _dot(dtv[rows, :].astype(BF16), expand)
        cd = jnp.broadcast_to(jnp.exp(a_last), (SUBLANES, LANES))
        cd_hi = cd.astype(BF16)
        cd_lo = (cd - cd_hi.astype(F32)).astype(BF16)
        cdx = (_dot(cd_hi, expand) + _dot(cd_lo, expand))[0:1, :]
        xd = actx_scr[rows, :] * dtx
        xdd = xd * dec
        xdb = xd.astype(BF16)
        for g in range(SSM_N_GROUPS):
            gc = slice(g * gw, (g + 1) * gw)
            nc = slice(g * SSM_D_STATE, (g + 1) * SSM_D_STATE)
            if c == 0:
                z_scr[:, gc] = _dot(xb, wz_ref[:, gc])
            bg = bmat_scr[rows, nc]
            cg = cmat_scr[rows, nc]
            cb = _dot_nt(cg, bg)
            xg = xdb[:, gc]
            mhs, xms = [], []
            for r in range(SSM_HEADS_PER_GROUP):
                hd = g * SSM_HEADS_PER_GROUP + r
                seg = acs[:, hd:hd + 1] - acs_t[hd:hd + 1, :]
                lmat = jnp.exp(jnp.where(causal, seg, NEG_BIG))
                mhs.append((cb * lmat).astype(BF16))
                xms.append(xg * head_mask[r])
            yg = (_dot(cg, state_scr[g].astype(BF16)) * e_in[:, gc]
                  + _dot(jnp.concatenate(mhs, axis=1), jnp.concatenate(xms, axis=0)))
            y_scr[rows, gc] = yg
            state_scr[g] = state_scr[g] * cdx[:, gc] + _dot_tn(bg, xdd[:, gc].astype(BF16))

    mix = None
    for g in range(SSM_N_GROUPS):
        gc = slice(g * gw, (g + 1) * gw)
        zz = z_scr[:, gc]
        yv = (y_scr[:, gc] + dexp_ref[:, gc] * actx_scr[:, gc]) * (zz * _sigmoid(zz))
        yv = yv * lax.rsqrt(jnp.mean(yv * yv, -1, keepdims=True) + SSM_NORM_EPS) * normg_ref[:, gc]
        part = _dot(yv.astype(BF16), wout_ref[gc, :])
        mix = part if mix is None else mix + part
    o_ref[...] = _layer_norm(DEEPNORM_ALPHA * x + mix, g_ref[...], b_ref[...])


def _ssd_layer(h, w_in, conv_w, conv_b, dt_bias, a_log, d_skip, norm_g, w_out, ln_g, ln_b):
    bsz, seq, d = h.shape
    n_heads = a_log.shape[0]
    d_inner = n_heads * SSM_HEAD_DIM
    gn = SSM_N_GROUPS * SSM_D_STATE
    conv_dim = d_inner + 2 * gn
    ts, q = SSD_TILE, SSD_CHUNK
    w_in = w_in.astype(BF16)
    wz = w_in[:, :d_inner]
    wx = w_in[:, d_inner:2 * d_inner]
    wb = w_in[:, 2 * d_inner:2 * d_inner + gn]
    wc = w_in[:, 2 * d_inner + gn:2 * d_inner + 2 * gn]
    pad = LANES - n_heads
    wdt = jnp.pad(w_in[:, d_inner + conv_dim:], ((0, 0), (0, pad)))
    dtb = jnp.pad(dt_bias.astype(F32), (0, pad)).reshape(1, LANES)
    alog = jnp.pad(a_log.astype(F32), (0, pad)).reshape(1, LANES)
    dexp = jnp.repeat(d_skip.astype(F32), SSM_HEAD_DIM).reshape(1, d_inner)
    expand = (jnp.arange(LANES)[:, None] == (jnp.arange(d_inner)[None, :] // SSM_HEAD_DIM)).astype(BF16)
    tile = pl.BlockSpec((None, ts, d), lambda b, s: (b, s, 0))
    return pl.pallas_call(
        functools.partial(_ssd_kernel, ts=ts, q=q, d_inner=d_inner, gn=gn),
        out_shape=jax.ShapeDtypeStruct((bsz, seq, d), F32),
        grid=(bsz, seq // ts),
        in_specs=[tile, _const_spec((d, d_inner)), _const_spec((d, d_inner)), _const_spec((d, gn)),
                  _const_spec((d, gn)), _const_spec((d, LANES)), _const_spec((SSM_CONV, conv_dim)),
                  _const_spec((1, conv_dim)), _const_spec((1, LANES)), _const_spec((1, LANES)),
                  _const_spec((1, d_inner)), _const_spec((1, d_inner)), _const_spec((LANES, d_inner)),
                  _const_spec((d_inner, d)), _const_spec((1, d)), _const_spec((1, d))],
        out_specs=tile,
        scratch_shapes=[pltpu.VMEM((conv_dim // LANES, CONV_HALO + ts, LANES), F32), pltpu.VMEM((ts, d_inner), F32),
                        pltpu.VMEM((ts, d_inner), F32), pltpu.VMEM((ts, gn), BF16), pltpu.VMEM((ts, gn), BF16),
                        pltpu.VMEM((ts, d_inner), F32),
                        pltpu.VMEM((SSM_N_GROUPS, SSM_D_STATE, d_inner // SSM_N_GROUPS), F32)],
        compiler_params=_params(2),
        name="ssd_mixer_ln",
    )(h, wz, wx, wb, wc, wdt, conv_w.astype(F32), conv_b.reshape(1, conv_dim), dtb, alog, dexp,
      norm_g.reshape(1, d_inner), expand, w_out.astype(BF16), ln_g.reshape(1, d), ln_b.reshape(1, d))


def _xattn_kernel(h_ref, mem_ref, wq_ref, wk_ref, wv_ref, wo_ref, g_ref, b_ref, wrh_ref, wrl_ref, br_ref,
                  o_ref, route_ref, route_t_ref, cnt_ref, k_scr, v_scr, carry_scr, *, ts, hd):
    b = pl.program_id(0)
    s = pl.program_id(1)

    @pl.when(s == 0)
    def _():
        mb = mem_ref[...].astype(BF16)
        k_scr[...] = _dot(mb, wk_ref[...]).astype(BF16)
        v_scr[...] = _dot(mb, wv_ref[...]).astype(BF16)

    @pl.when((b == 0) & (s == 0))
    def _():
        carry_scr[...] = jnp.zeros(carry_scr.shape, F32)

    sub = ts // XATTN_SPLIT
    halves = []
    for u in range(XATTN_SPLIT):
        rows = slice(u * sub, (u + 1) * sub)
        h = h_ref[rows, :]
        qv = (_dot(h.astype(BF16), wq_ref[...]) * (hd ** -0.5)).astype(BF16)
        heads = []
        for hh in range(XA_HEADS):
            cols = slice(hh * hd, (hh + 1) * hd)
            sc = _dot_nt(qv[:, cols], k_scr[:, cols])
            p = jnp.exp(sc - jnp.max(sc, -1, keepdims=True))
            o = _dot(p.astype(BF16), v_scr[:, cols]) / jnp.sum(p, -1, keepdims=True)
            heads.append(o.astype(BF16))
        halves.append((h, _dot(jnp.concatenate(heads, axis=1), wo_ref[...])))

    lane = lax.broadcasted_iota(jnp.int32, (sub, LANES), 1).astype(F32)
    below = (lax.broadcasted_iota(jnp.int32, (sub, sub), 0) > lax.broadcasted_iota(jnp.int32, (sub, sub), 1))
    below = jnp.where(below, 1.0, 0.0).astype(BF16)
    for u in range(XATTN_SPLIT):
        rows = slice(u * sub, (u + 1) * sub)
        h, xa = halves[u]
        h2 = _layer_norm(DEEPNORM_ALPHA * h + xa, g_ref[...], b_ref[...])
        o_ref[rows, :] = h2
        h2_hi = h2.astype(BF16)

        h2_lo = (h2 - h2_hi.astype(F32)).astype(BF16)
        logits = _dot(h2_hi, wrh_ref[...]) + _dot(h2_lo, wrh_ref[...]) + _dot(h2_hi, wrl_ref[...]) + br_ref[...]
        work = logits
        vals, idxs, sels = [], [], []
        for _ in range(TOP_K):
            m = jnp.max(work, -1, keepdims=True)
            ik = jnp.min(jnp.where(work == m, lane, float(LANES)), -1, keepdims=True)
            sel = lane == ik
            vals.append(m)
            idxs.append(ik)
            sels.append(sel)
            work = jnp.where(sel, -jnp.inf, work)
        exps = [jnp.exp(v - vals[0]) for v in vals]
        den = exps[0]
        for e in exps[1:]:
            den = den + e
        onehot = jnp.zeros((sub, LANES), F32)
        for sel in sels:
            onehot = onehot + sel.astype(F32)
        before = _dot(below, onehot.astype(BF16)) + carry_scr[0:1, :]
        route = jnp.zeros((sub, LANES), F32)
        for k in range(TOP_K):
            rank = jnp.sum(jnp.where(sels[k], before, 0.0), -1, keepdims=True)
            route = jnp.where(lane == float(k), idxs[k], route)
            route = jnp.where(lane == float(TOP_K + k), exps[k] / den, route)
            route = jnp.where(lane == float(2 * TOP_K + k), rank, route)
        route_ref[rows, :] = route
        route_t_ref[:, rows] = route.T[0:ROUTE_ROWS, :]
        carry_scr[...] = carry_scr[...] + jnp.sum(onehot, 0, keepdims=True)
    cnt_ref[...] = carry_scr[...]


def _xattn_layer(h, mem, wq, wk, wv, wo, ln_g, ln_b, w_router, b_router):
    bsz, seq, d = h.shape
    mlen = mem.shape[1]
    ts = XATTN_TILE
    hd = d // XA_HEADS
    n_exp = w_router.shape[1]
    wr = jnp.pad(w_router.astype(F32), ((0, 0), (0, LANES - n_exp)))
    wr_hi = wr.astype(BF16)
    wr_lo = (wr - wr_hi.astype(F32)).astype(BF16)
    br = jnp.pad(b_router.astype(F32), (0, LANES - n_exp), constant_values=-jnp.inf).reshape(1, LANES)
    tile = pl.BlockSpec((None, ts, d), lambda b, s: (b, s, 0))
    rtile = pl.BlockSpec((None, ts, LANES), lambda b, s: (b, s, 0))
    return pl.pallas_call(
        functools.partial(_xattn_kernel, ts=ts, hd=hd),
        out_shape=(jax.ShapeDtypeStruct((bsz, seq, d), F32), jax.ShapeDtypeStruct((bsz, seq, LANES), F32),
                   jax.ShapeDtypeStruct((ROUTE_ROWS, bsz * seq), F32), jax.ShapeDtypeStruct((SUBLANES, LANES), F32)),
        grid=(bsz, seq // ts),
        in_specs=[tile, pl.BlockSpec((None, mlen, d), lambda b, s: (b, 0, 0)),
                  _const_spec((d, d)), _const_spec((d, d)), _const_spec((d, d)), _const_spec((d, d)),
                  _const_spec((1, d)), _const_spec((1, d)), _const_spec((d, LANES)), _const_spec((d, LANES)),
                  _const_spec((1, LANES))],
        out_specs=(tile, rtile, pl.BlockSpec((ROUTE_ROWS, ts), lambda b, s: (0, b * (seq // ts) + s)),
                   pl.BlockSpec((SUBLANES, LANES), lambda b, s: (0, 0))),
        scratch_shapes=[pltpu.VMEM((mlen, d), BF16), pltpu.VMEM((mlen, d), BF16), pltpu.VMEM((SUBLANES, LANES), F32)],
        compiler_params=_params(2),
        name="xattn_ln_router",
    )(h, mem, wq.astype(BF16), wk.astype(BF16), wv.astype(BF16), wo.astype(BF16),
      ln_g.reshape(1, d), ln_b.reshape(1, d), wr_hi, wr_lo, br)


def _rows_from_tiles(ref, n):
    return jnp.concatenate([ref[pl.ds(c, n, stride=SUBLANES), :] for c in range(SUBLANES)], axis=1)


def _rows_to_tiles(ref, v, n):
    for c in range(SUBLANES):
        ref[pl.ds(c, n, stride=SUBLANES), :] = v[:, c * LANES:(c + 1) * LANES]


def _moe_kernel(be_ref, nu_ref, nv_ref, x_ref, wgu_ref, bg_ref, bu_ref, wd_ref, bd_ref, o_ref, wg_scr, wu_scr, wd_scr,
                *, tm):
    i = pl.program_id(0)
    active = i < nu_ref[0]
    new_expert = (i == 0) | (be_ref[i] != be_ref[jnp.maximum(i - 1, 0)])

    @pl.when(active & new_expert)
    def _():
        w2 = 2 * LANES
        src = lax.broadcasted_iota(jnp.int32, (w2, w2), 0)
        dst = lax.broadcasted_iota(jnp.int32, (w2, w2), 1)
        perm = jnp.where(src == jnp.where(dst < LANES, 2 * dst, 2 * (dst - LANES) + 1), 1.0, 0.0).astype(BF16)
        for c in range(wgu_ref.shape[1] // w2):
            res = _dot(wgu_ref[:, c * w2:(c + 1) * w2].astype(BF16), perm)
            wg_scr[:, c * LANES:(c + 1) * LANES] = res[:, :LANES].astype(BF16)
            wu_scr[:, c * LANES:(c + 1) * LANES] = res[:, LANES:].astype(BF16)
        wd_scr[...] = wd_ref[...].astype(BF16)

    nv = nv_ref[i]
    for r in range(MOE_ROW_GROUP, tm + 1, MOE_ROW_GROUP):
        @pl.when(active & (nv > r - MOE_ROW_GROUP) & (nv <= r))
        def _():
            x = _rows_from_tiles(x_ref, r).astype(BF16)
            gate = jnp.minimum(_dot(x, wg_scr[...]) + bg_ref[...], SWIGLU_LIMIT)
            up = jnp.clip(_dot(x, wu_scr[...]) + bu_ref[...], -SWIGLU_LIMIT, SWIGLU_LIMIT)
            act = (up + 1.0) * (gate * _sigmoid(SWIGLU_ALPHA * gate))
            _rows_to_tiles(o_ref, _dot(act.astype(BF16), wd_scr[...]) + bd_ref[...], r)
            if r < tm:
                o_ref[r * SUBLANES:, :] = jnp.zeros(((tm - r) * SUBLANES, LANES), F32)

    @pl.when(jnp.logical_not(active))
    def _():
        o_ref[...] = jnp.zeros(o_ref.shape, F32)


def _moe_experts(xs, block_e, n_used, n_valid, layer, w_gate_up, b_gate, b_up, w_down, b_down):
    _, n_exp, f, d = w_down.shape
    n_rows = xs.shape[0] // SUBLANES
    tm = MOE_TILE
    n_blocks = n_rows // tm

    def row_map(i, be, nu, nv):
        return (jnp.minimum(i, nu[0] - 1), 0)

    def w_map(i, be, nu, nv):
        return (be[i], 0, 0)

    def lw_map(i, be, nu, nv):
        return (layer, be[i], 0, 0)

    grid_spec = pltpu.PrefetchScalarGridSpec(
        num_scalar_prefetch=3,
        grid=(n_blocks,),
        in_specs=[pl.BlockSpec((tm * SUBLANES, LANES), row_map),
                  pl.BlockSpec((None, None, d, 2 * f), lw_map),
                  pl.BlockSpec((None, 1, f), w_map), pl.BlockSpec((None, 1, f), w_map),
                  pl.BlockSpec((None, None, f, d), lw_map), pl.BlockSpec((None, 1, d), w_map)],
        out_specs=pl.BlockSpec((tm * SUBLANES, LANES), lambda i, be, nu, nv: (i, 0)),
        scratch_shapes=[pltpu.VMEM((d, f), BF16), pltpu.VMEM((d, f), BF16), pltpu.VMEM((f, d), BF16)],
    )
    return pl.pallas_call(
        functools.partial(_moe_kernel, tm=tm),
        out_shape=jax.ShapeDtypeStruct((n_rows * SUBLANES, LANES), F32),
        grid_spec=grid_spec,
        compiler_params=_params(1),
        name="moe_experts",
    )(block_e, n_used, n_valid, xs, w_gate_up, b_gate.reshape(n_exp, 1, f), b_up.reshape(n_exp, 1, f),
      w_down, b_down.reshape(n_exp, 1, d))


def _row_copy(src, src_row, dst, dst_row, sem):
    return pltpu.make_async_copy(src.at[pl.ds(pl.multiple_of(src_row * SUBLANES, SUBLANES), SUBLANES)],
                                 dst.at[pl.ds(pl.multiple_of(dst_row * SUBLANES, SUBLANES), SUBLANES)], sem)


def _dispatch_kernel(dest_ref, zblk_ref, h_ref, xs_ref, stage0, stage1, zero_scr, sem, zsem, *, ts, n_tiles, tm):
    i = pl.program_id(0)

    @pl.when(i == 0)
    def _():
        zero_scr[...] = jnp.zeros(zero_scr.shape, F32)
        for j in range(zblk_ref.shape[0]):
            @pl.when(zblk_ref[j] >= 0)
            def _():
                start = pl.multiple_of(zblk_ref[j] * (tm * SUBLANES), tm * SUBLANES)
                pltpu.make_async_copy(zero_scr, xs_ref.at[pl.ds(start, tm * SUBLANES)], zsem).start()
        for j in range(zblk_ref.shape[0]):
            @pl.when(zblk_ref[j] >= 0)
            def _():
                pltpu.make_async_copy(zero_scr, xs_ref.at[pl.ds(0, tm * SUBLANES)], zsem).wait()

    def drain(stage, s):
        for _ in range(TOP_K):
            pltpu.make_async_copy(stage, xs_ref.at[pl.ds(0, ts * SUBLANES)], sem.at[s]).wait()

    def step(stage, s, other, so):
        @pl.when(i >= 2)
        def _():
            drain(stage, s)

        _rows_to_tiles(stage, h_ref[...], ts)

        def issue(j, carry):
            for u in range(ROWS_PER_ISSUE):
                tok = j * ROWS_PER_ISSUE + u
                for k in range(TOP_K):
                    row = dest_ref[k * (n_tiles * ts) + i * ts + tok]
                    _row_copy(stage, tok, xs_ref, row, sem.at[s]).start(priority=k % 2)
            return carry

        lax.fori_loop(0, ts // ROWS_PER_ISSUE, issue, 0)

        @pl.when(i == n_tiles - 1)
        def _():
            drain(other, so)
            drain(stage, s)

    @pl.when(i % 2 == 0)
    def _():
        step(stage0, 0, stage1, 1)

    @pl.when(i % 2 == 1)
    def _():
        step(stage1, 1, stage0, 0)


def _dispatch_rows(h, dest, zero_blocks, n_rows):
    t, d = h.shape
    ts = DISPATCH_TILE
    tm = MOE_TILE
    n_tiles = t // ts
    assert d == SUBLANES * LANES and n_tiles >= 2
    grid_spec = pltpu.PrefetchScalarGridSpec(
        num_scalar_prefetch=2,
        grid=(n_tiles,),
        in_specs=[pl.BlockSpec((ts, d), lambda i, dest, zb: (i, 0))],
        out_specs=pl.BlockSpec(memory_space=pl.ANY),
        scratch_shapes=[pltpu.VMEM((ts * SUBLANES, LANES), F32), pltpu.VMEM((ts * SUBLANES, LANES), F32),
                        pltpu.VMEM((tm * SUBLANES, LANES), F32),
                        pltpu.SemaphoreType.DMA((2,)), pltpu.SemaphoreType.DMA(())],
    )
    return pl.pallas_call(
        functools.partial(_dispatch_kernel, ts=ts, n_tiles=n_tiles, tm=tm),
        out_shape=jax.ShapeDtypeStruct((n_rows * SUBLANES, LANES), F32),
        grid_spec=grid_spec,
        compiler_params=_params(1),
        name="moe_dispatch",
    )(dest, zero_blocks, h)


def _combine_kernel(dest_ref, h_ref, route_ref, g_ref, b_ref, y_ref, o_ref, gbuf0, gbuf1, sem, *, ts, n_tiles):
    i = pl.program_id(0)

    def issue(tile, gbuf, s):
        def body(j, carry):
            for u in range(ROWS_PER_ISSUE):
                tok = j * ROWS_PER_ISSUE + u
                for k in range(TOP_K):
                    row = dest_ref[k * (n_tiles * ts) + tile * ts + tok]
                    _row_copy(y_ref, row, gbuf.at[k], tok, sem.at[s]).start(priority=k % 2)
            return carry

        lax.fori_loop(0, ts // ROWS_PER_ISSUE, body, 0)

    def drain(gbuf, s):
        for k in range(TOP_K):
            pltpu.make_async_copy(y_ref.at[pl.ds(0, ts * SUBLANES)], gbuf.at[k], sem.at[s]).wait()

    def step(gbuf, s, nxt, sn):
        @pl.when(i + 1 < n_tiles)
        def _():
            issue(i + 1, nxt, sn)

        drain(gbuf, s)
        parts = []
        for c in range(SUBLANES):
            acc = None
            for k in range(TOP_K):
                piece = gbuf[k, pl.ds(c, ts, stride=SUBLANES), :] * route_ref[:, TOP_K + k:TOP_K + k + 1]
                acc = piece if acc is None else acc + piece
            parts.append(acc)
        ff = jnp.concatenate(parts, axis=1)
        o_ref[...] = _layer_norm(DEEPNORM_ALPHA * h_ref[...] + ff, g_ref[...], b_ref[...])

    @pl.when(i == 0)
    def _():
        issue(0, gbuf0, 0)

    @pl.when(i % 2 == 0)
    def _():
        step(gbuf0, 0, gbuf1, 1)

    @pl.when(i % 2 == 1)
    def _():
        step(gbuf1, 1, gbuf0, 0)


def _combine_layer(h, y, dest, route, ln_g, ln_b):
    t, d = h.shape
    ts = COMBINE_TILE
    n_tiles = t // ts
    grid_spec = pltpu.PrefetchScalarGridSpec(
        num_scalar_prefetch=1,
        grid=(n_tiles,),
        in_specs=[pl.BlockSpec((ts, d), lambda i, dest: (i, 0)), pl.BlockSpec((ts, LANES), lambda i, dest: (i, 0)),
                  pl.BlockSpec((1, d), lambda i, dest: (0, 0)), pl.BlockSpec((1, d), lambda i, dest: (0, 0)),
                  pl.BlockSpec(memory_space=pl.ANY)],
        out_specs=pl.BlockSpec((ts, d), lambda i, dest: (i, 0)),
        scratch_shapes=[pltpu.VMEM((TOP_K, ts * SUBLANES, LANES), F32),
                        pltpu.VMEM((TOP_K, ts * SUBLANES, LANES), F32), pltpu.SemaphoreType.DMA((2,))],
    )
    return pl.pallas_call(
        functools.partial(_combine_kernel, ts=ts, n_tiles=n_tiles),
        out_shape=jax.ShapeDtypeStruct((t, d), F32),
        grid_spec=grid_spec,
        compiler_params=_params(1),
        name="moe_combine_ln",
    )(dest, h, route, ln_g.reshape(1, d), ln_b.reshape(1, d), y)


def _moe_layer(h2, route, route_t, cnt, layer, w_gate_up, b_gate_up, w_down, b_down, ln_g, ln_b):
    bsz, seq, d = h2.shape
    t = bsz * seq
    tm = MOE_TILE
    n_exp = w_gate_up.shape[1]
    h2 = h2.reshape(t, d)
    route = route.reshape(t, LANES)
    idx = route_t[0:TOP_K].astype(jnp.int32)
    rank = route_t[2 * TOP_K:3 * TOP_K].astype(jnp.int32)
    counts = cnt[0, :n_exp].astype(jnp.int32)
    padded = (counts + tm - 1) // tm * tm
    pend = jnp.cumsum(padded)
    pstart = pend - padded
    part = counts % tm
    skip = jnp.where(part > 0, tm - part, 0)

    def per_expert(table, e):
        return jnp.sum(jnp.where(e[..., None] == jnp.arange(n_exp, dtype=jnp.int32), table, 0), axis=-1)

    dest = (per_expert(pstart, idx) + rank
            + jnp.where(rank >= per_expert(part, idx), per_expert(skip, idx), 0)).reshape(-1)
    n_blocks = (t * TOP_K) // tm + n_exp
    n_used = (pend[-1] // tm).astype(jnp.int32)
    blk = jnp.arange(n_blocks, dtype=jnp.int32)
    block_e = jnp.sum((pend[None, :] <= (blk * tm)[:, None]).astype(jnp.int32), axis=1)
    block_e = jnp.minimum(block_e, n_exp - 1)
    block_e = jnp.where(blk < n_used, block_e, block_e[jnp.maximum(n_used - 1, 0)])
    first_blk = jnp.where(part > 0, pstart // tm, -1)
    tail_blk = n_used + jnp.arange(n_exp, dtype=jnp.int32)
    zero_blocks = jnp.concatenate([first_blk, jnp.where(tail_blk < n_blocks, tail_blk, -1)]).astype(jnp.int32)
    xs = _dispatch_rows(h2, dest, zero_blocks, n_blocks * tm)
    n_valid = jnp.where((blk == per_expert(first_blk, block_e)) & (blk < n_used), per_expert(part, block_e),
                        tm).astype(jnp.int32)
    y = _moe_experts(xs, block_e, n_used.reshape(1), n_valid, layer, w_gate_up, b_gate_up[:, 0::2], b_gate_up[:, 1::2],
                     w_down, b_down)
    return _combine_layer(h2, y, dest, route, ln_g, ln_b).reshape(bsz, seq, d)


def kernel(x, mem, pool_w_in, pool_w_grp, pool_scale, pool_w_out, ssm_w_in, ssm_conv_w, ssm_conv_b, ssm_dt_bias, ssm_a_log, ssm_d, ssm_norm_g, ssm_w_out, xa_wq, xa_wk, xa_wv, xa_wo, moe_w_router, moe_b_router, moe_w_gate_up, moe_b_gate_up, moe_w_down, moe_b_down, ln_mix_g, ln_mix_b, ln_xa_g, ln_xa_b, ln_ffn_g, ln_ffn_b):
    h = x
    for i in range(DEPTH):
        j = i // 2
        if i % 2 == 0:
            h = _pool_layer(h, pool_w_in[j], pool_w_grp[j], pool_scale[j], pool_w_out[j], ln_mix_g[i], ln_mix_b[i])
        else:
            h = _ssd_layer(h, ssm_w_in[j], ssm_conv_w[j], ssm_conv_b[j], ssm_dt_bias[j], ssm_a_log[j], ssm_d[j],
                           ssm_norm_g[j], ssm_w_out[j], ln_mix_g[i], ln_mix_b[i])
        h2, route, route_t, cnt = _xattn_layer(h, mem, xa_wq[i], xa_wk[i], xa_wv[i], xa_wo[i], ln_xa_g[i], ln_xa_b[i],
                                      moe_w_router[i], moe_b_router[i])
        h = _moe_layer(h2, route, route_t, cnt, i, moe_w_gate_up, moe_b_gate_up[i], moe_w_down, moe_b_down[i],
                       ln_ffn_g[i], ln_ffn_b[i])
    return h
```

```python
import functools

import jax
import jax.numpy as jnp
from jax import lax
from jax.experimental import pallas as pl
from jax.experimental.pallas import tpu as pltpu

F32 = jnp.float32
BF16 = jnp.bfloat16

DEPTH = 2
DEEPNORM_ALPHA = (2 * DEPTH) ** 0.25
LN_EPS = 1e-5
POOL_WINDOWS = (2, 4, 8, 16)
POOL_HALO = 16
SSM_HEAD_DIM = 64
SSM_N_GROUPS = 8
SSM_HEADS_PER_GROUP = 4
SSM_D_STATE = 128
SSM_CONV = 4
CONV_HALO = 8
SSM_NORM_EPS = 1e-5
XA_HEADS = 4
N_EXPERTS = 32
TOP_K = 4
SWIGLU_LIMIT = 7.0
SWIGLU_ALPHA = 1.702

LANES = 128
SUBLANES = 8
VMEM_LIMIT_BYTES = 56 * 1024 * 1024

POOL_TILE = 512
XATTN_TILE = 1024
XATTN_SPLIT = 2
SSD_TILE = 256
SSD_CHUNK = 128
SSD_STRIP = 512
MOE_TILE = 1024
MOE_ROW_GROUP = 256
DISPATCH_TILE = 512
COMBINE_TILE = 512
ROWS_PER_ISSUE = 8

ROUTE_ROWS = 16

NEG_BIG = -1e30


def _layer_norm(v, g, b):
    mu = jnp.mean(v, -1, keepdims=True)
    d = v - mu
    var = jnp.mean(d * d, -1, keepdims=True)
    return d * lax.rsqrt(var + LN_EPS) * g + b


def _dot(a, b):
    return jnp.dot(a, b, preferred_element_type=F32)


def _dot_nt(a, b):
    return lax.dot_general(a, b, (((1,), (1,)), ((), ())), preferred_element_type=F32)


def _dot_tn(a, b):
    return lax.dot_general(a, b, (((0,), (0,)), ((), ())), preferred_element_type=F32)


def _const_spec(shape):
    nd = len(shape)
    return pl.BlockSpec(shape, lambda *_: (0,) * nd, pipeline_mode=pl.Buffered(1))


def _params(n_axes):
    return pltpu.CompilerParams(dimension_semantics=("arbitrary",) * n_axes,
                                vmem_limit_bytes=VMEM_LIMIT_BYTES)


def _pool_kernel(x_ref, win_ref, wgrp_ref, scale_ref, wout_ref, g_ref, b_ref, o_ref, ext_ref, *, ts, gd):
    s = pl.program_id(1)

    @pl.when(s == 0)
    def _():
        ext_ref[0:POOL_HALO, :] = jnp.zeros((POOL_HALO, ext_ref.shape[1]), F32)

    x = x_ref[...]
    ext_ref[POOL_HALO:, :] = _dot(x.astype(BF16), win_ref[...])
    pos = lax.broadcasted_iota(jnp.int32, (ts, 1), 0) + s * ts
    mix = None
    for g, w in enumerate(POOL_WINDOWS):
        cols = slice(g * gd, (g + 1) * gd)
        e = ext_ref[:, cols]
        acc = e
        sh = 1
        while sh < w:
            acc = acc + pltpu.roll(acc, sh, axis=0)
            sh *= 2
        cnt = jnp.minimum(pos + 1, w).astype(F32)
        m = acc[POOL_HALO:, :] / cnt - e[POOL_HALO:, :]
        yg = _dot(m.astype(BF16), wgrp_ref[g]) * scale_ref[:, cols]
        part = _dot(yg.astype(BF16), wout_ref[cols, :])
        mix = part if mix is None else mix + part
    ext_ref[0:POOL_HALO, :] = ext_ref[ts:ts + POOL_HALO, :]
    o_ref[...] = _layer_norm(DEEPNORM_ALPHA * x + mix, g_ref[...], b_ref[...])


def _pool_layer(h, w_in, w_grp, scale, w_out, ln_g, ln_b):
    bsz, seq, d = h.shape
    ts = POOL_TILE
    gd = d // len(POOL_WINDOWS)
    tile = pl.BlockSpec((None, ts, d), lambda b, s: (b, s, 0))
    return pl.pallas_call(
        functools.partial(_pool_kernel, ts=ts, gd=gd),
        out_shape=jax.ShapeDtypeStruct((bsz, seq, d), F32),
        grid=(bsz, seq // ts),
        in_specs=[tile, _const_spec((d, d)), _const_spec((len(POOL_WINDOWS), gd, gd)), _const_spec((1, d)),
                  _const_spec((d, d)), _const_spec((1, d)), _const_spec((1, d))],
        out_specs=tile,
        scratch_shapes=[pltpu.VMEM((POOL_HALO + ts, d), F32)],
        compiler_params=_params(2),
        name="pool_mixer_ln",
    )(h, w_in.astype(BF16), w_grp.astype(BF16), scale.reshape(1, d), w_out.astype(BF16),
      ln_g.reshape(1, d), ln_b.reshape(1, d))


def _sigmoid(v):
    return 1.0 / (1.0 + jnp.exp(-v))


def _ssd_kernel(x_ref, wz_ref, wx_ref, wb_ref, wc_ref, wdt_ref, convw_ref, convb_ref, dtb_ref, alog_ref,
                dexp_ref, normg_ref, expand_ref, wout_ref, g_ref, b_ref, o_ref,
                xbc_scr, z_scr, actx_scr, bmat_scr, cmat_scr, y_scr, state_scr, *, ts, q, d_inner, gn):
    s = pl.program_id(1)
    conv_dim = d_inner + 2 * gn
    gw = d_inner // SSM_N_GROUPS

    @pl.when(s == 0)
    def _():
        xbc_scr[:, 0:CONV_HALO, :] = jnp.zeros((conv_dim // LANES, CONV_HALO, LANES), F32)
        state_scr[...] = jnp.zeros(state_scr.shape, F32)

    x = x_ref[...]
    xb = x.astype(BF16)
    dt_raw = _dot(xb, wdt_ref[...]) + dtb_ref[...]
    dtv = jnp.maximum(dt_raw, 0.0) + jnp.log1p(jnp.exp(-jnp.abs(dt_raw)))
    a_all = dtv * (-jnp.exp(alog_ref[...]))

    for w_ref, base in ((wb_ref, d_inner), (wc_ref, d_inner + gn), (wx_ref, 0)):
        for c in range(0, w_ref.shape[1], SSD_STRIP):
            res = _dot(xb, w_ref[:, c:c + SSD_STRIP])
            for jj in range(SSD_STRIP // LANES):
                j = (base + c) // LANES + jj
                cs = slice(j * LANES, (j + 1) * LANES)
                xbc_scr[j, CONV_HALO:, :] = res[:, jj * LANES:(jj + 1) * LANES]
                acc = convb_ref[:, cs]
                for k in range(SSM_CONV):
                    r0 = CONV_HALO - (SSM_CONV - 1) + k
                    acc = acc + convw_ref[k:k + 1, cs] * xbc_scr[j, r0:r0 + ts, :]
                act = acc * _sigmoid(acc)
                col = j * LANES
                if col < d_inner:
                    actx_scr[:, cs] = act
                elif col < d_inner + gn:
                    bmat_scr[:, col - d_inner:col - d_inner + LANES] = act.astype(BF16)
                else:
                    cmat_scr[:, col - d_inner - gn:col - d_inner - gn + LANES] = act.astype(BF16)
                xbc_scr[j, 0:CONV_HALO, :] = xbc_scr[j, ts:ts + CONV_HALO, :]

    expand = expand_ref[...]
    row_i = lax.broadcasted_iota(jnp.int32, (q, LANES), 0)
    causal = lax.broadcasted_iota(jnp.int32, (q, q), 0) >= lax.broadcasted_iota(jnp.int32, (q, q), 1)
    head_of_lane = lax.broadcasted_iota(jnp.int32, (q, gw), 1) // SSM_HEAD_DIM
    head_mask = [jnp.where(head_of_lane == r, 1.0, 0.0).astype(BF16) for r in range(SSM_HEADS_PER_GROUP)]

    for c in range(ts // q):
        rows = slice(c * q, (c + 1) * q)
        acs = a_all[rows, :]
        sh = 1
        while sh < q:
            acs = acs + jnp.where(row_i >= sh, pltpu.roll(acs, sh, axis=0), 0.0)
            sh *= 2
        acs_t = acs.T
        a_last = acs[q - 1:q, :]
        e_in = _dot(jnp.exp(acs).astype(BF16), expand)
        dec = _dot(jnp.exp(a_last - acs).astype(BF16), expand)
        dtx = _dot(dtv[rows, :].astype(BF16), expand)
        cd = jnp.broadcast_to(jnp.exp(a_last), (SUBLANES, LANES))
        cd_hi = cd.astype(BF16)
        cd_lo = (cd - cd_hi.astype(F32)).astype(BF16)
        cdx = (_dot(cd_hi, expand) + _dot(cd_lo, expand))[0:1, :]
        xd = actx_scr[rows, :] * dtx
        xdd = xd * dec
        xdb = xd.astype(BF16)
        for g in range(SSM_N_GROUPS):
            gc = slice(g * gw, (g + 1) * gw)
            nc = slice(g * SSM_D_STATE, (g + 1) * SSM_D_STATE)
            if c == 0:
                z_scr[:, gc] = _dot(xb, wz_ref[:, gc])
            bg = bmat_scr[rows, nc]
            cg = cmat_scr[rows, nc]
            cb = _dot_nt(cg, bg)
            xg = xdb[:, gc]
            mhs, xms = [], []
            for r in range(SSM_HEADS_PER_GROUP):
                hd = g * SSM_HEADS_PER_GROUP + r
                seg = acs[:, hd:hd + 1] - acs_t[hd:hd + 1, :]
                lmat = jnp.exp(jnp.where(causal, seg, NEG_BIG))
                mhs.append((cb * lmat).astype(BF16))
                xms.append(xg * head_mask[r])
            yg = (_dot(cg, state_scr[g].astype(BF16)) * e_in[:, gc]
                  + _dot(jnp.concatenate(mhs, axis=1), jnp.concatenate(xms, axis=0)))
            y_scr[rows, gc] = yg
            state_scr[g] = state_scr[g] * cdx[:, gc] + _dot_tn(bg, xdd[:, gc].astype(BF16))

    mix = None
    for g in range(SSM_N_GROUPS):
        gc = slice(g * gw, (g + 1) * gw)
        zz = z_scr[:, gc]
        yv = (y_scr[:, gc] + dexp_ref[:, gc] * actx_scr[:, gc]) * (zz * _sigmoid(zz))
        yv = yv * lax.rsqrt(jnp.mean(yv * yv, -1, keepdims=True) + SSM_NORM_EPS) * normg_ref[:, gc]
        part = _dot(yv.astype(BF16), wout_ref[gc, :])
        mix = part if mix is None else mix + part
    o_ref[...] = _layer_norm(DEEPNORM_ALPHA * x + mix, g_ref[...], b_ref[...])


def _ssd_layer(h, w_in, conv_w, conv_b, dt_bias, a_log, d_skip, norm_g, w_out, ln_g, ln_b):
    bsz, seq, d = h.shape
    n_heads = a_log.shape[0]
    d_inner = n_heads * SSM_HEAD_DIM
    gn = SSM_N_GROUPS * SSM_D_STATE
    conv_dim = d_inner + 2 * gn
    ts, q = SSD_TILE, SSD_CHUNK
    w_in = w_in.astype(BF16)
    wz = w_in[:, :d_inner]
    wx = w_in[:, d_inner:2 * d_inner]
    wb = w_in[:, 2 * d_inner:2 * d_inner + gn]
    wc = w_in[:, 2 * d_inner + gn:2 * d_inner + 2 * gn]
    pad = LANES - n_heads
    wdt = jnp.pad(w_in[:, d_inner + conv_dim:], ((0, 0), (0, pad)))
    dtb = jnp.pad(dt_bias.astype(F32), (0, pad)).reshape(1, LANES)
    alog = jnp.pad(a_log.astype(F32), (0, pad)).reshape(1, LANES)
    dexp = jnp.repeat(d_skip.astype(F32), SSM_HEAD_DIM).reshape(1, d_inner)
    expand = (jnp.arange(LANES)[:, None] == (jnp.arange(d_inner)[None, :] // SSM_HEAD_DIM)).astype(BF16)
    tile = pl.BlockSpec((None, ts, d), lambda b, s: (b, s, 0))
    return pl.pallas_call(
        functools.partial(_ssd_kernel, ts=ts, q=q, d_inner=d_inner, gn=gn),
        out_shape=jax.ShapeDtypeStruct((bsz, seq, d), F32),
        grid=(bsz, seq // ts),
        in_specs=[tile, _const_spec((d, d_inner)), _const_spec((d, d_inner)), _const_spec((d, gn)),
                  _const_spec((d, gn)), _const_spec((d, LANES)), _const_spec((SSM_CONV, conv_dim)),
                  _const_spec((1, conv_dim)), _const_spec((1, LANES)), _const_spec((1, LANES)),
                  _const_spec((1, d_inner)), _const_spec((1, d_inner)), _const_spec((LANES, d_inner)),
                  _const_spec((d_inner, d)), _const_spec((1, d)), _const_spec((1, d))],
        out_specs=tile,
        scratch_shapes=[pltpu.VMEM((conv_dim // LANES, CONV_HALO + ts, LANES), F32), pltpu.VMEM((ts, d_inner), F32),
                        pltpu.VMEM((ts, d_inner), F32), pltpu.VMEM((ts, gn), BF16), pltpu.VMEM((ts, gn), BF16),
                        pltpu.VMEM((ts, d_inner), F32),
                        pltpu.VMEM((SSM_N_GROUPS, SSM_D_STATE, d_inner // SSM_N_GROUPS), F32)],
        compiler_params=_params(2),
        name="ssd_mixer_ln",
    )(h, wz, wx, wb, wc, wdt, conv_w.astype(F32), conv_b.reshape(1, conv_dim), dtb, alog, dexp,
      norm_g.reshape(1, d_inner), expand, w_out.astype(BF16), ln_g.reshape(1, d), ln_b.reshape(1, d))


def _xattn_kernel(h_ref, mem_ref, wq_ref, wk_ref, wv_ref, wo_ref, g_ref, b_ref, wrh_ref, wrl_ref, br_ref,
                  o_ref, route_ref, route_t_ref, cnt_ref, k_scr, v_scr, carry_scr, *, ts, hd):
    b = pl.program_id(0)
    s = pl.program_id(1)

    @pl.when(s == 0)
    def _():
        mb = mem_ref[...].astype(BF16)
        k_scr[...] = _dot(mb, wk_ref[...]).astype(BF16)
        v_scr[...] = _dot(mb, wv_ref[...]).astype(BF16)

    @pl.when((b == 0) & (s == 0))
    def _():
        carry_scr[...] = jnp.zeros(carry_scr.shape, F32)

    sub = ts // XATTN_SPLIT
    halves = []
    for u in range(XATTN_SPLIT):
        rows = slice(u * sub, (u + 1) * sub)
        h = h_ref[rows, :]
        qv = (_dot(h.astype(BF16), wq_ref[...]) * (hd ** -0.5)).astype(BF16)
        heads = []
        for hh in range(XA_HEADS):
            cols = slice(hh * hd, (hh + 1) * hd)
            sc = _dot_nt(qv[:, cols], k_scr[:, cols])
            p = jnp.exp(sc - jnp.max(sc, -1, keepdims=True))
            o = _dot(p.astype(BF16), v_scr[:, cols]) / jnp.sum(p, -1, keepdims=True)
            heads.append(o.astype(BF16))
        halves.append((h, _dot(jnp.concatenate(heads, axis=1), wo_ref[...])))

    lane = lax.broadcasted_iota(jnp.int32, (sub, LANES), 1).astype(F32)
    below = (lax.broadcasted_iota(jnp.int32, (sub, sub), 0) > lax.broadcasted_iota(jnp.int32, (sub, sub), 1))
    below = jnp.where(below, 1.0, 0.0).astype(BF16)
    for u in range(XATTN_SPLIT):
        rows = slice(u * sub, (u + 1) * sub)
        h, xa = halves[u]
        h2 = _layer_norm(DEEPNORM_ALPHA * h + xa, g_ref[...], b_ref[...])
        o_ref[rows, :] = h2
        h2_hi = h2.astype(BF16)

        h2_lo = (h2 - h2_hi.astype(F32)).astype(BF16)
        logits = _dot(h2_hi, wrh_ref[...]) + _dot(h2_lo, wrh_ref[...]) + _dot(h2_hi, wrl_ref[...]) + br_ref[...]
        work = logits
        vals, idxs, sels = [], [], []
        for _ in range(TOP_K):
            m = jnp.max(work, -1, keepdims=True)
            ik = jnp.min(jnp.where(work == m, lane, float(LANES)), -1, keepdims=True)
            sel = lane == ik
            vals.append(m)
            idxs.append(ik)
            sels.append(sel)
            work = jnp.where(sel, -jnp.inf, work)
        exps = [jnp.exp(v - vals[0]) for v in vals]
        den = exps[0]
        for e in exps[1:]:
            den = den + e
        onehot = jnp.zeros((sub, LANES), F32)
        for sel in sels:
            onehot = onehot + sel.astype(F32)
        before = _dot(below, onehot.astype(BF16)) + carry_scr[0:1, :]
        route = jnp.zeros((sub, LANES), F32)
        for k in range(TOP_K):
            rank = jnp.sum(jnp.where(sels[k], before, 0.0), -1, keepdims=True)
            route = jnp.where(lane == float(k), idxs[k], route)
            route = jnp.where(lane == float(TOP_K + k), exps[k] / den, route)
            route = jnp.where(lane == float(2 * TOP_K + k), rank, route)
        route_ref[rows, :] = route
        route_t_ref[:, rows] = route.T[0:ROUTE_ROWS, :]
        carry_scr[...] = carry_scr[...] + jnp.sum(onehot, 0, keepdims=True)
    cnt_ref[...] = carry_scr[...]


def _xattn_layer(h, mem, wq, wk, wv, wo, ln_g, ln_b, w_router, b_router):
    bsz, seq, d = h.shape
    mlen = mem.shape[1]
    ts = XATTN_TILE
    hd = d // XA_HEADS
    n_exp = w_router.shape[1]
    wr = jnp.pad(w_router.astype(F32), ((0, 0), (0, LANES - n_exp)))
    wr_hi = wr.astype(BF16)
    wr_lo = (wr - wr_hi.astype(F32)).astype(BF16)
    br = jnp.pad(b_router.astype(F32), (0, LANES - n_exp), constant_values=-jnp.inf).reshape(1, LANES)
    tile = pl.BlockSpec((None, ts, d), lambda b, s: (b, s, 0))
    rtile = pl.BlockSpec((None, ts, LANES), lambda b, s: (b, s, 0))
    return pl.pallas_call(
        functools.partial(_xattn_kernel, ts=ts, hd=hd),
        out_shape=(jax.ShapeDtypeStruct((bsz, seq, d), F32), jax.ShapeDtypeStruct((bsz, seq, LANES), F32),
                   jax.ShapeDtypeStruct((ROUTE_ROWS, bsz * seq), F32), jax.ShapeDtypeStruct((SUBLANES, LANES), F32)),
        grid=(bsz, seq // ts),
        in_specs=[tile, pl.BlockSpec((None, mlen, d), lambda b, s: (b, 0, 0)),
                  _const_spec((d, d)), _const_spec((d, d)), _const_spec((d, d)), _const_spec((d, d)),
                  _const_spec((1, d)), _const_spec((1, d)), _const_spec((d, LANES)), _const_spec((d, LANES)),
                  _const_spec((1, LANES))],
        out_specs=(tile, rtile, pl.BlockSpec((ROUTE_ROWS, ts), lambda b, s: (0, b * (seq // ts) + s)),
                   pl.BlockSpec((SUBLANES, LANES), lambda b, s: (0, 0))),
        scratch_shapes=[pltpu.VMEM((mlen, d), BF16), pltpu.VMEM((mlen, d), BF16), pltpu.VMEM((SUBLANES, LANES), F32)],
        compiler_params=_params(2),
        name="xattn_ln_router",
    )(h, mem, wq.astype(BF16), wk.astype(BF16), wv.astype(BF16), wo.astype(BF16),
      ln_g.reshape(1, d), ln_b.reshape(1, d), wr_hi, wr_lo, br)


def _rows_from_tiles(ref, n):
    return jnp.concatenate([ref[pl.ds(c, n, stride=SUBLANES), :] for c in range(SUBLANES)], axis=1)


def _rows_to_tiles(ref, v, n):
    for c in range(SUBLANES):
        ref[pl.ds(c, n, stride=SUBLANES), :] = v[:, c * LANES:(c + 1) * LANES]


def _moe_kernel(be_ref, nu_ref, nv_ref, x_ref, wgu_ref, bg_ref, bu_ref, wd_ref, bd_ref, o_ref, wg_scr, wu_scr, wd_scr,
                *, tm):
    i = pl.program_id(0)
    active = i < nu_ref[0]
    new_expert = (i == 0) | (be_ref[i] != be_ref[jnp.maximum(i - 1, 0)])

    @pl.when(active & new_expert)
    def _():
        w2 = 2 * LANES
        src = lax.broadcasted_iota(jnp.int32, (w2, w2), 0)
        dst = lax.broadcasted_iota(jnp.int32, (w2, w2), 1)
        perm = jnp.where(src == jnp.where(dst < LANES, 2 * dst, 2 * (dst - LANES) + 1), 1.0, 0.0).astype(BF16)
        for c in range(wgu_ref.shape[1] // w2):
            res = _dot(wgu_ref[:, c * w2:(c + 1) * w2].astype(BF16), perm)
            wg_scr[:, c * LANES:(c + 1) * LANES] = res[:, :LANES].astype(BF16)
            wu_scr[:, c * LANES:(c + 1) * LANES] = res[:, LANES:].astype(BF16)
        wd_scr[...] = wd_ref[...].astype(BF16)

    nv = nv_ref[i]
    for r in range(MOE_ROW_GROUP, tm + 1, MOE_ROW_GROUP):
        @pl.when(active & (nv > r - MOE_ROW_GROUP) & (nv <= r))
        def _():
            x = _rows_from_tiles(x_ref, r).astype(BF16)
            gate = jnp.minimum(_dot(x, wg_scr[...]) + bg_ref[...], SWIGLU_LIMIT)
            up = jnp.clip(_dot(x, wu_scr[...]) + bu_ref[...], -SWIGLU_LIMIT, SWIGLU_LIMIT)
            act = (up + 1.0) * (gate * _sigmoid(SWIGLU_ALPHA * gate))
            _rows_to_tiles(o_ref, _dot(act.astype(BF16), wd_scr[...]) + bd_ref[...], r)
            if r < tm:
                o_ref[r * SUBLANES:, :] = jnp.zeros(((tm - r) * SUBLANES, LANES), F32)

    @pl.when(jnp.logical_not(active))
    def _():
        o_ref[...] = jnp.zeros(o_ref.shape, F32)


def _moe_experts(xs, block_e, n_used, n_valid, layer, w_gate_up, b_gate, b_up, w_down, b_down):
    _, n_exp, f, d = w_down.shape
    n_rows = xs.shape[0] // SUBLANES
    tm = MOE_TILE
    n_blocks = n_rows // tm

    def row_map(i, be, nu, nv):
        return (jnp.minimum(i, nu[0] - 1), 0)

    def w_map(i, be, nu, nv):
        return (be[i], 0, 0)

    def lw_map(i, be, nu, nv):
        return (layer, be[i], 0, 0)

    grid_spec = pltpu.PrefetchScalarGridSpec(
        num_scalar_prefetch=3,
        grid=(n_blocks,),
        in_specs=[pl.BlockSpec((tm * SUBLANES, LANES), row_map),
                  pl.BlockSpec((None, None, d, 2 * f), lw_map),
                  pl.BlockSpec((None, 1, f), w_map), pl.BlockSpec((None, 1, f), w_map),
                  pl.BlockSpec((None, None, f, d), lw_map), pl.BlockSpec((None, 1, d), w_map)],
        out_specs=pl.BlockSpec((tm * SUBLANES, LANES), lambda i, be, nu, nv: (i, 0)),
        scratch_shapes=[pltpu.VMEM((d, f), BF16), pltpu.VMEM((d, f), BF16), pltpu.VMEM((f, d), BF16)],
    )
    return pl.pallas_call(
        functools.partial(_moe_kernel, tm=tm),
        out_shape=jax.ShapeDtypeStruct((n_rows * SUBLANES, LANES), F32),
        grid_spec=grid_spec,
        compiler_params=_params(1),
        name="moe_experts",
    )(block_e, n_used, n_valid, xs, w_gate_up, b_gate.reshape(n_exp, 1, f), b_up.reshape(n_exp, 1, f),
      w_down, b_down.reshape(n_exp, 1, d))


def _row_copy(src, src_row, dst, dst_row, sem):
    return pltpu.make_async_copy(src.at[pl.ds(pl.multiple_of(src_row * SUBLANES, SUBLANES), SUBLANES)],
                                 dst.at[pl.ds(pl.multiple_of(dst_row * SUBLANES, SUBLANES), SUBLANES)], sem)


def _dispatch_kernel(dest_ref, zblk_ref, h_ref, xs_ref, stage0, stage1, zero_scr, sem, zsem, *, ts, n_tiles, tm):
    i = pl.program_id(0)

    @pl.when(i == 0)
    def _():
        zero_scr[...] = jnp.zeros(zero_scr.shape, F32)
        for j in range(zblk_ref.shape[0]):
            @pl.when(zblk_ref[j] >= 0)
            def _():
                start = pl.multiple_of(zblk_ref[j] * (tm * SUBLANES), tm * SUBLANES)
                pltpu.make_async_copy(zero_scr, xs_ref.at[pl.ds(start, tm * SUBLANES)], zsem).start()
        for j in range(zblk_ref.shape[0]):
            @pl.when(zblk_ref[j] >= 0)
            def _():
                pltpu.make_async_copy(zero_scr, xs_ref.at[pl.ds(0, tm * SUBLANES)], zsem).wait()

    def drain(stage, s):
        for _ in range(TOP_K):
            pltpu.make_async_copy(stage, xs_ref.at[pl.ds(0, ts * SUBLANES)], sem.at[s]).wait()

    def step(stage, s, other, so):
        @pl.when(i >= 2)
        def _():
            drain(stage, s)

        _rows_to_tiles(stage, h_ref[...], ts)

        def issue(j, carry):
            for u in range(ROWS_PER_ISSUE):
                tok = j * ROWS_PER_ISSUE + u
                for k in range(TOP_K):
                    row = dest_ref[k * (n_tiles * ts) + i * ts + tok]
                    _row_copy(stage, tok, xs_ref, row, sem.at[s]).start(priority=k % 2)
            return carry

        lax.fori_loop(0, ts // ROWS_PER_ISSUE, issue, 0)

        @pl.when(i == n_tiles - 1)
        def _():
            drain(other, so)
            drain(stage, s)

    @pl.when(i % 2 == 0)
    def _():
        step(stage0, 0, stage1, 1)

    @pl.when(i % 2 == 1)
    def _():
        step(stage1, 1, stage0, 0)


def _dispatch_rows(h, dest, zero_blocks, n_rows):
    t, d = h.shape
    ts = DISPATCH_TILE
    tm = MOE_TILE
    n_tiles = t // ts
    assert d == SUBLANES * LANES and n_tiles >= 2
    grid_spec = pltpu.PrefetchScalarGridSpec(
        num_scalar_prefetch=2,
        grid=(n_tiles,),
        in_specs=[pl.BlockSpec((ts, d), lambda i, dest, zb: (i, 0))],
        out_specs=pl.BlockSpec(memory_space=pl.ANY),
        scratch_shapes=[pltpu.VMEM((ts * SUBLANES, LANES), F32), pltpu.VMEM((ts * SUBLANES, LANES), F32),
                        pltpu.VMEM((tm * SUBLANES, LANES), F32),
                        pltpu.SemaphoreType.DMA((2,)), pltpu.SemaphoreType.DMA(())],
    )
    return pl.pallas_call(
        functools.partial(_dispatch_kernel, ts=ts, n_tiles=n_tiles, tm=tm),
        out_shape=jax.ShapeDtypeStruct((n_rows * SUBLANES, LANES), F32),
        grid_spec=grid_spec,
        compiler_params=_params(1),
        name="moe_dispatch",
    )(dest, zero_blocks, h)


def _combine_kernel(dest_ref, h_ref, route_ref, g_ref, b_ref, y_ref, o_ref, gbuf0, gbuf1, sem, *, ts, n_tiles):
    i = pl.program_id(0)

    def issue(tile, gbuf, s):
        def body(j, carry):
            for u in range(ROWS_PER_ISSUE):
                tok = j * ROWS_PER_ISSUE + u
                for k in range(TOP_K):
                    row = dest_ref[k * (n_tiles * ts) + tile * ts + tok]
                    _row_copy(y_ref, row, gbuf.at[k], tok, sem.at[s]).start(priority=k % 2)
            return carry

        lax.fori_loop(0, ts // ROWS_PER_ISSUE, body, 0)

    def drain(gbuf, s):
        for k in range(TOP_K):
            pltpu.make_async_copy(y_ref.at[pl.ds(0, ts * SUBLANES)], gbuf.at[k], sem.at[s]).wait()

    def step(gbuf, s, nxt, sn):
        @pl.when(i + 1 < n_tiles)
        def _():
            issue(i + 1, nxt, sn)

        drain(gbuf, s)
        parts = []
        for c in range(SUBLANES):
            acc = None
            for k in range(TOP_K):
                piece = gbuf[k, pl.ds(c, ts, stride=SUBLANES), :] * route_ref[:, TOP_K + k:TOP_K + k + 1]
                acc = piece if acc is None else acc + piece
            parts.append(acc)
        ff = jnp.concatenate(parts, axis=1)
        o_ref[...] = _layer_norm(DEEPNORM_ALPHA * h_ref[...] + ff, g_ref[...], b_ref[...])

    @pl.when(i == 0)
    def _():
        issue(0, gbuf0, 0)

    @pl.when(i % 2 == 0)
    def _():
        step(gbuf0, 0, gbuf1, 1)

    @pl.when(i % 2 == 1)
    def _():
        step(gbuf1, 1, gbuf0, 0)


def _combine_layer(h, y, dest, route, ln_g, ln_b):
    t, d = h.shape
    ts = COMBINE_TILE
    n_tiles = t // ts
    grid_spec = pltpu.PrefetchScalarGridSpec(
        num_scalar_prefetch=1,
        grid=(n_tiles,),
        in_specs=[pl.BlockSpec((ts, d), lambda i, dest: (i, 0)), pl.BlockSpec((ts, LANES), lambda i, dest: (i, 0)),
                  pl.BlockSpec((1, d), lambda i, dest: (0, 0)), pl.BlockSpec((1, d), lambda i, dest: (0, 0)),
                  pl.BlockSpec(memory_space=pl.ANY)],
        out_specs=pl.BlockSpec((ts, d), lambda i, dest: (i, 0)),
        scratch_shapes=[pltpu.VMEM((TOP_K, ts * SUBLANES, LANES), F32),
                        pltpu.VMEM((TOP_K, ts * SUBLANES, LANES), F32), pltpu.SemaphoreType.DMA((2,))],
    )
    return pl.pallas_call(
        functools.partial(_combine_kernel, ts=ts, n_tiles=n_tiles),
        out_shape=jax.ShapeDtypeStruct((t, d), F32),
        grid_spec=grid_spec,
        compiler_params=_params(1),
        name="moe_combine_ln",
    )(dest, h, route, ln_g.reshape(1, d), ln_b.reshape(1, d), y)


def _moe_layer(h2, route, route_t, cnt, layer, w_gate_up, b_gate_up, w_down, b_down, ln_g, ln_b):
    bsz, seq, d = h2.shape
    t = bsz * seq
    tm = MOE_TILE
    n_exp = w_gate_up.shape[1]
    h2 = h2.reshape(t, d)
    route = route.reshape(t, LANES)
    idx = route_t[0:TOP_K].astype(jnp.int32)
    rank = route_t[2 * TOP_K:3 * TOP_K].astype(jnp.int32)
    counts = cnt[0, :n_exp].astype(jnp.int32)
    padded = (counts + tm - 1) // tm * tm
    pend = jnp.cumsum(padded)
    pstart = pend - padded
    part = counts % tm
    skip = jnp.where(part > 0, tm - part, 0)

    def per_expert(table, e):
        return jnp.sum(jnp.where(e[..., None] == jnp.arange(n_exp, dtype=jnp.int32), table, 0), axis=-1)

    dest = (per_expert(pstart, idx) + rank
            + jnp.where(rank >= per_expert(part, idx), per_expert(skip, idx), 0)).reshape(-1)
    n_blocks = (t * TOP_K) // tm + n_exp
    n_used = (pend[-1] // tm).astype(jnp.int32)
    blk = jnp.arange(n_blocks, dtype=jnp.int32)
    block_e = jnp.sum((pend[None, :] <= (blk * tm)[:, None]).astype(jnp.int32), axis=1)
    block_e = jnp.minimum(block_e, n_exp - 1)
    block_e = jnp.where(blk < n_used, block_e, block_e[jnp.maximum(n_used - 1, 0)])
    first_blk = jnp.where(part > 0, pstart // tm, -1)
    tail_blk = n_used + jnp.arange(n_exp, dtype=jnp.int32)
    zero_blocks = jnp.concatenate([first_blk, jnp.where(tail_blk < n_blocks, tail_blk, -1)]).astype(jnp.int32)
    xs = _dispatch_rows(h2, dest, zero_blocks, n_blocks * tm)
    n_valid = jnp.where((blk == per_expert(first_blk, block_e)) & (blk < n_used), per_expert(part, block_e),
                        tm).astype(jnp.int32)
    y = _moe_experts(xs, block_e, n_used.reshape(1), n_valid, layer, w_gate_up, b_gate_up[:, 0::2], b_gate_up[:, 1::2],
                     w_down, b_down)
    return _combine_layer(h2, y, dest, route, ln_g, ln_b).reshape(bsz, seq, d)


def kernel(x, mem, pool_w_in, pool_w_grp, pool_scale, pool_w_out, ssm_w_in, ssm_conv_w, ssm_conv_b, ssm_dt_bias, ssm_a_log, ssm_d, ssm_norm_g, ssm_w_out, xa_wq, xa_wk, xa_wv, xa_wo, moe_w_router, moe_b_router, moe_w_gate_up, moe_b_gate_up, moe_w_down, moe_b_down, ln_mix_g, ln_mix_b, ln_xa_g, ln_xa_b, ln_ffn_g, ln_ffn_b):
    h = x
    for i in range(DEPTH):
        j = i // 2
        if i % 2 == 0:
            h = _pool_layer(h, pool_w_in[j], pool_w_grp[j], pool_scale[j], pool_w_out[j], ln_mix_g[i], ln_mix_b[i])
        else:
            h = _ssd_layer(h, ssm_w_in[j], ssm_conv_w[j], ssm_conv_b[j], ssm_dt_bias[j], ssm_a_log[j], ssm_d[j],
                           ssm_norm_g[j], ssm_w_out[j], ln_mix_g[i], ln_mix_b[i])
        h2, route, route_t, cnt = _xattn_layer(h, mem, xa_wq[i], xa_wk[i], xa_wv[i], xa_wo[i], ln_xa_g[i], ln_xa_b[i],
                                      moe_w_router[i], moe_b_router[i])
        h = _moe_layer(h2, route, route_t, cnt, i, moe_w_gate_up, moe_b_gate_up[i], moe_w_down, moe_b_down[i],
                       ln_ffn_g[i], ln_ffn_b[i])
    return h
```

```python
import functools

import jax
import jax.numpy as jnp
from jax import lax
from jax.experimental import pallas as pl
from jax.experimental.pallas import tpu as pltpu

F32 = jnp.float32
BF16 = jnp.bfloat16

DEPTH = 2
DEEPNORM_ALPHA = (2 * DEPTH) ** 0.25
LN_EPS = 1e-5
POOL_WINDOWS = (2, 4, 8, 16)
POOL_HALO = 16
SSM_HEAD_DIM = 64
SSM_N_GROUPS = 8
SSM_HEADS_PER_GROUP = 4
SSM_D_STATE = 128
SSM_CONV = 4
CONV_HALO = 8
SSM_NORM_EPS = 1e-5
XA_HEADS = 4
N_EXPERTS = 32
TOP_K = 4
SWIGLU_LIMIT = 7.0
SWIGLU_ALPHA = 1.702

LANES = 128
SUBLANES = 8
VMEM_LIMIT_BYTES = 56 * 1024 * 1024

POOL_TILE = 1024
XATTN_TILE = 1024
XATTN_SPLIT = 2
SSD_TILE = 256
SSD_CHUNK = 128
SSD_STRIP = 512
MOE_TILE = 1024
MOE_ROW_GROUP = 256
DISPATCH_TILE = 512
COMBINE_TILE = 512
ROWS_PER_ISSUE = 8

ROUTE_ROWS = 16

NEG_BIG = -1e30


def _layer_norm(v, g, b):
    mu = jnp.mean(v, -1, keepdims=True)
    d = v - mu
    var = jnp.mean(d * d, -1, keepdims=True)
    return d * lax.rsqrt(var + LN_EPS) * g + b


def _dot(a, b):
    return jnp.dot(a, b, preferred_element_type=F32)


def _dot_nt(a, b):
    return lax.dot_general(a, b, (((1,), (1,)), ((), ())), preferred_element_type=F32)


def _dot_tn(a, b):
    return lax.dot_general(a, b, (((0,), (0,)), ((), ())), preferred_element_type=F32)


def _const_spec(shape):
    nd = len(shape)
    return pl.BlockSpec(shape, lambda *_: (0,) * nd, pipeline_mode=pl.Buffered(1))


def _params(n_axes):
    return pltpu.CompilerParams(dimension_semantics=("arbitrary",) * n_axes,
                                vmem_limit_bytes=VMEM_LIMIT_BYTES)


def _pool_kernel(x_ref, win_ref, wgrp_ref, scale_ref, wout_ref, g_ref, b_ref, o_ref, ext_ref, *, ts, gd):
    s = pl.program_id(1)

    @pl.when(s == 0)
    def _():
        ext_ref[0:POOL_HALO, :] = jnp.zeros((POOL_HALO, ext_ref.shape[1]), F32)

    x = x_ref[...]
    ext_ref[POOL_HALO:, :] = _dot(x.astype(BF16), win_ref[...])
    pos = lax.broadcasted_iota(jnp.int32, (ts, 1), 0) + s * ts
    mix = None
    for g, w in enumerate(POOL_WINDOWS):
        cols = slice(g * gd, (g + 1) * gd)
        e = ext_ref[:, cols]
        acc = e
        sh = 1
        while sh < w:
            acc = acc + pltpu.roll(acc, sh, axis=0)
            sh *= 2
        cnt = jnp.minimum(pos + 1, w).astype(F32)
        m = acc[POOL_HALO:, :] / cnt - e[POOL_HALO:, :]
        yg = _dot(m.astype(BF16), wgrp_ref[g]) * scale_ref[:, cols]
        part = _dot(yg.astype(BF16), wout_ref[cols, :])
        mix = part if mix is None else mix + part
    ext_ref[0:POOL_HALO, :] = ext_ref[ts:ts + POOL_HALO, :]
    o_ref[...] = _layer_norm(DEEPNORM_ALPHA * x + mix, g_ref[...], b_ref[...])


def _pool_layer(h, w_in, w_grp, scale, w_out, ln_g, ln_b):
    bsz, seq, d = h.shape
    ts = POOL_TILE
    gd = d // len(POOL_WINDOWS)
    tile = pl.BlockSpec((None, ts, d), lambda b, s: (b, s, 0))
    return pl.pallas_call(
        functools.partial(_pool_kernel, ts=ts, gd=gd),
        out_shape=jax.ShapeDtypeStruct((bsz, seq, d), F32),
        grid=(bsz, seq // ts),
        in_specs=[tile, _const_spec((d, d)), _const_spec((len(POOL_WINDOWS), gd, gd)), _const_spec((1, d)),
                  _const_spec((d, d)), _const_spec((1, d)), _const_spec((1, d))],
        out_specs=tile,
        scratch_shapes=[pltpu.VMEM((POOL_HALO + ts, d), F32)],
        compiler_params=_params(2),
        name="pool_mixer_ln",
    )(h, w_in.astype(BF16), w_grp.astype(BF16), scale.reshape(1, d), w_out.astype(BF16),
      ln_g.reshape(1, d), ln_b.reshape(1, d))


def _sigmoid(v):
    return 1.0 / (1.0 + jnp.exp(-v))


def _ssd_kernel(x_ref, wz_ref, wx_ref, wb_ref, wc_ref, wdt_ref, convw_ref, convb_ref, dtb_ref, alog_ref,
                dexp_ref, normg_ref, expand_ref, wout_ref, g_ref, b_ref, o_ref,
                xbc_scr, z_scr, actx_scr, bmat_scr, cmat_scr, y_scr, state_scr, *, ts, q, d_inner, gn):
    s = pl.program_id(1)
    conv_dim = d_inner + 2 * gn
    gw = d_inner // SSM_N_GROUPS

    @pl.when(s == 0)
    def _():
        xbc_scr[:, 0:CONV_HALO, :] = jnp.zeros((conv_dim // LANES, CONV_HALO, LANES), F32)
        state_scr[...] = jnp.zeros(state_scr.shape, F32)

    x = x_ref[...]
    xb = x.astype(BF16)
    dt_raw = _dot(xb, wdt_ref[...]) + dtb_ref[...]
    dtv = jnp.maximum(dt_raw, 0.0) + jnp.log1p(jnp.exp(-jnp.abs(dt_raw)))
    a_all = dtv * (-jnp.exp(alog_ref[...]))

    for w_ref, base in ((wb_ref, d_inner), (wc_ref, d_inner + gn), (wx_ref, 0)):
        for c in range(0, w_ref.shape[1], SSD_STRIP):
            res = _dot(xb, w_ref[:, c:c + SSD_STRIP])
            for jj in range(SSD_STRIP // LANES):
                j = (base + c) // LANES + jj
                cs = slice(j * LANES, (j + 1) * LANES)
                xbc_scr[j, CONV_HALO:, :] = res[:, jj * LANES:(jj + 1) * LANES]
                acc = convb_ref[:, cs]
                for k in range(SSM_CONV):
                    r0 = CONV_HALO - (SSM_CONV - 1) + k
                    acc = acc + convw_ref[k:k + 1, cs] * xbc_scr[j, r0:r0 + ts, :]
                act = acc * _sigmoid(acc)
                col = j * LANES
                if col < d_inner:
                    actx_scr[:, cs] = act
                elif col < d_inner + gn:
                    bmat_scr[:, col - d_inner:col - d_inner + LANES] = act.astype(BF16)
                else:
                    cmat_scr[:, col - d_inner - gn:col - d_inner - gn + LANES] = act.astype(BF16)
                xbc_scr[j, 0:CONV_HALO, :] = xbc_scr[j, ts:ts + CONV_HALO, :]

    expand = expand_ref[...]
    row_i = lax.broadcasted_iota(jnp.int32, (q, LANES), 0)
    causal = lax.broadcasted_iota(jnp.int32, (q, q), 0) >= lax.broadcasted_iota(jnp.int32, (q, q), 1)
    head_of_lane = lax.broadcasted_iota(jnp.int32, (q, gw), 1) // SSM_HEAD_DIM
    head_mask = [jnp.where(head_of_lane == r, 1.0, 0.0).astype(BF16) for r in range(SSM_HEADS_PER_GROUP)]

    for c in range(ts // q):
        rows = slice(c * q, (c + 1) * q)
        acs = a_all[rows, :]
        sh = 1
        while sh < q:
            acs = acs + jnp.where(row_i >= sh, pltpu.roll(acs, sh, axis=0), 0.0)
            sh *= 2
        acs_t = acs.T
        a_last = acs[q - 1:q, :]
        e_in = _dot(jnp.exp(acs).astype(BF16), expand)
        dec = _dot(jnp.exp(a_last - acs).astype(BF16), expand)
        dtx = _dot(dtv[rows, :].astype(BF16), expand)
        cd = jnp.broadcast_to(jnp.exp(a_last), (SUBLANES, LANES))
        cd_hi = cd.astype(BF16)
        cd_lo = (cd - cd_hi.astype(F32)).astype(BF16)
        cdx = (_dot(cd_hi, expand) + _dot(cd_lo, expand))[0:1, :]
        xd = actx_scr[rows, :] * dtx
        xdd = xd * dec
        xdb = xd.astype(BF16)
        for g in range(SSM_N_GROUPS):
            gc = slice(g * gw, (g + 1) * gw)
            nc = slice(g * SSM_D_STATE, (g + 1) * SSM_D_STATE)
            if c == 0:
                z_scr[:, gc] = _dot(xb, wz_ref[:, gc])
            bg = bmat_scr[rows, nc]
            cg = cmat_scr[rows, nc]
            cb = _dot_nt(cg, bg)
            xg = xdb[:, gc]
            mhs, xms = [], []
            for r in range(SSM_HEADS_PER_GROUP):
                hd = g * SSM_HEADS_PER_GROUP + r
                seg = acs[:, hd:hd + 1] - acs_t[hd:hd + 1, :]
                lmat = jnp.exp(jnp.where(causal, seg, NEG_BIG))
                mhs.append((cb * lmat).astype(BF16))
                xms.append(xg * head_mask[r])
            yg = (_dot(cg, state_scr[g].astype(BF16)) * e_in[:, gc]
                  + _dot(jnp.concatenate(mhs, axis=1), jnp.concatenate(xms, axis=0)))
            y_scr[rows, gc] = yg
            state_scr[g] = state_scr[g] * cdx[:, gc] + _dot_tn(bg, xdd[:, gc].astype(BF16))

    mix = None
    for g in range(SSM_N_GROUPS):
        gc = slice(g * gw, (g + 1) * gw)
        zz = z_scr[:, gc]
        yv = (y_scr[:, gc] + dexp_ref[:, gc] * actx_scr[:, gc]) * (zz * _sigmoid(zz))
        yv = yv * lax.rsqrt(jnp.mean(yv * yv, -1, keepdims=True) + SSM_NORM_EPS) * normg_ref[:, gc]
        part = _dot(yv.astype(BF16), wout_ref[gc, :])
        mix = part if mix is None else mix + part
    o_ref[...] = _layer_norm(DEEPNORM_ALPHA * x + mix, g_ref[...], b_ref[...])


def _ssd_layer(h, w_in, conv_w, conv_b, dt_bias, a_log, d_skip, norm_g, w_out, ln_g, ln_b):
    bsz, seq, d = h.shape
    n_heads = a_log.shape[0]
    d_inner = n_heads * SSM_HEAD_DIM
    gn = SSM_N_GROUPS * SSM_D_STATE
    conv_dim = d_inner + 2 * gn
    ts, q = SSD_TILE, SSD_CHUNK
    w_in = w_in.astype(BF16)
    wz = w_in[:, :d_inner]
    wx = w_in[:, d_inner:2 * d_inner]
    wb = w_in[:, 2 * d_inner:2 * d_inner + gn]
    wc = w_in[:, 2 * d_inner + gn:2 * d_inner + 2 * gn]
    pad = LANES - n_heads
    wdt = jnp.pad(w_in[:, d_inner + conv_dim:], ((0, 0), (0, pad)))
    dtb = jnp.pad(dt_bias.astype(F32), (0, pad)).reshape(1, LANES)
    alog = jnp.pad(a_log.astype(F32), (0, pad)).reshape(1, LANES)
    dexp = jnp.repeat(d_skip.astype(F32), SSM_HEAD_DIM).reshape(1, d_inner)
    expand = (jnp.arange(LANES)[:, None] == (jnp.arange(d_inner)[None, :] // SSM_HEAD_DIM)).astype(BF16)
    tile = pl.BlockSpec((None, ts, d), lambda b, s: (b, s, 0))
    return pl.pallas_call(
        functools.partial(_ssd_kernel, ts=ts, q=q, d_inner=d_inner, gn=gn),
        out_shape=jax.ShapeDtypeStruct((bsz, seq, d), F32),
        grid=(bsz, seq // ts),
        in_specs=[tile, _const_spec((d, d_inner)), _const_spec((d, d_inner)), _const_spec((d, gn)),
                  _const_spec((d, gn)), _const_spec((d, LANES)), _const_spec((SSM_CONV, conv_dim)),
                  _const_spec((1, conv_dim)), _const_spec((1, LANES)), _const_spec((1, LANES)),
                  _const_spec((1, d_inner)), _const_spec((1, d_inner)), _const_spec((LANES, d_inner)),
                  _const_spec((d_inner, d)), _const_spec((1, d)), _const_spec((1, d))],
        out_specs=tile,
        scratch_shapes=[pltpu.VMEM((conv_dim // LANES, CONV_HALO + ts, LANES), F32), pltpu.VMEM((ts, d_inner), F32),
                        pltpu.VMEM((ts, d_inner), F32), pltpu.VMEM((ts, gn), BF16), pltpu.VMEM((ts, gn), BF16),
                        pltpu.VMEM((ts, d_inner), F32),
                        pltpu.VMEM((SSM_N_GROUPS, SSM_D_STATE, d_inner // SSM_N_GROUPS), F32)],
        compiler_params=_params(2),
        name="ssd_mixer_ln",
    )(h, wz, wx, wb, wc, wdt, conv_w.astype(F32), conv_b.reshape(1, conv_dim), dtb, alog, dexp,
      norm_g.reshape(1, d_inner), expand, w_out.astype(BF16), ln_g.reshape(1, d), ln_b.reshape(1, d))


def _xattn_kernel(h_ref, mem_ref, wq_ref, wk_ref, wv_ref, wo_ref, g_ref, b_ref, wrh_ref, wrl_ref, br_ref,
                  o_ref, route_ref, route_t_ref, cnt_ref, k_scr, v_scr, carry_scr, *, ts, hd):
    b = pl.program_id(0)
    s = pl.program_id(1)

    @pl.when(s == 0)
    def _():
        mb = mem_ref[...].astype(BF16)
        k_scr[...] = _dot(mb, wk_ref[...]).astype(BF16)
        v_scr[...] = _dot(mb, wv_ref[...]).astype(BF16)

    @pl.when((b == 0) & (s == 0))
    def _():
        carry_scr[...] = jnp.zeros(carry_scr.shape, F32)

    sub = ts // XATTN_SPLIT
    halves = []
    for u in range(XATTN_SPLIT):
        rows = slice(u * sub, (u + 1) * sub)
        h = h_ref[rows, :]
        qv = (_dot(h.astype(BF16), wq_ref[...]) * (hd ** -0.5)).astype(BF16)
        heads = []
        for hh in range(XA_HEADS):
            cols = slice(hh * hd, (hh + 1) * hd)
            sc = _dot_nt(qv[:, cols], k_scr[:, cols])
            p = jnp.exp(sc - jnp.max(sc, -1, keepdims=True))
            o = _dot(p.astype(BF16), v_scr[:, cols]) / jnp.sum(p, -1, keepdims=True)
            heads.append(o.astype(BF16))
        halves.append((h, _dot(jnp.concatenate(heads, axis=1), wo_ref[...])))

    lane = lax.broadcasted_iota(jnp.int32, (sub, LANES), 1).astype(F32)
    below = (lax.broadcasted_iota(jnp.int32, (sub, sub), 0) > lax.broadcasted_iota(jnp.int32, (sub, sub), 1))
    below = jnp.where(below, 1.0, 0.0).astype(BF16)
    for u in range(XATTN_SPLIT):
        rows = slice(u * sub, (u + 1) * sub)
        h, xa = halves[u]
        h2 = _layer_norm(DEEPNORM_ALPHA * h + xa, g_ref[...], b_ref[...])
        o_ref[rows, :] = h2
        h2_hi = h2.astype(BF16)

        h2_lo = (h2 - h2_hi.astype(F32)).astype(BF16)
        logits = _dot(h2_hi, wrh_ref[...]) + _dot(h2_lo, wrh_ref[...]) + _dot(h2_hi, wrl_ref[...]) + br_ref[...]
        work = logits
        vals, idxs, sels = [], [], []
        for _ in range(TOP_K):
            m = jnp.max(work, -1, keepdims=True)
            ik = jnp.min(jnp.where(work == m, lane, float(LANES)), -1, keepdims=True)
            sel = lane == ik
            vals.append(m)
            idxs.append(ik)
            sels.append(sel)
            work = jnp.where(sel, -jnp.inf, work)
        exps = [jnp.exp(v - vals[0]) for v in vals]
        den = exps[0]
        for e in exps[1:]:
            den = den + e
        onehot = jnp.zeros((sub, LANES), F32)
        for sel in sels:
            onehot = onehot + sel.astype(F32)
        before = _dot(below, onehot.astype(BF16)) + carry_scr[0:1, :]
        route = jnp.zeros((sub, LANES), F32)
        for k in range(TOP_K):
            rank = jnp.sum(jnp.where(sels[k], before, 0.0), -1, keepdims=True)
            route = jnp.where(lane == float(k), idxs[k], route)
            route = jnp.where(lane == float(TOP_K + k), exps[k] / den, route)
            route = jnp.where(lane == float(2 * TOP_K + k), rank, route)
        route_ref[rows, :] = route
        route_t_ref[:, rows] = route.T[0:ROUTE_ROWS, :]
        carry_scr[...] = carry_scr[...] + jnp.sum(onehot, 0, keepdims=True)
    cnt_ref[...] = carry_scr[...]


def _xattn_layer(h, mem, wq, wk, wv, wo, ln_g, ln_b, w_router, b_router):
    bsz, seq, d = h.shape
    mlen = mem.shape[1]
    ts = XATTN_TILE
    hd = d // XA_HEADS
    n_exp = w_router.shape[1]
    wr = jnp.pad(w_router.astype(F32), ((0, 0), (0, LANES - n_exp)))
    wr_hi = wr.astype(BF16)
    wr_lo = (wr - wr_hi.astype(F32)).astype(BF16)
    br = jnp.pad(b_router.astype(F32), (0, LANES - n_exp), constant_values=-jnp.inf).reshape(1, LANES)
    tile = pl.BlockSpec((None, ts, d), lambda b, s: (b, s, 0))
    rtile = pl.BlockSpec((None, ts, LANES), lambda b, s: (b, s, 0))
    return pl.pallas_call(
        functools.partial(_xattn_kernel, ts=ts, hd=hd),
        out_shape=(jax.ShapeDtypeStruct((bsz, seq, d), F32), jax.ShapeDtypeStruct((bsz, seq, LANES), F32),
                   jax.ShapeDtypeStruct((ROUTE_ROWS, bsz * seq), F32), jax.ShapeDtypeStruct((SUBLANES, LANES), F32)),
        grid=(bsz, seq // ts),
        in_specs=[tile, pl.BlockSpec((None, mlen, d), lambda b, s: (b, 0, 0)),
                  _const_spec((d, d)), _const_spec((d, d)), _const_spec((d, d)), _const_spec((d, d)),
                  _const_spec((1, d)), _const_spec((1, d)), _const_spec((d, LANES)), _const_spec((d, LANES)),
                  _const_spec((1, LANES))],
        out_specs=(tile, rtile, pl.BlockSpec((ROUTE_ROWS, ts), lambda b, s: (0, b * (seq // ts) + s)),
                   pl.BlockSpec((SUBLANES, LANES), lambda b, s: (0, 0))),
        scratch_shapes=[pltpu.VMEM((mlen, d), BF16), pltpu.VMEM((mlen, d), BF16), pltpu.VMEM((SUBLANES, LANES), F32)],
        compiler_params=_params(2),
        name="xattn_ln_router",
    )(h, mem, wq.astype(BF16), wk.astype(BF16), wv.astype(BF16), wo.astype(BF16),
      ln_g.reshape(1, d), ln_b.reshape(1, d), wr_hi, wr_lo, br)


def _rows_from_tiles(ref, n):
    return jnp.concatenate([ref[pl.ds(c, n, stride=SUBLANES), :] for c in range(SUBLANES)], axis=1)


def _rows_to_tiles(ref, v, n):
    for c in range(SUBLANES):
        ref[pl.ds(c, n, stride=SUBLANES), :] = v[:, c * LANES:(c + 1) * LANES]


def _moe_kernel(be_ref, nu_ref, nv_ref, x_ref, wgu_ref, bg_ref, bu_ref, wd_ref, bd_ref, o_ref, wg_scr, wu_scr, wd_scr,
                *, tm):
    i = pl.program_id(0)
    active = i < nu_ref[0]
    new_expert = (i == 0) | (be_ref[i] != be_ref[jnp.maximum(i - 1, 0)])

    @pl.when(active & new_expert)
    def _():
        w2 = 2 * LANES
        src = lax.broadcasted_iota(jnp.int32, (w2, w2), 0)
        dst = lax.broadcasted_iota(jnp.int32, (w2, w2), 1)
        perm = jnp.where(src == jnp.where(dst < LANES, 2 * dst, 2 * (dst - LANES) + 1), 1.0, 0.0).astype(BF16)
        for c in range(wgu_ref.shape[1] // w2):
            res = _dot(wgu_ref[:, c * w2:(c + 1) * w2].astype(BF16), perm)
            wg_scr[:, c * LANES:(c + 1) * LANES] = res[:, :LANES].astype(BF16)
            wu_scr[:, c * LANES:(c + 1) * LANES] = res[:, LANES:].astype(BF16)
        wd_scr[...] = wd_ref[...].astype(BF16)

    nv = nv_ref[i]
    for r in range(MOE_ROW_GROUP, tm + 1, MOE_ROW_GROUP):
        @pl.when(active & (nv > r - MOE_ROW_GROUP) & (nv <= r))
        def _():
            x = _rows_from_tiles(x_ref, r).astype(BF16)
            gate = jnp.minimum(_dot(x, wg_scr[...]) + bg_ref[...], SWIGLU_LIMIT)
            up = jnp.clip(_dot(x, wu_scr[...]) + bu_ref[...], -SWIGLU_LIMIT, SWIGLU_LIMIT)
            act = (up + 1.0) * (gate * _sigmoid(SWIGLU_ALPHA * gate))
            _rows_to_tiles(o_ref, _dot(act.astype(BF16), wd_scr[...]) + bd_ref[...], r)
            if r < tm:
                o_ref[r * SUBLANES:, :] = jnp.zeros(((tm - r) * SUBLANES, LANES), F32)

    @pl.when(jnp.logical_not(active))
    def _():
        o_ref[...] = jnp.zeros(o_ref.shape, F32)


def _moe_experts(xs, block_e, n_used, n_valid, layer, w_gate_up, b_gate, b_up, w_down, b_down):
    _, n_exp, f, d = w_down.shape
    n_rows = xs.shape[0] // SUBLANES
    tm = MOE_TILE
    n_blocks = n_rows // tm

    def row_map(i, be, nu, nv):
        return (jnp.minimum(i, nu[0] - 1), 0)

    def w_map(i, be, nu, nv):
        return (be[i], 0, 0)

    def lw_map(i, be, nu, nv):
        return (layer, be[i], 0, 0)

    grid_spec = pltpu.PrefetchScalarGridSpec(
        num_scalar_prefetch=3,
        grid=(n_blocks,),
        in_specs=[pl.BlockSpec((tm * SUBLANES, LANES), row_map),
                  pl.BlockSpec((None, None, d, 2 * f), lw_map),
                  pl.BlockSpec((None, 1, f), w_map), pl.BlockSpec((None, 1, f), w_map),
                  pl.BlockSpec((None, None, f, d), lw_map), pl.BlockSpec((None, 1, d), w_map)],
        out_specs=pl.BlockSpec((tm * SUBLANES, LANES), lambda i, be, nu, nv: (i, 0)),
        scratch_shapes=[pltpu.VMEM((d, f), BF16), pltpu.VMEM((d, f), BF16), pltpu.VMEM((f, d), BF16)],
    )
    return pl.pallas_call(
        functools.partial(_moe_kernel, tm=tm),
        out_shape=jax.ShapeDtypeStruct((n_rows * SUBLANES, LANES), F32),
        grid_spec=grid_spec,
        compiler_params=_params(1),
        name="moe_experts",
    )(block_e, n_used, n_valid, xs, w_gate_up, b_gate.reshape(n_exp, 1, f), b_up.reshape(n_exp, 1, f),
      w_down, b_down.reshape(n_exp, 1, d))


def _row_copy(src, src_row, dst, dst_row, sem):
    return pltpu.make_async_copy(src.at[pl.ds(pl.multiple_of(src_row * SUBLANES, SUBLANES), SUBLANES)],
                                 dst.at[pl.ds(pl.multiple_of(dst_row * SUBLANES, SUBLANES), SUBLANES)], sem)


def _dispatch_kernel(dest_ref, zblk_ref, h_ref, xs_ref, stage0, stage1, zero_scr, sem, zsem, *, ts, n_tiles, tm):
    i = pl.program_id(0)

    n_fill = zblk_ref.shape[0] // 2

    def fill_wait(lo, hi, s):
        for j in range(lo, hi):
            @pl.when(zblk_ref[j] >= 0)
            def _():
                pltpu.make_async_copy(zero_scr, xs_ref.at[pl.ds(0, tm * SUBLANES)], zsem.at[s]).wait()

    @pl.when(i == 0)
    def _():
        zero_scr[...] = jnp.zeros(zero_scr.shape, F32)
        for j in range(2 * n_fill):
            @pl.when(zblk_ref[j] >= 0)
            def _():
                start = pl.multiple_of(zblk_ref[j] * (tm * SUBLANES), tm * SUBLANES)
                pltpu.make_async_copy(zero_scr, xs_ref.at[pl.ds(start, tm * SUBLANES)], zsem.at[j // n_fill]).start()
        fill_wait(0, n_fill, 0)

    @pl.when(i == n_tiles - 1)
    def _():
        fill_wait(n_fill, 2 * n_fill, 1)

    def drain(stage, s):
        for _ in range(TOP_K):
            pltpu.make_async_copy(stage, xs_ref.at[pl.ds(0, ts * SUBLANES)], sem.at[s]).wait()

    def step(stage, s, other, so):
        @pl.when(i >= 2)
        def _():
            drain(stage, s)

        _rows_to_tiles(stage, h_ref[...], ts)

        def issue(j, carry):
            for u in range(ROWS_PER_ISSUE):
                tok = j * ROWS_PER_ISSUE + u
                for k in range(TOP_K):
                    row = dest_ref[k * (n_tiles * ts) + i * ts + tok]
                    _row_copy(stage, tok, xs_ref, row, sem.at[s]).start(priority=k % 2)
            return carry

        lax.fori_loop(0, ts // ROWS_PER_ISSUE, issue, 0)

        @pl.when(i == n_tiles - 1)
        def _():
            drain(other, so)
            drain(stage, s)

    @pl.when(i % 2 == 0)
    def _():
        step(stage0, 0, stage1, 1)

    @pl.when(i % 2 == 1)
    def _():
        step(stage1, 1, stage0, 0)


def _dispatch_rows(h, dest, zero_blocks, n_rows):
    t, d = h.shape
    ts = DISPATCH_TILE
    tm = MOE_TILE
    n_tiles = t // ts
    assert d == SUBLANES * LANES and n_tiles >= 2
    grid_spec = pltpu.PrefetchScalarGridSpec(
        num_scalar_prefetch=2,
        grid=(n_tiles,),
        in_specs=[pl.BlockSpec((ts, d), lambda i, dest, zb: (i, 0))],
        out_specs=pl.BlockSpec(memory_space=pl.ANY),
        scratch_shapes=[pltpu.VMEM((ts * SUBLANES, LANES), F32), pltpu.VMEM((ts * SUBLANES, LANES), F32),
                        pltpu.VMEM((tm * SUBLANES, LANES), F32),
                        pltpu.SemaphoreType.DMA((2,)), pltpu.SemaphoreType.DMA((2,))],
    )
    return pl.pallas_call(
        functools.partial(_dispatch_kernel, ts=ts, n_tiles=n_tiles, tm=tm),
        out_shape=jax.ShapeDtypeStruct((n_rows * SUBLANES, LANES), F32),
        grid_spec=grid_spec,
        compiler_params=_params(1),
        name="moe_dispatch",
    )(dest, zero_blocks, h)


def _combine_kernel(dest_ref, h_ref, route_ref, g_ref, b_ref, y_ref, o_ref, gbuf0, gbuf1, sem, *, ts, n_tiles):
    i = pl.program_id(0)

    def issue(tile, gbuf, s):
        def body(j, carry):
            for u in range(ROWS_PER_ISSUE):
                tok = j * ROWS_PER_ISSUE + u
                for k in range(TOP_K):
                    row = dest_ref[k * (n_tiles * ts) + tile * ts + tok]
                    _row_copy(y_ref, row, gbuf.at[k], tok, sem.at[s]).start(priority=k % 2)
            return carry

        lax.fori_loop(0, ts // ROWS_PER_ISSUE, body, 0)

    def drain(gbuf, s):
        for k in range(TOP_K):
            pltpu.make_async_copy(y_ref.at[pl.ds(0, ts * SUBLANES)], gbuf.at[k], sem.at[s]).wait()

    def step(gbuf, s, nxt, sn):
        @pl.when(i + 1 < n_tiles)
        def _():
            issue(i + 1, nxt, sn)

        drain(gbuf, s)
        parts = []
        for c in range(SUBLANES):
            acc = None
            for k in range(TOP_K):
                piece = gbuf[k, pl.ds(c, ts, stride=SUBLANES), :] * route_ref[:, TOP_K + k:TOP_K + k + 1]
                acc = piece if acc is None else acc + piece
            parts.append(acc)
        ff = jnp.concatenate(parts, axis=1)
        o_ref[...] = _layer_norm(DEEPNORM_ALPHA * h_ref[...] + ff, g_ref[...], b_ref[...])

    @pl.when(i == 0)
    def _():
        issue(0, gbuf0, 0)

    @pl.when(i % 2 == 0)
    def _():
        step(gbuf0, 0, gbuf1, 1)

    @pl.when(i % 2 == 1)
    def _():
        step(gbuf1, 1, gbuf0, 0)


def _combine_layer(h, y, dest, route, ln_g, ln_b):
    t, d = h.shape
    ts = COMBINE_TILE
    n_tiles = t // ts
    grid_spec = pltpu.PrefetchScalarGridSpec(
        num_scalar_prefetch=1,
        grid=(n_tiles,),
        in_specs=[pl.BlockSpec((ts, d), lambda i, dest: (i, 0)), pl.BlockSpec((ts, LANES), lambda i, dest: (i, 0)),
                  pl.BlockSpec((1, d), lambda i, dest: (0, 0)), pl.BlockSpec((1, d), lambda i, dest: (0, 0)),
                  pl.BlockSpec(memory_space=pl.ANY)],
        out_specs=pl.BlockSpec((ts, d), lambda i, dest: (i, 0)),
        scratch_shapes=[pltpu.VMEM((TOP_K, ts * SUBLANES, LANES), F32),
                        pltpu.VMEM((TOP_K, ts * SUBLANES, LANES), F32), pltpu.SemaphoreType.DMA((2,))],
    )
    return pl.pallas_call(
        functools.partial(_combine_kernel, ts=ts, n_tiles=n_tiles),
        out_shape=jax.ShapeDtypeStruct((t, d), F32),
        grid_spec=grid_spec,
        compiler_params=_params(1),
        name="moe_combine_ln",
    )(dest, h, route, ln_g.reshape(1, d), ln_b.reshape(1, d), y)


def _moe_layer(h2, route, route_t, cnt, layer, w_gate_up, b_gate_up, w_down, b_down, ln_g, ln_b):
    bsz, seq, d = h2.shape
    t = bsz * seq
    tm = MOE_TILE
    n_exp = w_gate_up.shape[1]
    h2 = h2.reshape(t, d)
    route = route.reshape(t, LANES)
    idx = route_t[0:TOP_K].astype(jnp.int32)
    rank = route_t[2 * TOP_K:3 * TOP_K].astype(jnp.int32)
    counts = cnt[0, :n_exp].astype(jnp.int32)
    padded = (counts + tm - 1) // tm * tm
    pend = jnp.cumsum(padded)
    pstart = pend - padded
    part = counts % tm
    skip = jnp.where(part > 0, tm - part, 0)

    def per_expert(table, e):
        return jnp.sum(jnp.where(e[..., None] == jnp.arange(n_exp, dtype=jnp.int32), table, 0), axis=-1)

    dest = (per_expert(pstart, idx) + rank
            + jnp.where(rank >= per_expert(part, idx), per_expert(skip, idx), 0)).reshape(-1)
    n_blocks = (t * TOP_K) // tm + n_exp
    n_used = (pend[-1] // tm).astype(jnp.int32)
    blk = jnp.arange(n_blocks, dtype=jnp.int32)
    block_e = jnp.sum((pend[None, :] <= (blk * tm)[:, None]).astype(jnp.int32), axis=1)
    block_e = jnp.minimum(block_e, n_exp - 1)
    block_e = jnp.where(blk < n_used, block_e, block_e[jnp.maximum(n_used - 1, 0)])
    first_blk = jnp.where(part > 0, pstart // tm, -1)
    tail_blk = n_used + jnp.arange(n_exp, dtype=jnp.int32)
    zero_blocks = jnp.concatenate([first_blk, jnp.where(tail_blk < n_blocks, tail_blk, -1)]).astype(jnp.int32)
    xs = _dispatch_rows(h2, dest, zero_blocks, n_blocks * tm)
    n_valid = jnp.where((blk == per_expert(first_blk, block_e)) & (blk < n_used), per_expert(part, block_e),
                        tm).astype(jnp.int32)
    y = _moe_experts(xs, block_e, n_used.reshape(1), n_valid, layer, w_gate_up, b_gate_up[:, 0::2], b_gate_up[:, 1::2],
                     w_down, b_down)
    return _combine_layer(h2, y, dest, route, ln_g, ln_b).reshape(bsz, seq, d)


def kernel(x, mem, pool_w_in, pool_w_grp, pool_scale, pool_w_out, ssm_w_in, ssm_conv_w, ssm_conv_b, ssm_dt_bias, ssm_a_log, ssm_d, ssm_norm_g, ssm_w_out, xa_wq, xa_wk, xa_wv, xa_wo, moe_w_router, moe_b_router, moe_w_gate_up, moe_b_gate_up, moe_w_down, moe_b_down, ln_mix_g, ln_mix_b, ln_xa_g, ln_xa_b, ln_ffn_g, ln_ffn_b):
    h = x
    for i in range(DEPTH):
        j = i // 2
        if i % 2 == 0:
            h = _pool_layer(h, pool_w_in[j], pool_w_grp[j], pool_scale[j], pool_w_out[j], ln_mix_g[i], ln_mix_b[i])
        else:
            h = _ssd_layer(h, ssm_w_in[j], ssm_conv_w[j], ssm_conv_b[j], ssm_dt_bias[j], ssm_a_log[j], ssm_d[j],
                           ssm_norm_g[j], ssm_w_out[j], ln_mix_g[i], ln_mix_b[i])
        h2, route, route_t, cnt = _xattn_layer(h, mem, xa_wq[i], xa_wk[i], xa_wv[i], xa_wo[i], ln_xa_g[i], ln_xa_b[i],
                                      moe_w_router[i], moe_b_router[i])
        h = _moe_layer(h2, route, route_t, cnt, i, moe_w_gate_up, moe_b_gate_up[i], moe_w_down, moe_b_down[i],
                       ln_ffn_g[i], ln_ffn_b[i])
    return h
```

```python
import functools

import jax
import jax.numpy as jnp
from jax import lax
from jax.experimental import pallas as pl
from jax.experimental.pallas import tpu as pltpu

F32 = jnp.float32
BF16 = jnp.bfloat16

DEPTH = 2
DEEPNORM_ALPHA = (2 * DEPTH) ** 0.25
LN_EPS = 1e-5
POOL_WINDOWS = (2, 4, 8, 16)
POOL_HALO = 16
SSM_HEAD_DIM = 64
SSM_N_GROUPS = 8
SSM_HEADS_PER_GROUP = 4
SSM_D_STATE = 128
SSM_CONV = 4
CONV_HALO = 8
SSM_NORM_EPS = 1e-5
XA_HEADS = 4
N_EXPERTS = 32
TOP_K = 4
SWIGLU_LIMIT = 7.0
SWIGLU_ALPHA = 1.702

LANES = 128
SUBLANES = 8
VMEM_LIMIT_BYTES = 56 * 1024 * 1024

POOL_TILE = 1024
XATTN_TILE = 1024
XATTN_SPLIT = 2
SSD_TILE = 256
SSD_CHUNK = 128
SSD_STRIP = 512
MOE_TILE = 1024
MOE_ROW_GROUP = 256
DISPATCH_TILE = 512
COMBINE_TILE = 512
ROWS_PER_ISSUE = 8

ROUTE_ROWS = 16

NEG_BIG = -1e30


def _layer_norm(v, g, b):
    mu = jnp.mean(v, -1, keepdims=True)
    d = v - mu
    var = jnp.mean(d * d, -1, keepdims=True)
    return d * lax.rsqrt(var + LN_EPS) * g + b


def _dot(a, b):
    return jnp.dot(a, b, preferred_element_type=F32)


def _dot_nt(a, b):
    return lax.dot_general(a, b, (((1,), (1,)), ((), ())), preferred_element_type=F32)


def _dot_tn(a, b):
    return lax.dot_general(a, b, (((0,), (0,)), ((), ())), preferred_element_type=F32)


def _const_spec(shape):
    nd = len(shape)
    return pl.BlockSpec(shape, lambda *_: (0,) * nd, pipeline_mode=pl.Buffered(1))


def _params(n_axes):
    return pltpu.CompilerParams(dimension_semantics=("arbitrary",) * n_axes,
                                vmem_limit_bytes=VMEM_LIMIT_BYTES)


def _pool_kernel(x_ref, win_ref, wgrp_ref, scale_ref, wout_ref, g_ref, b_ref, o_ref, ext_ref, *, ts, gd):
    s = pl.program_id(1)

    @pl.when(s == 0)
    def _():
        ext_ref[0:POOL_HALO, :] = jnp.zeros((POOL_HALO, ext_ref.shape[1]), F32)

    x = x_ref[...]
    ext_ref[POOL_HALO:, :] = _dot(x.astype(BF16), win_ref[...])
    pos = lax.broadcasted_iota(jnp.int32, (ts, 1), 0) + s * ts
    mix = None
    for g, w in enumerate(POOL_WINDOWS):
        cols = slice(g * gd, (g + 1) * gd)
        e = ext_ref[:, cols]
        acc = e
        sh = 1
        while sh < w:
            acc = acc + pltpu.roll(acc, sh, axis=0)
            sh *= 2
        cnt = jnp.minimum(pos + 1, w).astype(F32)
        m = acc[POOL_HALO:, :] / cnt - e[POOL_HALO:, :]
        yg = _dot(m.astype(BF16), wgrp_ref[g]) * scale_ref[:, cols]
        part = _dot(yg.astype(BF16), wout_ref[cols, :])
        mix = part if mix is None else mix + part
    ext_ref[0:POOL_HALO, :] = ext_ref[ts:ts + POOL_HALO, :]
    o_ref[...] = _layer_norm(DEEPNORM_ALPHA * x + mix, g_ref[...], b_ref[...])


def _pool_layer(h, w_in, w_grp, scale, w_out, ln_g, ln_b):
    bsz, seq, d = h.shape
    ts = POOL_TILE
    gd = d // len(POOL_WINDOWS)
    tile = pl.BlockSpec((None, ts, d), lambda b, s: (b, s, 0))
    return pl.pallas_call(
        functools.partial(_pool_kernel, ts=ts, gd=gd),
        out_shape=jax.ShapeDtypeStruct((bsz, seq, d), F32),
        grid=(bsz, seq // ts),
        in_specs=[tile, _const_spec((d, d)), _const_spec((len(POOL_WINDOWS), gd, gd)), _const_spec((1, d)),
                  _const_spec((d, d)), _const_spec((1, d)), _const_spec((1, d))],
        out_specs=tile,
        scratch_shapes=[pltpu.VMEM((POOL_HALO + ts, d), F32)],
        compiler_params=_params(2),
        name="pool_mixer_ln",
    )(h, w_in.astype(BF16), w_grp.astype(BF16), scale.reshape(1, d), w_out.astype(BF16),
      ln_g.reshape(1, d), ln_b.reshape(1, d))


def _sigmoid(v):
    return 1.0 / (1.0 + jnp.exp(-v))


def _ssd_body(x, wz_ref, wx_ref, wb_ref, wc_ref, wdt_ref, convw_ref, convb_ref, dtb_ref, alog_ref,
              dexp_ref, normg_ref, expand_ref, wout_ref, g_ref, b_ref, o_ref,
              xbc_scr, z_scr, actx_scr, bmat_scr, cmat_scr, y_scr, state_scr, *, ts, q, d_inner, gn):
    s = pl.program_id(1)
    conv_dim = d_inner + 2 * gn
    gw = d_inner // SSM_N_GROUPS

    @pl.when(s == 0)
    def _():
        xbc_scr[:, 0:CONV_HALO, :] = jnp.zeros((conv_dim // LANES, CONV_HALO, LANES), F32)
        state_scr[...] = jnp.zeros(state_scr.shape, F32)

    xb = x.astype(BF16)
    dt_raw = _dot(xb, wdt_ref[...]) + dtb_ref[...]
    dtv = jnp.maximum(dt_raw, 0.0) + jnp.log1p(jnp.exp(-jnp.abs(dt_raw)))
    a_all = dtv * (-jnp.exp(alog_ref[...]))

    for w_ref, base in ((wb_ref, d_inner), (wc_ref, d_inner + gn), (wx_ref, 0)):
        for c in range(0, w_ref.shape[1], SSD_STRIP):
            res = _dot(xb, w_ref[:, c:c + SSD_STRIP])
            for jj in range(SSD_STRIP // LANES):
                j = (base + c) // LANES + jj
                cs = slice(j * LANES, (j + 1) * LANES)
                xbc_scr[j, CONV_HALO:, :] = res[:, jj * LANES:(jj + 1) * LANES]
                acc = convb_ref[:, cs]
                for k in range(SSM_CONV):
                    r0 = CONV_HALO - (SSM_CONV - 1) + k
                    acc = acc + convw_ref[k:k + 1, cs] * xbc_scr[j, r0:r0 + ts, :]
                act = acc * _sigmoid(acc)
                col = j * LANES
                if col < d_inner:
                    actx_scr[:, cs] = act
                elif col < d_inner + gn:
                    bmat_scr[:, col - d_inner:col - d_inner + LANES] = act.astype(BF16)
                else:
                    cmat_scr[:, col - d_inner - gn:col - d_inner - gn + LANES] = act.astype(BF16)
                xbc_scr[j, 0:CONV_HALO, :] = xbc_scr[j, ts:ts + CONV_HALO, :]

    expand = expand_ref[...]
    row_i = lax.broadcasted_iota(jnp.int32, (q, LANES), 0)
    causal = lax.broadcasted_iota(jnp.int32, (q, q), 0) >= lax.broadcasted_iota(jnp.int32, (q, q), 1)
    head_of_lane = lax.broadcasted_iota(jnp.int32, (q, gw), 1) // SSM_HEAD_DIM
    head_mask = [jnp.where(head_of_lane == r, 1.0, 0.0).astype(BF16) for r in range(SSM_HEADS_PER_GROUP)]

    for c in range(ts // q):
        rows = slice(c * q, (c + 1) * q)
        acs = a_all[rows, :]
        sh = 1
        while sh < q:
            acs = acs + jnp.where(row_i >= sh, pltpu.roll(acs, sh, axis=0), 0.0)
            sh *= 2
        acs_t = acs.T
        a_last = acs[q - 1:q, :]
        e_in = _dot(jnp.exp(acs).astype(BF16), expand)
        dec = _dot(jnp.exp(a_last - acs).astype(BF16), expand)
        dtx = _dot(dtv[rows, :].astype(BF16), expand)
        cd = jnp.broadcast_to(jnp.exp(a_last), (SUBLANES, LANES))
        cd_hi = cd.astype(BF16)
        cd_lo = (cd - cd_hi.astype(F32)).astype(BF16)
        cdx = (_dot(cd_hi, expand) + _dot(cd_lo, expand))[0:1, :]
        xd = actx_scr[rows, :] * dtx
        xdd = xd * dec
        xdb = xd.astype(BF16)
        for g in range(SSM_N_GROUPS):
            gc = slice(g * gw, (g + 1) * gw)
            nc = slice(g * SSM_D_STATE, (g + 1) * SSM_D_STATE)
            if c == 0:
                z_scr[:, gc] = _dot(xb, wz_ref[:, gc])
            bg = bmat_scr[rows, nc]
            cg = cmat_scr[rows, nc]
            cb = _dot_nt(cg, bg)
            xg = xdb[:, gc]
            mhs, xms = [], []
            for r in range(SSM_HEADS_PER_GROUP):
                hd = g * SSM_HEADS_PER_GROUP + r
                seg = acs[:, hd:hd + 1] - acs_t[hd:hd + 1, :]
                lmat = jnp.exp(jnp.where(causal, seg, NEG_BIG))
                mhs.append((cb * lmat).astype(BF16))
                xms.append(xg * head_mask[r])
            yg = (_dot(cg, state_scr[g].astype(BF16)) * e_in[:, gc]
                  + _dot(jnp.concatenate(mhs, axis=1), jnp.concatenate(xms, axis=0)))
            y_scr[rows, gc] = yg
            state_scr[g] = state_scr[g] * cdx[:, gc] + _dot_tn(bg, xdd[:, gc].astype(BF16))

    mix = None
    for g in range(SSM_N_GROUPS):
        gc = slice(g * gw, (g + 1) * gw)
        zz = z_scr[:, gc]
        yv = (y_scr[:, gc] + dexp_ref[:, gc] * actx_scr[:, gc]) * (zz * _sigmoid(zz))
        yv = yv * lax.rsqrt(jnp.mean(yv * yv, -1, keepdims=True) + SSM_NORM_EPS) * normg_ref[:, gc]
        part = _dot(yv.astype(BF16), wout_ref[gc, :])
        mix = part if mix is None else mix + part
    o_ref[...] = _layer_norm(DEEPNORM_ALPHA * x + mix, g_ref[...], b_ref[...])


def _ssd_kernel(x_ref, *refs, **static):
    _ssd_body(x_ref[...], *refs, **static)


def _ssd_combine_kernel(dest_ref, h_ref, route_ref, g3_ref, b3_ref, y_ref, *refs, ts, **static):
    *ssd_refs, gbuf0, gbuf1, xin_scr, gsem = refs
    i = pl.program_id(0) * pl.num_programs(1) + pl.program_id(1)
    n_tiles = pl.num_programs(0) * pl.num_programs(1)
    _gather_combine(i, n_tiles, ts, dest_ref, h_ref, route_ref, g3_ref, b3_ref, y_ref, xin_scr, gbuf0, gbuf1, gsem,
                    split_issue=True)
    _ssd_body(xin_scr[...], *ssd_refs, ts=ts, **static)


def _ssd_layer(h, w_in, conv_w, conv_b, dt_bias, a_log, d_skip, norm_g, w_out, ln_g, ln_b, pending=None):
    bsz, seq, d = h.shape
    n_heads = a_log.shape[0]
    d_inner = n_heads * SSM_HEAD_DIM
    gn = SSM_N_GROUPS * SSM_D_STATE
    conv_dim = d_inner + 2 * gn
    ts, q = SSD_TILE, SSD_CHUNK
    w_in = w_in.astype(BF16)
    wz = w_in[:, :d_inner]
    wx = w_in[:, d_inner:2 * d_inner]
    wb = w_in[:, 2 * d_inner:2 * d_inner + gn]
    wc = w_in[:, 2 * d_inner + gn:2 * d_inner + 2 * gn]
    pad = LANES - n_heads
    wdt = jnp.pad(w_in[:, d_inner + conv_dim:], ((0, 0), (0, pad)))
    dtb = jnp.pad(dt_bias.astype(F32), (0, pad)).reshape(1, LANES)
    alog = jnp.pad(a_log.astype(F32), (0, pad)).reshape(1, LANES)
    dexp = jnp.repeat(d_skip.astype(F32), SSM_HEAD_DIM).reshape(1, d_inner)
    expand = (jnp.arange(LANES)[:, None] == (jnp.arange(d_inner)[None, :] // SSM_HEAD_DIM)).astype(BF16)
    weight_specs = [_const_spec((d, d_inner)), _const_spec((d, d_inner)), _const_spec((d, gn)),
                    _const_spec((d, gn)), _const_spec((d, LANES)), _const_spec((SSM_CONV, conv_dim)),
                    _const_spec((1, conv_dim)), _const_spec((1, LANES)), _const_spec((1, LANES)),
                    _const_spec((1, d_inner)), _const_spec((1, d_inner)), _const_spec((LANES, d_inner)),
                    _const_spec((d_inner, d)), _const_spec((1, d)), _const_spec((1, d))]
    weights = (wz, wx, wb, wc, wdt, conv_w.astype(F32), conv_b.reshape(1, conv_dim), dtb, alog, dexp,
               norm_g.reshape(1, d_inner), expand, w_out.astype(BF16), ln_g.reshape(1, d), ln_b.reshape(1, d))
    scratch = [pltpu.VMEM((conv_dim // LANES, CONV_HALO + ts, LANES), F32), pltpu.VMEM((ts, d_inner), F32),
               pltpu.VMEM((ts, d_inner), F32), pltpu.VMEM((ts, gn), BF16), pltpu.VMEM((ts, gn), BF16),
               pltpu.VMEM((ts, d_inner), F32),
               pltpu.VMEM((SSM_N_GROUPS, SSM_D_STATE, d_inner // SSM_N_GROUPS), F32)]
    static = dict(ts=ts, q=q, d_inner=d_inner, gn=gn)
    out_shape = jax.ShapeDtypeStruct((bsz, seq, d), F32)
    if pending is None:
        tile = pl.BlockSpec((None, ts, d), lambda b, s: (b, s, 0))
        return pl.pallas_call(
            functools.partial(_ssd_kernel, **static),
            out_shape=out_shape,
            grid=(bsz, seq // ts),
            in_specs=[tile] + weight_specs,
            out_specs=tile,
            scratch_shapes=scratch,
            compiler_params=_params(2),
            name="ssd_mixer_ln",
        )(h, *weights)
    y, dest, route, g3, b3 = pending
    tile = pl.BlockSpec((None, ts, d), lambda b, s, dest: (b, s, 0))
    grid_spec = pltpu.PrefetchScalarGridSpec(
        num_scalar_prefetch=1,
        grid=(bsz, seq // ts),
        in_specs=[tile, pl.BlockSpec((None, ts, LANES), lambda b, s, dest: (b, s, 0)), _const_spec((1, d)),
                  _const_spec((1, d)), pl.BlockSpec(memory_space=pl.ANY)] + weight_specs,
        out_specs=tile,
        scratch_shapes=scratch + [pltpu.VMEM((TOP_K, ts * SUBLANES, LANES), F32),
                                  pltpu.VMEM((TOP_K, ts * SUBLANES, LANES), F32), pltpu.VMEM((ts, d), F32),
                                  pltpu.SemaphoreType.DMA((2,))],
    )
    return pl.pallas_call(
        functools.partial(_ssd_combine_kernel, **static),
        out_shape=out_shape,
        grid_spec=grid_spec,
        compiler_params=_params(2),
        name="combine_ln_ssd_mixer_ln",
    )(dest, h, route, g3.reshape(1, d), b3.reshape(1, d), y, *weights)


def _xattn_kernel(h_ref, mem_ref, wq_ref, wk_ref, wv_ref, wo_ref, g_ref, b_ref, wrh_ref, wrl_ref, br_ref,
                  o_ref, route_ref, route_t_ref, cnt_ref, k_scr, v_scr, carry_scr, *, ts, hd):
    b = pl.program_id(0)
    s = pl.program_id(1)

    @pl.when(s == 0)
    def _():
        mb = mem_ref[...].astype(BF16)
        k_scr[...] = _dot(mb, wk_ref[...]).astype(BF16)
        v_scr[...] = _dot(mb, wv_ref[...]).astype(BF16)

    @pl.when((b == 0) & (s == 0))
    def _():
        carry_scr[...] = jnp.zeros(carry_scr.shape, F32)

    sub = ts // XATTN_SPLIT
    halves = []
    for u in range(XATTN_SPLIT):
        rows = slice(u * sub, (u + 1) * sub)
        h = h_ref[rows, :]
        qv = (_dot(h.astype(BF16), wq_ref[...]) * (hd ** -0.5)).astype(BF16)
        heads = []
        for hh in range(XA_HEADS):
            cols = slice(hh * hd, (hh + 1) * hd)
            sc = _dot_nt(qv[:, cols], k_scr[:, cols])
            p = jnp.exp(sc - jnp.max(sc, -1, keepdims=True))
            o = _dot(p.astype(BF16), v_scr[:, cols]) / jnp.sum(p, -1, keepdims=True)
            heads.append(o.astype(BF16))
        halves.append((h, _dot(jnp.concatenate(heads, axis=1), wo_ref[...])))

    lane = lax.broadcasted_iota(jnp.int32, (sub, LANES), 1).astype(F32)
    below = (lax.broadcasted_iota(jnp.int32, (sub, sub), 0) > lax.broadcasted_iota(jnp.int32, (sub, sub), 1))
    below = jnp.where(below, 1.0, 0.0).astype(BF16)
    for u in range(XATTN_SPLIT):
        rows = slice(u * sub, (u + 1) * sub)
        h, xa = halves[u]
        h2 = _layer_norm(DEEPNORM_ALPHA * h + xa, g_ref[...], b_ref[...])
        o_ref[rows, :] = h2
        h2_hi = h2.astype(BF16)

        h2_lo = (h2 - h2_hi.astype(F32)).astype(BF16)
        logits = _dot(h2_hi, wrh_ref[...]) + _dot(h2_lo, wrh_ref[...]) + _dot(h2_hi, wrl_ref[...]) + br_ref[...]
        work = logits
        vals, idxs, sels = [], [], []
        for _ in range(TOP_K):
            m = jnp.max(work, -1, keepdims=True)
            ik = jnp.min(jnp.where(work == m, lane, float(LANES)), -1, keepdims=True)
            sel = lane == ik
            vals.append(m)
            idxs.append(ik)
            sels.append(sel)
            work = jnp.where(sel, -jnp.inf, work)
        exps = [jnp.exp(v - vals[0]) for v in vals]
        den = exps[0]
        for e in exps[1:]:
            den = den + e
        onehot = jnp.zeros((sub, LANES), F32)
        for sel in sels:
            onehot = onehot + sel.astype(F32)
        before = _dot(below, onehot.astype(BF16)) + carry_scr[0:1, :]
        route = jnp.zeros((sub, LANES), F32)
        for k in range(TOP_K):
            rank = jnp.sum(jnp.where(sels[k], before, 0.0), -1, keepdims=True)
            route = jnp.where(lane == float(k), idxs[k], route)
            route = jnp.where(lane == float(TOP_K + k), exps[k] / den, route)
            route = jnp.where(lane == float(2 * TOP_K + k), rank, route)
        route_ref[rows, :] = route
        route_t_ref[:, rows] = route.T[0:ROUTE_ROWS, :]
        carry_scr[...] = carry_scr[...] + jnp.sum(onehot, 0, keepdims=True)
    cnt_ref[...] = carry_scr[...]


def _xattn_layer(h, mem, wq, wk, wv, wo, ln_g, ln_b, w_router, b_router):
    bsz, seq, d = h.shape
    mlen = mem.shape[1]
    ts = XATTN_TILE
    hd = d // XA_HEADS
    n_exp = w_router.shape[1]
    wr = jnp.pad(w_router.astype(F32), ((0, 0), (0, LANES - n_exp)))
    wr_hi = wr.astype(BF16)
    wr_lo = (wr - wr_hi.astype(F32)).astype(BF16)
    br = jnp.pad(b_router.astype(F32), (0, LANES - n_exp), constant_values=-jnp.inf).reshape(1, LANES)
    tile = pl.BlockSpec((None, ts, d), lambda b, s: (b, s, 0))
    rtile = pl.BlockSpec((None, ts, LANES), lambda b, s: (b, s, 0))
    return pl.pallas_call(
        functools.partial(_xattn_kernel, ts=ts, hd=hd),
        out_shape=(jax.ShapeDtypeStruct((bsz, seq, d), F32), jax.ShapeDtypeStruct((bsz, seq, LANES), F32),
                   jax.ShapeDtypeStruct((ROUTE_ROWS, bsz * seq), F32), jax.ShapeDtypeStruct((SUBLANES, LANES), F32)),
        grid=(bsz, seq // ts),
        in_specs=[tile, pl.BlockSpec((None, mlen, d), lambda b, s: (b, 0, 0)),
                  _const_spec((d, d)), _const_spec((d, d)), _const_spec((d, d)), _const_spec((d, d)),
                  _const_spec((1, d)), _const_spec((1, d)), _const_spec((d, LANES)), _const_spec((d, LANES)),
                  _const_spec((1, LANES))],
        out_specs=(tile, rtile, pl.BlockSpec((ROUTE_ROWS, ts), lambda b, s: (0, b * (seq // ts) + s)),
                   pl.BlockSpec((SUBLANES, LANES), lambda b, s: (0, 0))),
        scratch_shapes=[pltpu.VMEM((mlen, d), BF16), pltpu.VMEM((mlen, d), BF16), pltpu.VMEM((SUBLANES, LANES), F32)],
        compiler_params=_params(2),
        name="xattn_ln_router",
    )(h, mem, wq.astype(BF16), wk.astype(BF16), wv.astype(BF16), wo.astype(BF16),
      ln_g.reshape(1, d), ln_b.reshape(1, d), wr_hi, wr_lo, br)


def _rows_from_tiles(ref, n):
    return jnp.concatenate([ref[pl.ds(c, n, stride=SUBLANES), :] for c in range(SUBLANES)], axis=1)


def _rows_to_tiles(ref, v, n):
    for c in range(SUBLANES):
        ref[pl.ds(c, n, stride=SUBLANES), :] = v[:, c * LANES:(c + 1) * LANES]


def _moe_kernel(be_ref, nu_ref, nv_ref, x_ref, wgu_ref, bg_ref, bu_ref, wd_ref, bd_ref, o_ref, wg_scr, wu_scr, wd_scr,
                *, tm):
    i = pl.program_id(0)
    active = i < nu_ref[0]
    new_expert = (i == 0) | (be_ref[i] != be_ref[jnp.maximum(i - 1, 0)])

    @pl.when(active & new_expert)
    def _():
        w2 = 2 * LANES
        src = lax.broadcasted_iota(jnp.int32, (w2, w2), 0)
        dst = lax.broadcasted_iota(jnp.int32, (w2, w2), 1)
        perm = jnp.where(src == jnp.where(dst < LANES, 2 * dst, 2 * (dst - LANES) + 1), 1.0, 0.0).astype(BF16)
        for c in range(wgu_ref.shape[1] // w2):
            res = _dot(wgu_ref[:, c * w2:(c + 1) * w2].astype(BF16), perm)
            wg_scr[:, c * LANES:(c + 1) * LANES] = res[:, :LANES].astype(BF16)
            wu_scr[:, c * LANES:(c + 1) * LANES] = res[:, LANES:].astype(BF16)
        wd_scr[...] = wd_ref[...].astype(BF16)

    nv = nv_ref[i]
    for r in range(MOE_ROW_GROUP, tm + 1, MOE_ROW_GROUP):
        @pl.when(active & (nv > r - MOE_ROW_GROUP) & (nv <= r))
        def _():
            x = _rows_from_tiles(x_ref, r).astype(BF16)
            gate = jnp.minimum(_dot(x, wg_scr[...]) + bg_ref[...], SWIGLU_LIMIT)
            up = jnp.clip(_dot(x, wu_scr[...]) + bu_ref[...], -SWIGLU_LIMIT, SWIGLU_LIMIT)
            act = (up + 1.0) * (gate * _sigmoid(SWIGLU_ALPHA * gate))
            _rows_to_tiles(o_ref, _dot(act.astype(BF16), wd_scr[...]) + bd_ref[...], r)
            if r < tm:
                o_ref[r * SUBLANES:, :] = jnp.zeros(((tm - r) * SUBLANES, LANES), F32)

    @pl.when(jnp.logical_not(active))
    def _():
        o_ref[...] = jnp.zeros(o_ref.shape, F32)


def _moe_experts(xs, block_e, n_used, n_valid, layer, w_gate_up, b_gate, b_up, w_down, b_down):
    _, n_exp, f, d = w_down.shape
    n_rows = xs.shape[0] // SUBLANES
    tm = MOE_TILE
    n_blocks = n_rows // tm

    def row_map(i, be, nu, nv):
        return (jnp.minimum(i, nu[0] - 1), 0)

    def w_map(i, be, nu, nv):
        return (be[i], 0, 0)

    def lw_map(i, be, nu, nv):
        return (layer, be[i], 0, 0)

    grid_spec = pltpu.PrefetchScalarGridSpec(
        num_scalar_prefetch=3,
        grid=(n_blocks,),
        in_specs=[pl.BlockSpec((tm * SUBLANES, LANES), row_map),
                  pl.BlockSpec((None, None, d, 2 * f), lw_map),
                  pl.BlockSpec((None, 1, f), w_map), pl.BlockSpec((None, 1, f), w_map),
                  pl.BlockSpec((None, None, f, d), lw_map), pl.BlockSpec((None, 1, d), w_map)],
        out_specs=pl.BlockSpec((tm * SUBLANES, LANES), lambda i, be, nu, nv: (i, 0)),
        scratch_shapes=[pltpu.VMEM((d, f), BF16), pltpu.VMEM((d, f), BF16), pltpu.VMEM((f, d), BF16)],
    )
    return pl.pallas_call(
        functools.partial(_moe_kernel, tm=tm),
        out_shape=jax.ShapeDtypeStruct((n_rows * SUBLANES, LANES), F32),
        grid_spec=grid_spec,
        compiler_params=_params(1),
        name="moe_experts",
    )(block_e, n_used, n_valid, xs, w_gate_up, b_gate.reshape(n_exp, 1, f), b_up.reshape(n_exp, 1, f),
      w_down, b_down.reshape(n_exp, 1, d))


def _row_copy(src, src_row, dst, dst_row, sem):
    return pltpu.make_async_copy(src.at[pl.ds(pl.multiple_of(src_row * SUBLANES, SUBLANES), SUBLANES)],
                                 dst.at[pl.ds(pl.multiple_of(dst_row * SUBLANES, SUBLANES), SUBLANES)], sem)


def _dispatch_kernel(dest_ref, zblk_ref, h_ref, xs_ref, stage0, stage1, zero_scr, sem, zsem, *, ts, n_tiles, tm):
    i = pl.program_id(0)

    n_fill = zblk_ref.shape[0] // 2

    def fill_wait(lo, hi, s):
        for j in range(lo, hi):
            @pl.when(zblk_ref[j] >= 0)
            def _():
                pltpu.make_async_copy(zero_scr, xs_ref.at[pl.ds(0, tm * SUBLANES)], zsem.at[s]).wait()

    @pl.when(i == 0)
    def _():
        zero_scr[...] = jnp.zeros(zero_scr.shape, F32)
        for j in range(2 * n_fill):
            @pl.when(zblk_ref[j] >= 0)
            def _():
                start = pl.multiple_of(zblk_ref[j] * (tm * SUBLANES), tm * SUBLANES)
                pltpu.make_async_copy(zero_scr, xs_ref.at[pl.ds(start, tm * SUBLANES)], zsem.at[j // n_fill]).start()
        fill_wait(0, n_fill, 0)

    @pl.when(i == n_tiles - 1)
    def _():
        fill_wait(n_fill, 2 * n_fill, 1)

    def drain(stage, s):
        for _ in range(TOP_K):
            pltpu.make_async_copy(stage, xs_ref.at[pl.ds(0, ts * SUBLANES)], sem.at[s]).wait()

    def step(stage, s, other, so):
        @pl.when(i >= 2)
        def _():
            drain(stage, s)

        _rows_to_tiles(stage, h_ref[...], ts)

        def issue(j, carry):
            for u in range(ROWS_PER_ISSUE):
                tok = j * ROWS_PER_ISSUE + u
                for k in range(TOP_K):
                    row = dest_ref[k * (n_tiles * ts) + i * ts + tok]
                    _row_copy(stage, tok, xs_ref, row, sem.at[s]).start(priority=k % 2)
            return carry

        lax.fori_loop(0, ts // ROWS_PER_ISSUE, issue, 0)

        @pl.when(i == n_tiles - 1)
        def _():
            drain(other, so)
            drain(stage, s)

    @pl.when(i % 2 == 0)
    def _():
        step(stage0, 0, stage1, 1)

    @pl.when(i % 2 == 1)
    def _():
        step(stage1, 1, stage0, 0)


def _dispatch_rows(h, dest, zero_blocks, n_rows):
    t, d = h.shape
    ts = DISPATCH_TILE
    tm = MOE_TILE
    n_tiles = t // ts
    assert d == SUBLANES * LANES and n_tiles >= 2
    grid_spec = pltpu.PrefetchScalarGridSpec(
        num_scalar_prefetch=2,
        grid=(n_tiles,),
        in_specs=[pl.BlockSpec((ts, d), lambda i, dest, zb: (i, 0))],
        out_specs=pl.BlockSpec(memory_space=pl.ANY),
        scratch_shapes=[pltpu.VMEM((ts * SUBLANES, LANES), F32), pltpu.VMEM((ts * SUBLANES, LANES), F32),
                        pltpu.VMEM((tm * SUBLANES, LANES), F32),
                        pltpu.SemaphoreType.DMA((2,)), pltpu.SemaphoreType.DMA((2,))],
    )
    return pl.pallas_call(
        functools.partial(_dispatch_kernel, ts=ts, n_tiles=n_tiles, tm=tm),
        out_shape=jax.ShapeDtypeStruct((n_rows * SUBLANES, LANES), F32),
        grid_spec=grid_spec,
        compiler_params=_params(1),
        name="moe_dispatch",
    )(dest, zero_blocks, h)


def _gather_combine(i, n_tiles, ts, dest_ref, h_ref, route_ref, g_ref, b_ref, y_ref, o_ref, gbuf0, gbuf1, sem,
                    split_issue=False):
    total = n_tiles * ts
    n_issue = ts // ROWS_PER_ISSUE
    first = n_issue // 2 if split_issue else n_issue

    def issue(tile, gbuf, s, lo=0, hi=n_issue):
        def body(j, carry):
            for u in range(ROWS_PER_ISSUE):
                tok = j * ROWS_PER_ISSUE + u
                for k in range(TOP_K):
                    row = dest_ref[k * total + tile * ts + tok]
                    _row_copy(y_ref, row, gbuf.at[k], tok, sem.at[s]).start(priority=k % 2)
            return carry

        lax.fori_loop(lo, hi, body, 0)

    def drain(gbuf, s):
        for k in range(TOP_K):
            pltpu.make_async_copy(y_ref.at[pl.ds(0, ts * SUBLANES)], gbuf.at[k], sem.at[s]).wait()

    def step(gbuf, s, nxt, sn):
        @pl.when(i + 1 < n_tiles)
        def _():
            issue(i + 1, nxt, sn, 0, first)

        drain(gbuf, s)
        parts = []
        for c in range(SUBLANES):
            acc = None
            for k in range(TOP_K):
                piece = gbuf[k, pl.ds(c, ts, stride=SUBLANES), :] * route_ref[:, TOP_K + k:TOP_K + k + 1]
                acc = piece if acc is None else acc + piece
            parts.append(acc)
        ff = jnp.concatenate(parts, axis=1)
        o_ref[...] = _layer_norm(DEEPNORM_ALPHA * h_ref[...] + ff, g_ref[...], b_ref[...])
        if split_issue:
            @pl.when(i + 1 < n_tiles)
            def _():
                issue(i + 1, nxt, sn, first, n_issue)

    @pl.when(i == 0)
    def _():
        issue(0, gbuf0, 0)

    @pl.when(i % 2 == 0)
    def _():
        step(gbuf0, 0, gbuf1, 1)

    @pl.when(i % 2 == 1)
    def _():
        step(gbuf1, 1, gbuf0, 0)


def _combine_kernel(dest_ref, h_ref, route_ref, g_ref, b_ref, y_ref, o_ref, gbuf0, gbuf1, sem, *, ts, n_tiles):
    _gather_combine(pl.program_id(0), n_tiles, ts, dest_ref, h_ref, route_ref, g_ref, b_ref, y_ref, o_ref, gbuf0, gbuf1,
                    sem)


def _combine_layer(h, y, dest, route, ln_g, ln_b):
    t, d = h.shape
    ts = COMBINE_TILE
    n_tiles = t // ts
    grid_spec = pltpu.PrefetchScalarGridSpec(
        num_scalar_prefetch=1,
        grid=(n_tiles,),
        in_specs=[pl.BlockSpec((ts, d), lambda i, dest: (i, 0)), pl.BlockSpec((ts, LANES), lambda i, dest: (i, 0)),
                  pl.BlockSpec((1, d), lambda i, dest: (0, 0)), pl.BlockSpec((1, d), lambda i, dest: (0, 0)),
                  pl.BlockSpec(memory_space=pl.ANY)],
        out_specs=pl.BlockSpec((ts, d), lambda i, dest: (i, 0)),
        scratch_shapes=[pltpu.VMEM((TOP_K, ts * SUBLANES, LANES), F32),
                        pltpu.VMEM((TOP_K, ts * SUBLANES, LANES), F32), pltpu.SemaphoreType.DMA((2,))],
    )
    return pl.pallas_call(
        functools.partial(_combine_kernel, ts=ts, n_tiles=n_tiles),
        out_shape=jax.ShapeDtypeStruct((t, d), F32),
        grid_spec=grid_spec,
        compiler_params=_params(1),
        name="moe_combine_ln",
    )(dest, h, route, ln_g.reshape(1, d), ln_b.reshape(1, d), y)


def _moe_layer(h2, route, route_t, cnt, layer, w_gate_up, b_gate_up, w_down, b_down, ln_g, ln_b, defer_combine):
    bsz, seq, d = h2.shape
    t = bsz * seq
    tm = MOE_TILE
    n_exp = w_gate_up.shape[1]
    h2 = h2.reshape(t, d)
    route = route.reshape(t, LANES)
    idx = route_t[0:TOP_K].astype(jnp.int32)
    rank = route_t[2 * TOP_K:3 * TOP_K].astype(jnp.int32)
    counts = cnt[0, :n_exp].astype(jnp.int32)
    padded = (counts + tm - 1) // tm * tm
    pend = jnp.cumsum(padded)
    pstart = pend - padded
    part = counts % tm
    skip = jnp.where(part > 0, tm - part, 0)

    def per_expert(table, e):
        return jnp.sum(jnp.where(e[..., None] == jnp.arange(n_exp, dtype=jnp.int32), table, 0), axis=-1)

    dest = (per_expert(pstart, idx) + rank
            + jnp.where(rank >= per_expert(part, idx), per_expert(skip, idx), 0)).reshape(-1)
    n_blocks = (t * TOP_K) // tm + n_exp
    n_used = (pend[-1] // tm).astype(jnp.int32)
    blk = jnp.arange(n_blocks, dtype=jnp.int32)
    block_e = jnp.sum((pend[None, :] <= (blk * tm)[:, None]).astype(jnp.int32), axis=1)
    block_e = jnp.minimum(block_e, n_exp - 1)
    block_e = jnp.where(blk < n_used, block_e, block_e[jnp.maximum(n_used - 1, 0)])
    first_blk = jnp.where(part > 0, pstart // tm, -1)
    tail_blk = n_used + jnp.arange(n_exp, dtype=jnp.int32)
    zero_blocks = jnp.concatenate([first_blk, jnp.where(tail_blk < n_blocks, tail_blk, -1)]).astype(jnp.int32)
    xs = _dispatch_rows(h2, dest, zero_blocks, n_blocks * tm)
    n_valid = jnp.where((blk == per_expert(first_blk, block_e)) & (blk < n_used), per_expert(part, block_e),
                        tm).astype(jnp.int32)
    y = _moe_experts(xs, block_e, n_used.reshape(1), n_valid, layer, w_gate_up, b_gate_up[:, 0::2], b_gate_up[:, 1::2],
                     w_down, b_down)
    if defer_combine:
        return h2.reshape(bsz, seq, d), (y, dest, route.reshape(bsz, seq, LANES), ln_g, ln_b)
    return _combine_layer(h2, y, dest, route, ln_g, ln_b).reshape(bsz, seq, d), None


def kernel(x, mem, pool_w_in, pool_w_grp, pool_scale, pool_w_out, ssm_w_in, ssm_conv_w, ssm_conv_b, ssm_dt_bias, ssm_a_log, ssm_d, ssm_norm_g, ssm_w_out, xa_wq, xa_wk, xa_wv, xa_wo, moe_w_router, moe_b_router, moe_w_gate_up, moe_b_gate_up, moe_w_down, moe_b_down, ln_mix_g, ln_mix_b, ln_xa_g, ln_xa_b, ln_ffn_g, ln_ffn_b):
    h = x
    pending = None
    for i in range(DEPTH):
        j = i // 2
        if i % 2 == 0:
            assert pending is None
            h = _pool_layer(h, pool_w_in[j], pool_w_grp[j], pool_scale[j], pool_w_out[j], ln_mix_g[i], ln_mix_b[i])
        else:
            h = _ssd_layer(h, ssm_w_in[j], ssm_conv_w[j], ssm_conv_b[j], ssm_dt_bias[j], ssm_a_log[j], ssm_d[j],
                           ssm_norm_g[j], ssm_w_out[j], ln_mix_g[i], ln_mix_b[i], pending)
        h2, route, route_t, cnt = _xattn_layer(h, mem, xa_wq[i], xa_wk[i], xa_wv[i], xa_wo[i], ln_xa_g[i], ln_xa_b[i],
                                      moe_w_router[i], moe_b_router[i])
        next_is_ssd = i + 1 < DEPTH and (i + 1) % 2 == 1
        h, pending = _moe_layer(h2, route, route_t, cnt, i, moe_w_gate_up, moe_b_gate_up[i], moe_w_down, moe_b_down[i],
                                ln_ffn_g[i], ln_ffn_b[i], next_is_ssd)
    return h
```

```python
import functools

import jax
import jax.numpy as jnp
from jax import lax
from jax.experimental import pallas as pl
from jax.experimental.pallas import tpu as pltpu

F32 = jnp.float32
BF16 = jnp.bfloat16

DEPTH = 2
DEEPNORM_ALPHA = (2 * DEPTH) ** 0.25
LN_EPS = 1e-5
POOL_WINDOWS = (2, 4, 8, 16)
POOL_HALO = 16
SSM_HEAD_DIM = 64
SSM_N_GROUPS = 8
SSM_HEADS_PER_GROUP = 4
SSM_D_STATE = 128
SSM_CONV = 4
CONV_HALO = 8
SSM_NORM_EPS = 1e-5
XA_HEADS = 4
N_EXPERTS = 32
TOP_K = 4
SWIGLU_LIMIT = 7.0
SWIGLU_ALPHA = 1.702

LANES = 128
SUBLANES = 8
VMEM_LIMIT_BYTES = 56 * 1024 * 1024

POOL_TILE = 1024
XATTN_TILE = 1024
XATTN_SPLIT = 2
SSD_TILE = 256
SSD_CHUNK = 128
SSD_STRIP = 512
MOE_TILE = 1024
MOE_ROW_GROUP = 256
DISPATCH_TILE = 512
COMBINE_TILE = 512
ROWS_PER_ISSUE = 8

ROUTE_ROWS = 16

NEG_BIG = -1e30


def _layer_norm(v, g, b):
    mu = jnp.mean(v, -1, keepdims=True)
    d = v - mu
    var = jnp.mean(d * d, -1, keepdims=True)
    return d * lax.rsqrt(var + LN_EPS) * g + b


def _dot(a, b):
    return jnp.dot(a, b, preferred_element_type=F32)


def _dot_nt(a, b):
    return lax.dot_general(a, b, (((1,), (1,)), ((), ())), preferred_element_type=F32)


def _dot_tn(a, b):
    return lax.dot_general(a, b, (((0,), (0,)), ((), ())), preferred_element_type=F32)


def _const_spec(shape):
    nd = len(shape)
    return pl.BlockSpec(shape, lambda *_: (0,) * nd, pipeline_mode=pl.Buffered(1))


def _params(n_axes):
    return pltpu.CompilerParams(dimension_semantics=("arbitrary",) * n_axes,
                                vmem_limit_bytes=VMEM_LIMIT_BYTES)


def _pool_kernel(x_ref, win_ref, wgrp_ref, scale_ref, wout_ref, g_ref, b_ref, o_ref, ext_ref, *, ts, gd):
    s = pl.program_id(1)

    @pl.when(s == 0)
    def _():
        ext_ref[0:POOL_HALO, :] = jnp.zeros((POOL_HALO, ext_ref.shape[1]), F32)

    x = x_ref[...]
    ext_ref[POOL_HALO:, :] = _dot(x.astype(BF16), win_ref[...])
    pos = lax.broadcasted_iota(jnp.int32, (ts, 1), 0) + s * ts
    mix = None
    for g, w in enumerate(POOL_WINDOWS):
        cols = slice(g * gd, (g + 1) * gd)
        e = ext_ref[:, cols]
        acc = e
        sh = 1
        while sh < w:
            acc = acc + pltpu.roll(acc, sh, axis=0)
            sh *= 2
        cnt = jnp.minimum(pos + 1, w).astype(F32)
        m = acc[POOL_HALO:, :] / cnt - e[POOL_HALO:, :]
        yg = _dot(m.astype(BF16), wgrp_ref[g]) * scale_ref[:, cols]
        part = _dot(yg.astype(BF16), wout_ref[cols, :])
        mix = part if mix is None else mix + part
    ext_ref[0:POOL_HALO, :] = ext_ref[ts:ts + POOL_HALO, :]
    o_ref[...] = _layer_norm(DEEPNORM_ALPHA * x + mix, g_ref[...], b_ref[...])


def _pool_layer(h, w_in, w_grp, scale, w_out, ln_g, ln_b):
    bsz, seq, d = h.shape
    ts = POOL_TILE
    gd = d // len(POOL_WINDOWS)
    tile = pl.BlockSpec((None, ts, d), lambda b, s: (b, s, 0))
    return pl.pallas_call(
        functools.partial(_pool_kernel, ts=ts, gd=gd),
        out_shape=jax.ShapeDtypeStruct((bsz, seq, d), F32),
        grid=(bsz, seq // ts),
        in_specs=[tile, _const_spec((d, d)), _const_spec((len(POOL_WINDOWS), gd, gd)), _const_spec((1, d)),
                  _const_spec((d, d)), _const_spec((1, d)), _const_spec((1, d))],
        out_specs=tile,
        scratch_shapes=[pltpu.VMEM((POOL_HALO + ts, d), F32)],
        compiler_params=_params(2),
        name="pool_mixer_ln",
    )(h, w_in.astype(BF16), w_grp.astype(BF16), scale.reshape(1, d), w_out.astype(BF16),
      ln_g.reshape(1, d), ln_b.reshape(1, d))


def _sigmoid(v):
    return 1.0 / (1.0 + jnp.exp(-v))


def _ssd_body(x, wz_ref, wx_ref, wb_ref, wc_ref, wdt_ref, convw_ref, convb_ref, dtb_ref, alog_ref,
              dexp_ref, normg_ref, expand_ref, wout_ref, g_ref, b_ref, o_ref,
              xbc_scr, z_scr, actx_scr, bmat_scr, cmat_scr, y_scr, state_scr, *, ts, q, d_inner, gn, per_group=None):
    s = pl.program_id(1)
    conv_dim = d_inner + 2 * gn
    gw = d_inner // SSM_N_GROUPS

    @pl.when(s == 0)
    def _():
        xbc_scr[:, 0:CONV_HALO, :] = jnp.zeros((conv_dim // LANES, CONV_HALO, LANES), F32)
        state_scr[...] = jnp.zeros(state_scr.shape, F32)

    xb = x.astype(BF16)
    dt_raw = _dot(xb, wdt_ref[...]) + dtb_ref[...]
    dtv = jnp.maximum(dt_raw, 0.0) + jnp.log1p(jnp.exp(-jnp.abs(dt_raw)))
    a_all = dtv * (-jnp.exp(alog_ref[...]))

    for w_ref, base in ((wb_ref, d_inner), (wc_ref, d_inner + gn), (wx_ref, 0)):
        for c in range(0, w_ref.shape[1], SSD_STRIP):
            res = _dot(xb, w_ref[:, c:c + SSD_STRIP])
            for jj in range(SSD_STRIP // LANES):
                j = (base + c) // LANES + jj
                cs = slice(j * LANES, (j + 1) * LANES)
                xbc_scr[j, CONV_HALO:, :] = res[:, jj * LANES:(jj + 1) * LANES]
                acc = convb_ref[:, cs]
                for k in range(SSM_CONV):
                    r0 = CONV_HALO - (SSM_CONV - 1) + k
                    acc = acc + convw_ref[k:k + 1, cs] * xbc_scr[j, r0:r0 + ts, :]
                act = acc * _sigmoid(acc)
                col = j * LANES
                if col < d_inner:
                    actx_scr[:, cs] = act
                elif col < d_inner + gn:
                    bmat_scr[:, col - d_inner:col - d_inner + LANES] = act.astype(BF16)
                else:
                    cmat_scr[:, col - d_inner - gn:col - d_inner - gn + LANES] = act.astype(BF16)
                xbc_scr[j, 0:CONV_HALO, :] = xbc_scr[j, ts:ts + CONV_HALO, :]

    expand = expand_ref[...]
    row_i = lax.broadcasted_iota(jnp.int32, (q, LANES), 0)
    causal = lax.broadcasted_iota(jnp.int32, (q, q), 0) >= lax.broadcasted_iota(jnp.int32, (q, q), 1)
    head_of_lane = lax.broadcasted_iota(jnp.int32, (q, gw), 1) // SSM_HEAD_DIM
    head_mask = [jnp.where(head_of_lane == r, 1.0, 0.0).astype(BF16) for r in range(SSM_HEADS_PER_GROUP)]

    for c in range(ts // q):
        rows = slice(c * q, (c + 1) * q)
        acs = a_all[rows, :]
        sh = 1
        while sh < q:
            acs = acs + jnp.where(row_i >= sh, pltpu.roll(acs, sh, axis=0), 0.0)
            sh *= 2
        acs_t = acs.T
        a_last = acs[q - 1:q, :]
        e_in = _dot(jnp.exp(acs).astype(BF16), expand)
        dec = _dot(jnp.exp(a_last - acs).astype(BF16), expand)
        dtx = _dot(dtv[rows, :].astype(BF16), expand)
        cd = jnp.broadcast_to(jnp.exp(a_last), (SUBLANES, LANES))
        cd_hi = cd.astype(BF16)
        cd_lo = (cd - cd_hi.astype(F32)).astype(BF16)
        cdx = (_dot(cd_hi, expand) + _dot(cd_lo, expand))[0:1, :]
        xd = actx_scr[rows, :] * dtx
        xdd = xd * dec
        xdb = xd.astype(BF16)
        for g in range(SSM_N_GROUPS):
            gc = slice(g * gw, (g + 1) * gw)
            nc = slice(g * SSM_D_STATE, (g + 1) * SSM_D_STATE)
            if c == 0:
                z_scr[:, gc] = _dot(xb, wz_ref[:, gc])
                if per_group is not None:
                    per_group(g)
            bg = bmat_scr[rows, nc]
            cg = cmat_scr[rows, nc]
            cb = _dot_nt(cg, bg)
            xg = xdb[:, gc]
            mhs, xms = [], []
            for r in range(SSM_HEADS_PER_GROUP):
                hd = g * SSM_HEADS_PER_GROUP + r
                seg = acs[:, hd:hd + 1] - acs_t[hd:hd + 1, :]
                lmat = jnp.exp(jnp.where(causal, seg, NEG_BIG))
                mhs.append((cb * lmat).astype(BF16))
                xms.append(xg * head_mask[r])
            yg = (_dot(cg, state_scr[g].astype(BF16)) * e_in[:, gc]
                  + _dot(jnp.concatenate(mhs, axis=1), jnp.concatenate(xms, axis=0)))
            y_scr[rows, gc] = yg
            state_scr[g] = state_scr[g] * cdx[:, gc] + _dot_tn(bg, xdd[:, gc].astype(BF16))

    mix = None
    for g in range(SSM_N_GROUPS):
        gc = slice(g * gw, (g + 1) * gw)
        zz = z_scr[:, gc]
        yv = (y_scr[:, gc] + dexp_ref[:, gc] * actx_scr[:, gc]) * (zz * _sigmoid(zz))
        yv = yv * lax.rsqrt(jnp.mean(yv * yv, -1, keepdims=True) + SSM_NORM_EPS) * normg_ref[:, gc]
        part = _dot(yv.astype(BF16), wout_ref[gc, :])
        mix = part if mix is None else mix + part
    o_ref[...] = _layer_norm(DEEPNORM_ALPHA * x + mix, g_ref[...], b_ref[...])


def _ssd_kernel(x_ref, *refs, **static):
    _ssd_body(x_ref[...], *refs, **static)


def _ssd_combine_kernel(dest_ref, h_ref, route_ref, g3_ref, b3_ref, y_ref, *refs, ts, **static):
    *ssd_refs, gbuf, gsem = refs
    i = pl.program_id(0) * pl.num_programs(1) + pl.program_id(1)
    n_tiles = pl.num_programs(0) * pl.num_programs(1)
    total = n_tiles * ts
    cur = i % 2
    nxt = 1 - cur
    nxt_tile = jnp.minimum(i + 1, n_tiles - 1)

    def start_rows(tile, slot, lo, hi):
        for tok in range(lo, hi):
            for k in range(TOP_K):
                row = dest_ref[k * total + tile * ts + tok]
                pltpu.make_async_copy(y_ref.at[pl.ds(pl.multiple_of(row * SUBLANES, SUBLANES), SUBLANES)],
                                      gbuf.at[slot, k, pl.ds(tok * SUBLANES, SUBLANES)],
                                      gsem.at[slot]).start(priority=k % 2)

    def drain(slot):
        for k in range(TOP_K):
            pltpu.make_async_copy(y_ref.at[pl.ds(0, ts * SUBLANES)], gbuf.at[slot, k], gsem.at[slot]).wait()

    @pl.when(i == 0)
    def _():
        def body(j, carry):
            for u in range(ROWS_PER_ISSUE):
                tok = j * ROWS_PER_ISSUE + u
                for k in range(TOP_K):
                    row = dest_ref[k * total + tok]
                    _row_copy(y_ref, row, gbuf.at[0, k], tok, gsem.at[0]).start(priority=k % 2)
            return carry
        lax.fori_loop(0, ts // ROWS_PER_ISSUE, body, 0)

    drain(cur)
    parts = []
    for c in range(SUBLANES):
        acc = None
        for k in range(TOP_K):
            piece = gbuf[cur, k, pl.ds(c, ts, stride=SUBLANES), :] * route_ref[:, TOP_K + k:TOP_K + k + 1]
            acc = piece if acc is None else acc + piece
        parts.append(acc)
    ff = jnp.concatenate(parts, axis=1)
    x = _layer_norm(DEEPNORM_ALPHA * h_ref[...] + ff, g3_ref[...], b3_ref[...])
    per = ts // SSM_N_GROUPS
    _ssd_body(x, *ssd_refs, ts=ts, per_group=lambda g: start_rows(nxt_tile, nxt, g * per, (g + 1) * per), **static)

    @pl.when(i == n_tiles - 1)
    def _():
        drain(nxt)


def _ssd_layer(h, w_in, conv_w, conv_b, dt_bias, a_log, d_skip, norm_g, w_out, ln_g, ln_b, pending=None):
    bsz, seq, d = h.shape
    n_heads = a_log.shape[0]
    d_inner = n_heads * SSM_HEAD_DIM
    gn = SSM_N_GROUPS * SSM_D_STATE
    conv_dim = d_inner + 2 * gn
    ts, q = SSD_TILE, SSD_CHUNK
    w_in = w_in.astype(BF16)
    wz = w_in[:, :d_inner]
    wx = w_in[:, d_inner:2 * d_inner]
    wb = w_in[:, 2 * d_inner:2 * d_inner + gn]
    wc = w_in[:, 2 * d_inner + gn:2 * d_inner + 2 * gn]
    pad = LANES - n_heads
    wdt = jnp.pad(w_in[:, d_inner + conv_dim:], ((0, 0), (0, pad)))
    dtb = jnp.pad(dt_bias.astype(F32), (0, pad)).reshape(1, LANES)
    alog = jnp.pad(a_log.astype(F32), (0, pad)).reshape(1, LANES)
    dexp = jnp.repeat(d_skip.astype(F32), SSM_HEAD_DIM).reshape(1, d_inner)
    expand = (jnp.arange(LANES)[:, None] == (jnp.arange(d_inner)[None, :] // SSM_HEAD_DIM)).astype(BF16)
    weight_specs = [_const_spec((d, d_inner)), _const_spec((d, d_inner)), _const_spec((d, gn)),
                    _const_spec((d, gn)), _const_spec((d, LANES)), _const_spec((SSM_CONV, conv_dim)),
                    _const_spec((1, conv_dim)), _const_spec((1, LANES)), _const_spec((1, LANES)),
                    _const_spec((1, d_inner)), _const_spec((1, d_inner)), _const_spec((LANES, d_inner)),
                    _const_spec((d_inner, d)), _const_spec((1, d)), _const_spec((1, d))]
    weights = (wz, wx, wb, wc, wdt, conv_w.astype(F32), conv_b.reshape(1, conv_dim), dtb, alog, dexp,
               norm_g.reshape(1, d_inner), expand, w_out.astype(BF16), ln_g.reshape(1, d), ln_b.reshape(1, d))
    scratch = [pltpu.VMEM((conv_dim // LANES, CONV_HALO + ts, LANES), F32), pltpu.VMEM((ts, d_inner), F32),
               pltpu.VMEM((ts, d_inner), F32), pltpu.VMEM((ts, gn), BF16), pltpu.VMEM((ts, gn), BF16),
               pltpu.VMEM((ts, d_inner), F32),
               pltpu.VMEM((SSM_N_GROUPS, SSM_D_STATE, d_inner // SSM_N_GROUPS), F32)]
    static = dict(ts=ts, q=q, d_inner=d_inner, gn=gn)
    out_shape = jax.ShapeDtypeStruct((bsz, seq, d), F32)
    if pending is None:
        tile = pl.BlockSpec((None, ts, d), lambda b, s: (b, s, 0))
        return pl.pallas_call(
            functools.partial(_ssd_kernel, **static),
            out_shape=out_shape,
            grid=(bsz, seq // ts),
            in_specs=[tile] + weight_specs,
            out_specs=tile,
            scratch_shapes=scratch,
            compiler_params=_params(2),
            name="ssd_mixer_ln",
        )(h, *weights)
    y, dest, route, g3, b3 = pending
    tile = pl.BlockSpec((None, ts, d), lambda b, s, dest: (b, s, 0))
    grid_spec = pltpu.PrefetchScalarGridSpec(
        num_scalar_prefetch=1,
        grid=(bsz, seq // ts),
        in_specs=[tile, pl.BlockSpec((None, ts, LANES), lambda b, s, dest: (b, s, 0)), _const_spec((1, d)),
                  _const_spec((1, d)), pl.BlockSpec(memory_space=pl.ANY)] + weight_specs,
        out_specs=tile,
        scratch_shapes=scratch + [pltpu.VMEM((2, TOP_K, ts * SUBLANES, LANES), F32), pltpu.SemaphoreType.DMA((2,))],
    )
    return pl.pallas_call(
        functools.partial(_ssd_combine_kernel, **static),
        out_shape=out_shape,
        grid_spec=grid_spec,
        compiler_params=_params(2),
        name="combine_ln_ssd_mixer_ln",
    )(dest, h, route, g3.reshape(1, d), b3.reshape(1, d), y, *weights)


def _xattn_kernel(h_ref, mem_ref, wq_ref, wk_ref, wv_ref, wo_ref, g_ref, b_ref, wrh_ref, wrl_ref, br_ref,
                  o_ref, route_ref, route_t_ref, cnt_ref, k_scr, v_scr, carry_scr, *, ts, hd):
    b = pl.program_id(0)
    s = pl.program_id(1)

    @pl.when(s == 0)
    def _():
        mb = mem_ref[...].astype(BF16)
        k_scr[...] = _dot(mb, wk_ref[...]).astype(BF16)
        v_scr[...] = _dot(mb, wv_ref[...]).astype(BF16)

    @pl.when((b == 0) & (s == 0))
    def _():
        carry_scr[...] = jnp.zeros(carry_scr.shape, F32)

    sub = ts // XATTN_SPLIT
    halves = []
    for u in range(XATTN_SPLIT):
        rows = slice(u * sub, (u + 1) * sub)
        h = h_ref[rows, :]
        qv = (_dot(h.astype(BF16), wq_ref[...]) * (hd ** -0.5)).astype(BF16)
        heads = []
        for hh in range(XA_HEADS):
            cols = slice(hh * hd, (hh + 1) * hd)
            sc = _dot_nt(qv[:, cols], k_scr[:, cols])
            p = jnp.exp(sc - jnp.max(sc, -1, keepdims=True))
            o = _dot(p.astype(BF16), v_scr[:, cols]) / jnp.sum(p, -1, keepdims=True)
            heads.append(o.astype(BF16))
        halves.append((h, _dot(jnp.concatenate(heads, axis=1), wo_ref[...])))

    lane = lax.broadcasted_iota(jnp.int32, (sub, LANES), 1).astype(F32)
    below = (lax.broadcasted_iota(jnp.int32, (sub, sub), 0) > lax.broadcasted_iota(jnp.int32, (sub, sub), 1))
    below = jnp.where(below, 1.0, 0.0).astype(BF16)
    for u in range(XATTN_SPLIT):
        rows = slice(u * sub, (u + 1) * sub)
        h, xa = halves[u]
        h2 = _layer_norm(DEEPNORM_ALPHA * h + xa, g_ref[...], b_ref[...])
        o_ref[rows, :] = h2
        h2_hi = h2.astype(BF16)

        h2_lo = (h2 - h2_hi.astype(F32)).astype(BF16)
        logits = _dot(h2_hi, wrh_ref[...]) + _dot(h2_lo, wrh_ref[...]) + _dot(h2_hi, wrl_ref[...]) + br_ref[...]
        work = logits
        vals, idxs, sels = [], [], []
        for _ in range(TOP_K):
            m = jnp.max(work, -1, keepdims=True)
            ik = jnp.min(jnp.where(work == m, lane, float(LANES)), -1, keepdims=True)
            sel = lane == ik
            vals.append(m)
            idxs.append(ik)
            sels.append(sel)
            work = jnp.where(sel, -jnp.inf, work)
        exps = [jnp.exp(v - vals[0]) for v in vals]
        den = exps[0]
        for e in exps[1:]:
            den = den + e
        onehot = jnp.zeros((sub, LANES), F32)
        for sel in sels:
            onehot = onehot + sel.astype(F32)
        before = _dot(below, onehot.astype(BF16)) + carry_scr[0:1, :]
        route = jnp.zeros((sub, LANES), F32)
        for k in range(TOP_K):
            rank = jnp.sum(jnp.where(sels[k], before, 0.0), -1, keepdims=True)
            route = jnp.where(lane == float(k), idxs[k], route)
            route = jnp.where(lane == float(TOP_K + k), exps[k] / den, route)
            route = jnp.where(lane == float(2 * TOP_K + k), rank, route)
        route_ref[rows, :] = route
        route_t_ref[:, rows] = route.T[0:ROUTE_ROWS, :]
        carry_scr[...] = carry_scr[...] + jnp.sum(onehot, 0, keepdims=True)
    cnt_ref[...] = carry_scr[...]


def _xattn_layer(h, mem, wq, wk, wv, wo, ln_g, ln_b, w_router, b_router):
    bsz, seq, d = h.shape
    mlen = mem.shape[1]
    ts = XATTN_TILE
    hd = d // XA_HEADS
    n_exp = w_router.shape[1]
    wr = jnp.pad(w_router.astype(F32), ((0, 0), (0, LANES - n_exp)))
    wr_hi = wr.astype(BF16)
    wr_lo = (wr - wr_hi.astype(F32)).astype(BF16)
    br = jnp.pad(b_router.astype(F32), (0, LANES - n_exp), constant_values=-jnp.inf).reshape(1, LANES)
    tile = pl.BlockSpec((None, ts, d), lambda b, s: (b, s, 0))
    rtile = pl.BlockSpec((None, ts, LANES), lambda b, s: (b, s, 0))
    return pl.pallas_call(
        functools.partial(_xattn_kernel, ts=ts, hd=hd),
        out_shape=(jax.ShapeDtypeStruct((bsz, seq, d), F32), jax.ShapeDtypeStruct((bsz, seq, LANES), F32),
                   jax.ShapeDtypeStruct((ROUTE_ROWS, bsz * seq), F32), jax.ShapeDtypeStruct((SUBLANES, LANES), F32)),
        grid=(bsz, seq // ts),
        in_specs=[tile, pl.BlockSpec((None, mlen, d), lambda b, s: (b, 0, 0)),
                  _const_spec((d, d)), _const_spec((d, d)), _const_spec((d, d)), _const_spec((d, d)),
                  _const_spec((1, d)), _const_spec((1, d)), _const_spec((d, LANES)), _const_spec((d, LANES)),
                  _const_spec((1, LANES))],
        out_specs=(tile, rtile, pl.BlockSpec((ROUTE_ROWS, ts), lambda b, s: (0, b * (seq // ts) + s)),
                   pl.BlockSpec((SUBLANES, LANES), lambda b, s: (0, 0))),
        scratch_shapes=[pltpu.VMEM((mlen, d), BF16), pltpu.VMEM((mlen, d), BF16), pltpu.VMEM((SUBLANES, LANES), F32)],
        compiler_params=_params(2),
        name="xattn_ln_router",
    )(h, mem, wq.astype(BF16), wk.astype(BF16), wv.astype(BF16), wo.astype(BF16),
      ln_g.reshape(1, d), ln_b.reshape(1, d), wr_hi, wr_lo, br)


def _rows_from_tiles(ref, n):
    return jnp.concatenate([ref[pl.ds(c, n, stride=SUBLANES), :] for c in range(SUBLANES)], axis=1)


def _rows_to_tiles(ref, v, n):
    for c in range(SUBLANES):
        ref[pl.ds(c, n, stride=SUBLANES), :] = v[:, c * LANES:(c + 1) * LANES]


def _moe_kernel(be_ref, nu_ref, nv_ref, x_ref, wgu_ref, bg_ref, bu_ref, wd_ref, bd_ref, o_ref, wg_scr, wu_scr, wd_scr,
                *, tm):
    i = pl.program_id(0)
    active = i < nu_ref[0]
    new_expert = (i == 0) | (be_ref[i] != be_ref[jnp.maximum(i - 1, 0)])

    @pl.when(active & new_expert)
    def _():
        w2 = 2 * LANES
        src = lax.broadcasted_iota(jnp.int32, (w2, w2), 0)
        dst = lax.broadcasted_iota(jnp.int32, (w2, w2), 1)
        perm = jnp.where(src == jnp.where(dst < LANES, 2 * dst, 2 * (dst - LANES) + 1), 1.0, 0.0).astype(BF16)
        for c in range(wgu_ref.shape[1] // w2):
            res = _dot(wgu_ref[:, c * w2:(c + 1) * w2].astype(BF16), perm)
            wg_scr[:, c * LANES:(c + 1) * LANES] = res[:, :LANES].astype(BF16)
            wu_scr[:, c * LANES:(c + 1) * LANES] = res[:, LANES:].astype(BF16)
        wd_scr[...] = wd_ref[...].astype(BF16)

    nv = nv_ref[i]
    for r in range(MOE_ROW_GROUP, tm + 1, MOE_ROW_GROUP):
        @pl.when(active & (nv > r - MOE_ROW_GROUP) & (nv <= r))
        def _():
            x = _rows_from_tiles(x_ref, r).astype(BF16)
            gate = jnp.minimum(_dot(x, wg_scr[...]) + bg_ref[...], SWIGLU_LIMIT)
            up = jnp.clip(_dot(x, wu_scr[...]) + bu_ref[...], -SWIGLU_LIMIT, SWIGLU_LIMIT)
            act = (up + 1.0) * (gate * _sigmoid(SWIGLU_ALPHA * gate))
            _rows_to_tiles(o_ref, _dot(act.astype(BF16), wd_scr[...]) + bd_ref[...], r)
            if r < tm:
                o_ref[r * SUBLANES:, :] = jnp.zeros(((tm - r) * SUBLANES, LANES), F32)

    @pl.when(jnp.logical_not(active))
    def _():
        o_ref[...] = jnp.zeros(o_ref.shape, F32)


def _moe_experts(xs, block_e, n_used, n_valid, layer, w_gate_up, b_gate, b_up, w_down, b_down):
    _, n_exp, f, d = w_down.shape
    n_rows = xs.shape[0] // SUBLANES
    tm = MOE_TILE
    n_blocks = n_rows // tm

    def row_map(i, be, nu, nv):
        return (jnp.minimum(i, nu[0] - 1), 0)

    def w_map(i, be, nu, nv):
        return (be[i], 0, 0)

    def lw_map(i, be, nu, nv):
        return (layer, be[i], 0, 0)

    grid_spec = pltpu.PrefetchScalarGridSpec(
        num_scalar_prefetch=3,
        grid=(n_blocks,),
        in_specs=[pl.BlockSpec((tm * SUBLANES, LANES), row_map),
                  pl.BlockSpec((None, None, d, 2 * f), lw_map),
                  pl.BlockSpec((None, 1, f), w_map), pl.BlockSpec((None, 1, f), w_map),
                  pl.BlockSpec((None, None, f, d), lw_map), pl.BlockSpec((None, 1, d), w_map)],
        out_specs=pl.BlockSpec((tm * SUBLANES, LANES), lambda i, be, nu, nv: (i, 0)),
        scratch_shapes=[pltpu.VMEM((d, f), BF16), pltpu.VMEM((d, f), BF16), pltpu.VMEM((f, d), BF16)],
    )
    return pl.pallas_call(
        functools.partial(_moe_kernel, tm=tm),
        out_shape=jax.ShapeDtypeStruct((n_rows * SUBLANES, LANES), F32),
        grid_spec=grid_spec,
        compiler_params=_params(1),
        name="moe_experts",
    )(block_e, n_used, n_valid, xs, w_gate_up, b_gate.reshape(n_exp, 1, f), b_up.reshape(n_exp, 1, f),
      w_down, b_down.reshape(n_exp, 1, d))


def _row_copy(src, src_row, dst, dst_row, sem):
    return pltpu.make_async_copy(src.at[pl.ds(pl.multiple_of(src_row * SUBLANES, SUBLANES), SUBLANES)],
                                 dst.at[pl.ds(pl.multiple_of(dst_row * SUBLANES, SUBLANES), SUBLANES)], sem)


def _dispatch_kernel(dest_ref, zblk_ref, h_ref, xs_ref, stage0, stage1, zero_scr, sem, zsem, *, ts, n_tiles, tm):
    i = pl.program_id(0)

    n_fill = zblk_ref.shape[0] // 2

    def fill_wait(lo, hi, s):
        for j in range(lo, hi):
            @pl.when(zblk_ref[j] >= 0)
            def _():
                pltpu.make_async_copy(zero_scr, xs_ref.at[pl.ds(0, tm * SUBLANES)], zsem.at[s]).wait()

    @pl.when(i == 0)
    def _():
        zero_scr[...] = jnp.zeros(zero_scr.shape, F32)
        for j in range(2 * n_fill):
            @pl.when(zblk_ref[j] >= 0)
            def _():
                start = pl.multiple_of(zblk_ref[j] * (tm * SUBLANES), tm * SUBLANES)
                pltpu.make_async_copy(zero_scr, xs_ref.at[pl.ds(start, tm * SUBLANES)], zsem.at[j // n_fill]).start()
        fill_wait(0, n_fill, 0)

    @pl.when(i == n_tiles - 1)
    def _():
        fill_wait(n_fill, 2 * n_fill, 1)

    def drain(stage, s):
        for _ in range(TOP_K):
            pltpu.make_async_copy(stage, xs_ref.at[pl.ds(0, ts * SUBLANES)], sem.at[s]).wait()

    def step(stage, s, other, so):
        @pl.when(i >= 2)
        def _():
            drain(stage, s)

        _rows_to_tiles(stage, h_ref[...], ts)

        def issue(j, carry):
            for u in range(ROWS_PER_ISSUE):
                tok = j * ROWS_PER_ISSUE + u
                for k in range(TOP_K):
                    row = dest_ref[k * (n_tiles * ts) + i * ts + tok]
                    _row_copy(stage, tok, xs_ref, row, sem.at[s]).start(priority=k % 2)
            return carry

        lax.fori_loop(0, ts // ROWS_PER_ISSUE, issue, 0)

        @pl.when(i == n_tiles - 1)
        def _():
            drain(other, so)
            drain(stage, s)

    @pl.when(i % 2 == 0)
    def _():
        step(stage0, 0, stage1, 1)

    @pl.when(i % 2 == 1)
    def _():
        step(stage1, 1, stage0, 0)


def _dispatch_rows(h, dest, zero_blocks, n_rows):
    t, d = h.shape
    ts = DISPATCH_TILE
    tm = MOE_TILE
    n_tiles = t // ts
    assert d == SUBLANES * LANES and n_tiles >= 2
    grid_spec = pltpu.PrefetchScalarGridSpec(
        num_scalar_prefetch=2,
        grid=(n_tiles,),
        in_specs=[pl.BlockSpec((ts, d), lambda i, dest, zb: (i, 0))],
        out_specs=pl.BlockSpec(memory_space=pl.ANY),
        scratch_shapes=[pltpu.VMEM((ts * SUBLANES, LANES), F32), pltpu.VMEM((ts * SUBLANES, LANES), F32),
                        pltpu.VMEM((tm * SUBLANES, LANES), F32),
                        pltpu.SemaphoreType.DMA((2,)), pltpu.SemaphoreType.DMA((2,))],
    )
    return pl.pallas_call(
        functools.partial(_dispatch_kernel, ts=ts, n_tiles=n_tiles, tm=tm),
        out_shape=jax.ShapeDtypeStruct((n_rows * SUBLANES, LANES), F32),
        grid_spec=grid_spec,
        compiler_params=_params(1),
        name="moe_dispatch",
    )(dest, zero_blocks, h)


def _gather_combine(i, n_tiles, ts, dest_ref, h_ref, route_ref, g_ref, b_ref, y_ref, o_ref, gbuf0, gbuf1, sem):
    total = n_tiles * ts

    def issue(tile, gbuf, s):
        def body(j, carry):
            for u in range(ROWS_PER_ISSUE):
                tok = j * ROWS_PER_ISSUE + u
                for k in range(TOP_K):
                    row = dest_ref[k * total + tile * ts + tok]
                    _row_copy(y_ref, row, gbuf.at[k], tok, sem.at[s]).start(priority=k % 2)
            return carry

        lax.fori_loop(0, ts // ROWS_PER_ISSUE, body, 0)

    def drain(gbuf, s):
        for k in range(TOP_K):
            pltpu.make_async_copy(y_ref.at[pl.ds(0, ts * SUBLANES)], gbuf.at[k], sem.at[s]).wait()

    def step(gbuf, s, nxt, sn):
        @pl.when(i + 1 < n_tiles)
        def _():
            issue(i + 1, nxt, sn)

        drain(gbuf, s)
        parts = []
        for c in range(SUBLANES):
            acc = None
            for k in range(TOP_K):
                piece = gbuf[k, pl.ds(c, ts, stride=SUBLANES), :] * route_ref[:, TOP_K + k:TOP_K + k + 1]
                acc = piece if acc is None else acc + piece
            parts.append(acc)
        ff = jnp.concatenate(parts, axis=1)
        o_ref[...] = _layer_norm(DEEPNORM_ALPHA * h_ref[...] + ff, g_ref[...], b_ref[...])

    @pl.when(i == 0)
    def _():
        issue(0, gbuf0, 0)

    @pl.when(i % 2 == 0)
    def _():
        step(gbuf0, 0, gbuf1, 1)

    @pl.when(i % 2 == 1)
    def _():
        step(gbuf1, 1, gbuf0, 0)


def _combine_kernel(dest_ref, h_ref, route_ref, g_ref, b_ref, y_ref, o_ref, gbuf0, gbuf1, sem, *, ts, n_tiles):
    _gather_combine(pl.program_id(0), n_tiles, ts, dest_ref, h_ref, route_ref, g_ref, b_ref, y_ref, o_ref, gbuf0, gbuf1,
                    sem)


def _combine_layer(h, y, dest, route, ln_g, ln_b):
    t, d = h.shape
    ts = COMBINE_TILE
    n_tiles = t // ts
    grid_spec = pltpu.PrefetchScalarGridSpec(
        num_scalar_prefetch=1,
        grid=(n_tiles,),
        in_specs=[pl.BlockSpec((ts, d), lambda i, dest: (i, 0)), pl.BlockSpec((ts, LANES), lambda i, dest: (i, 0)),
                  pl.BlockSpec((1, d), lambda i, dest: (0, 0)), pl.BlockSpec((1, d), lambda i, dest: (0, 0)),
                  pl.BlockSpec(memory_space=pl.ANY)],
        out_specs=pl.BlockSpec((ts, d), lambda i, dest: (i, 0)),
        scratch_shapes=[pltpu.VMEM((TOP_K, ts * SUBLANES, LANES), F32),
                        pltpu.VMEM((TOP_K, ts * SUBLANES, LANES), F32), pltpu.SemaphoreType.DMA((2,))],
    )
    return pl.pallas_call(
        functools.partial(_combine_kernel, ts=ts, n_tiles=n_tiles),
        out_shape=jax.ShapeDtypeStruct((t, d), F32),
        grid_spec=grid_spec,
        compiler_params=_params(1),
        name="moe_combine_ln",
    )(dest, h, route, ln_g.reshape(1, d), ln_b.reshape(1, d), y)


def _moe_layer(h2, route, route_t, cnt, layer, w_gate_up, b_gate_up, w_down, b_down, ln_g, ln_b, defer_combine):
    bsz, seq, d = h2.shape
    t = bsz * seq
    tm = MOE_TILE
    n_exp = w_gate_up.shape[1]
    h2 = h2.reshape(t, d)
    route = route.reshape(t, LANES)
    idx = route_t[0:TOP_K].astype(jnp.int32)
    rank = route_t[2 * TOP_K:3 * TOP_K].astype(jnp.int32)
    counts = cnt[0, :n_exp].astype(jnp.int32)
    padded = (counts + tm - 1) // tm * tm
    pend = jnp.cumsum(padded)
    pstart = pend - padded
    part = counts % tm
    skip = jnp.where(part > 0, tm - part, 0)

    def per_expert(table, e):
        return jnp.sum(jnp.where(e[..., None] == jnp.arange(n_exp, dtype=jnp.int32), table, 0), axis=-1)

    dest = (per_expert(pstart, idx) + rank
            + jnp.where(rank >= per_expert(part, idx), per_expert(skip, idx), 0)).reshape(-1)
    n_blocks = (t * TOP_K) // tm + n_exp
    n_used = (pend[-1] // tm).astype(jnp.int32)
    blk = jnp.arange(n_blocks, dtype=jnp.int32)
    block_e = jnp.sum((pend[None, :] <= (blk * tm)[:, None]).astype(jnp.int32), axis=1)
    block_e = jnp.minimum(block_e, n_exp - 1)
    block_e = jnp.where(blk < n_used, block_e, block_e[jnp.maximum(n_used - 1, 0)])
    first_blk = jnp.where(part > 0, pstart // tm, -1)
    tail_blk = n_used + jnp.arange(n_exp, dtype=jnp.int32)
    zero_blocks = jnp.concatenate([first_blk, jnp.where(tail_blk < n_blocks, tail_blk, -1)]).astype(jnp.int32)
    xs = _dispatch_rows(h2, dest, zero_blocks, n_blocks * tm)
    n_valid = jnp.where((blk == per_expert(first_blk, block_e)) & (blk < n_used), per_expert(part, block_e),
                        tm).astype(jnp.int32)
    y = _moe_experts(xs, block_e, n_used.reshape(1), n_valid, layer, w_gate_up, b_gate_up[:, 0::2], b_gate_up[:, 1::2],
                     w_down, b_down)
    if defer_combine:
        return h2.reshape(bsz, seq, d), (y, dest, route.reshape(bsz, seq, LANES), ln_g, ln_b)
    return _combine_layer(h2, y, dest, route, ln_g, ln_b).reshape(bsz, seq, d), None


def kernel(x, mem, pool_w_in, pool_w_grp, pool_scale, pool_w_out, ssm_w_in, ssm_conv_w, ssm_conv_b, ssm_dt_bias, ssm_a_log, ssm_d, ssm_norm_g, ssm_w_out, xa_wq, xa_wk, xa_wv, xa_wo, moe_w_router, moe_b_router, moe_w_gate_up, moe_b_gate_up, moe_w_down, moe_b_down, ln_mix_g, ln_mix_b, ln_xa_g, ln_xa_b, ln_ffn_g, ln_ffn_b):
    h = x
    pending = None
    for i in range(DEPTH):
        j = i // 2
        if i % 2 == 0:
            assert pending is None
            h = _pool_layer(h, pool_w_in[j], pool_w_grp[j], pool_scale[j], pool_w_out[j], ln_mix_g[i], ln_mix_b[i])
        else:
            h = _ssd_layer(h, ssm_w_in[j], ssm_conv_w[j], ssm_conv_b[j], ssm_dt_bias[j], ssm_a_log[j], ssm_d[j],
                           ssm_norm_g[j], ssm_w_out[j], ln_mix_g[i], ln_mix_b[i], pending)
        h2, route, route_t, cnt = _xattn_layer(h, mem, xa_wq[i], xa_wk[i], xa_wv[i], xa_wo[i], ln_xa_g[i], ln_xa_b[i],
                                      moe_w_router[i], moe_b_router[i])
        next_is_ssd = i + 1 < DEPTH and (i + 1) % 2 == 1
        h, pending = _moe_layer(h2, route, route_t, cnt, i, moe_w_gate_up, moe_b_gate_up[i], moe_w_down, moe_b_down[i],
                                ln_ffn_g[i], ln_ffn_b[i], next_is_ssd)
    return h
```

```python
import functools

import jax
import jax.numpy as jnp
from jax import lax
from jax.experimental import pallas as pl
from jax.experimental.pallas import tpu as pltpu

F32 = jnp.float32
BF16 = jnp.bfloat16

DEPTH = 2
DEEPNORM_ALPHA = (2 * DEPTH) ** 0.25
LN_EPS = 1e-5
POOL_WINDOWS = (2, 4, 8, 16)
POOL_HALO = 16
SSM_HEAD_DIM = 64
SSM_N_GROUPS = 8
SSM_HEADS_PER_GROUP = 4
SSM_D_STATE = 128
SSM_CONV = 4
CONV_HALO = 8
SSM_NORM_EPS = 1e-5
XA_HEADS = 4
N_EXPERTS = 32
TOP_K = 4
SWIGLU_LIMIT = 7.0
SWIGLU_ALPHA = 1.702

LANES = 128
SUBLANES = 8
VMEM_LIMIT_BYTES = 56 * 1024 * 1024

POOL_TILE = 1024
XATTN_TILE = 1024
XATTN_SPLIT = 2
SSD_TILE = 256
SSD_CHUNK = 128
SSD_STRIP = 512
MOE_TILE = 1024
MOE_ROW_GROUP = 256
DISPATCH_TILE = 512
COMBINE_TILE = 512
ROWS_PER_ISSUE = 8

ROUTE_ROWS = 16

NEG_BIG = -1e30


def _layer_norm(v, g, b):
    mu = jnp.mean(v, -1, keepdims=True)
    d = v - mu
    var = jnp.mean(d * d, -1, keepdims=True)
    return d * lax.rsqrt(var + LN_EPS) * g + b


def _dot(a, b):
    return jnp.dot(a, b, preferred_element_type=F32)


def _dot_nt(a, b):
    return lax.dot_general(a, b, (((1,), (1,)), ((), ())), preferred_element_type=F32)


def _dot_tn(a, b):
    return lax.dot_general(a, b, (((0,), (0,)), ((), ())), preferred_element_type=F32)


def _const_spec(shape):
    nd = len(shape)
    return pl.BlockSpec(shape, lambda *_: (0,) * nd, pipeline_mode=pl.Buffered(1))


def _params(n_axes):
    return pltpu.CompilerParams(dimension_semantics=("arbitrary",) * n_axes,
                                vmem_limit_bytes=VMEM_LIMIT_BYTES)


def _pool_kernel(x_ref, win_ref, wgrp_ref, scale_ref, wout_ref, g_ref, b_ref, o_ref, ext_ref, *, ts, gd):
    s = pl.program_id(1)

    @pl.when(s == 0)
    def _():
        ext_ref[0:POOL_HALO, :] = jnp.zeros((POOL_HALO, ext_ref.shape[1]), F32)

    x = x_ref[...]
    ext_ref[POOL_HALO:, :] = _dot(x.astype(BF16), win_ref[...])
    pos = lax.broadcasted_iota(jnp.int32, (ts, 1), 0) + s * ts
    mix = None
    for g, w in enumerate(POOL_WINDOWS):
        cols = slice(g * gd, (g + 1) * gd)
        e = ext_ref[:, cols]
        acc = e
        sh = 1
        while sh < w:
            acc = acc + pltpu.roll(acc, sh, axis=0)
            sh *= 2
        cnt = jnp.minimum(pos + 1, w).astype(F32)
        m = acc[POOL_HALO:, :] / cnt - e[POOL_HALO:, :]
        yg = _dot(m.astype(BF16), wgrp_ref[g]) * scale_ref[:, cols]
        part = _dot(yg.astype(BF16), wout_ref[cols, :])
        mix = part if mix is None else mix + part
    ext_ref[0:POOL_HALO, :] = ext_ref[ts:ts + POOL_HALO, :]
    o_ref[...] = _layer_norm(DEEPNORM_ALPHA * x + mix, g_ref[...], b_ref[...])


def _pool_layer(h, w_in, w_grp, scale, w_out, ln_g, ln_b):
    bsz, seq, d = h.shape
    ts = POOL_TILE
    gd = d // len(POOL_WINDOWS)
    tile = pl.BlockSpec((None, ts, d), lambda b, s: (b, s, 0))
    return pl.pallas_call(
        functools.partial(_pool_kernel, ts=ts, gd=gd),
        out_shape=jax.ShapeDtypeStruct((bsz, seq, d), F32),
        grid=(bsz, seq // ts),
        in_specs=[tile, _const_spec((d, d)), _const_spec((len(POOL_WINDOWS), gd, gd)), _const_spec((1, d)),
                  _const_spec((d, d)), _const_spec((1, d)), _const_spec((1, d))],
        out_specs=tile,
        scratch_shapes=[pltpu.VMEM((POOL_HALO + ts, d), F32)],
        compiler_params=_params(2),
        name="pool_mixer_ln",
    )(h, w_in.astype(BF16), w_grp.astype(BF16), scale.reshape(1, d), w_out.astype(BF16),
      ln_g.reshape(1, d), ln_b.reshape(1, d))


def _sigmoid(v):
    return 1.0 / (1.0 + jnp.exp(-v))


def _ssd_body(x, wz_ref, wx_ref, wb_ref, wc_ref, wdt_ref, convw_ref, convb_ref, dtb_ref, alog_ref,
              dexp_ref, normg_ref, expand_ref, wout_ref, g_ref, b_ref, o_ref,
              xbc_scr, z_scr, actx_scr, bmat_scr, cmat_scr, y_scr, state_scr, *, ts, q, d_inner, gn, per_group=None):
    s = pl.program_id(1)
    conv_dim = d_inner + 2 * gn
    gw = d_inner // SSM_N_GROUPS

    @pl.when(s == 0)
    def _():
        xbc_scr[:, 0:CONV_HALO, :] = jnp.zeros((conv_dim // LANES, CONV_HALO, LANES), F32)
        state_scr[...] = jnp.zeros(state_scr.shape, F32)

    xb = x.astype(BF16)
    dt_raw = _dot(xb, wdt_ref[...]) + dtb_ref[...]
    dtv = jnp.maximum(dt_raw, 0.0) + jnp.log1p(jnp.exp(-jnp.abs(dt_raw)))
    a_all = dtv * (-jnp.exp(alog_ref[...]))

    for w_ref, base in ((wb_ref, d_inner), (wc_ref, d_inner + gn), (wx_ref, 0)):
        for c in range(0, w_ref.shape[1], SSD_STRIP):
            res = _dot(xb, w_ref[:, c:c + SSD_STRIP])
            for jj in range(SSD_STRIP // LANES):
                j = (base + c) // LANES + jj
                cs = slice(j * LANES, (j + 1) * LANES)
                xbc_scr[j, CONV_HALO:, :] = res[:, jj * LANES:(jj + 1) * LANES]
                acc = convb_ref[:, cs]
                for k in range(SSM_CONV):
                    r0 = CONV_HALO - (SSM_CONV - 1) + k
                    acc = acc + convw_ref[k:k + 1, cs] * xbc_scr[j, r0:r0 + ts, :]
                act = acc * _sigmoid(acc)
                col = j * LANES
                if col < d_inner:
                    actx_scr[:, cs] = act
                elif col < d_inner + gn:
                    bmat_scr[:, col - d_inner:col - d_inner + LANES] = act.astype(BF16)
                else:
                    cmat_scr[:, col - d_inner - gn:col - d_inner - gn + LANES] = act.astype(BF16)
                xbc_scr[j, 0:CONV_HALO, :] = xbc_scr[j, ts:ts + CONV_HALO, :]

    expand = expand_ref[...]
    row_i = lax.broadcasted_iota(jnp.int32, (q, LANES), 0)
    causal = lax.broadcasted_iota(jnp.int32, (q, q), 0) >= lax.broadcasted_iota(jnp.int32, (q, q), 1)
    head_of_lane = lax.broadcasted_iota(jnp.int32, (q, gw), 1) // SSM_HEAD_DIM
    head_mask = [jnp.where(head_of_lane == r, 1.0, 0.0).astype(BF16) for r in range(SSM_HEADS_PER_GROUP)]

    for c in range(ts // q):
        rows = slice(c * q, (c + 1) * q)
        acs = a_all[rows, :]
        sh = 1
        while sh < q:
            acs = acs + jnp.where(row_i >= sh, pltpu.roll(acs, sh, axis=0), 0.0)
            sh *= 2
        acs_t = acs.T
        a_last = acs[q - 1:q, :]
        e_in = _dot(jnp.exp(acs).astype(BF16), expand)
        dec = _dot(jnp.exp(a_last - acs).astype(BF16), expand)
        dtx = _dot(dtv[rows, :].astype(BF16), expand)
        cd = jnp.broadcast_to(jnp.exp(a_last), (SUBLANES, LANES))
        cd_hi = cd.astype(BF16)
        cd_lo = (cd - cd_hi.astype(F32)).astype(BF16)
        cdx = (_dot(cd_hi, expand) + _dot(cd_lo, expand))[0:1, :]
        xd = actx_scr[rows, :] * dtx
        xdd = xd * dec
        xdb = xd.astype(BF16)
        for g in range(SSM_N_GROUPS):
            gc = slice(g * gw, (g + 1) * gw)
            nc = slice(g * SSM_D_STATE, (g + 1) * SSM_D_STATE)
            if c == 0:
                z_scr[:, gc] = _dot(xb, wz_ref[:, gc])
                if per_group is not None:
                    per_group(g)
            bg = bmat_scr[rows, nc]
            cg = cmat_scr[rows, nc]
            cb = _dot_nt(cg, bg)
            xg = xdb[:, gc]
            mhs, xms = [], []
            for r in range(SSM_HEADS_PER_GROUP):
                hd = g * SSM_HEADS_PER_GROUP + r
                seg = acs[:, hd:hd + 1] - acs_t[hd:hd + 1, :]
                lmat = jnp.exp(jnp.where(causal, seg, NEG_BIG))
                mhs.append((cb * lmat).astype(BF16))
                xms.append(xg * head_mask[r])
            yg = (_dot(cg, state_scr[g].astype(BF16)) * e_in[:, gc]
                  + _dot(jnp.concatenate(mhs, axis=1), jnp.concatenate(xms, axis=0)))
            y_scr[rows, gc] = yg
            state_scr[g] = state_scr[g] * cdx[:, gc] + _dot_tn(bg, xdd[:, gc].astype(BF16))

    mix = None
    for g in range(SSM_N_GROUPS):
        gc = slice(g * gw, (g + 1) * gw)
        zz = z_scr[:, gc]
        yv = (y_scr[:, gc] + dexp_ref[:, gc] * actx_scr[:, gc]) * (zz * _sigmoid(zz))
        yv = yv * lax.rsqrt(jnp.mean(yv * yv, -1, keepdims=True) + SSM_NORM_EPS) * normg_ref[:, gc]
        part = _dot(yv.astype(BF16), wout_ref[gc, :])
        mix = part if mix is None else mix + part
    o_ref[...] = _layer_norm(DEEPNORM_ALPHA * x + mix, g_ref[...], b_ref[...])


def _ssd_kernel(x_ref, *refs, **static):
    _ssd_body(x_ref[...], *refs, **static)


def _ssd_combine_kernel(dest_ref, h_ref, route_ref, g3_ref, b3_ref, y_ref, *refs, ts, **static):
    *ssd_refs, gbuf, gsem = refs
    i = pl.program_id(0) * pl.num_programs(1) + pl.program_id(1)
    n_tiles = pl.num_programs(0) * pl.num_programs(1)
    total = n_tiles * ts
    cur = i % 2
    nxt = 1 - cur
    nxt_tile = jnp.minimum(i + 1, n_tiles - 1)

    def start_rows(tile, slot, lo, hi):
        for tok in range(lo, hi):
            for k in range(TOP_K):
                row = dest_ref[k * total + tile * ts + tok]
                pltpu.make_async_copy(y_ref.at[pl.ds(pl.multiple_of(row * SUBLANES, SUBLANES), SUBLANES)],
                                      gbuf.at[slot, k, pl.ds(tok * SUBLANES, SUBLANES)],
                                      gsem.at[slot]).start(priority=k % 2)

    def drain(slot):
        for k in range(TOP_K):
            pltpu.make_async_copy(y_ref.at[pl.ds(0, ts * SUBLANES)], gbuf.at[slot, k], gsem.at[slot]).wait()

    @pl.when(i == 0)
    def _():
        def body(j, carry):
            for u in range(ROWS_PER_ISSUE):
                tok = j * ROWS_PER_ISSUE + u
                for k in range(TOP_K):
                    row = dest_ref[k * total + tok]
                    _row_copy(y_ref, row, gbuf.at[0, k], tok, gsem.at[0]).start(priority=k % 2)
            return carry
        lax.fori_loop(0, ts // ROWS_PER_ISSUE, body, 0)

    drain(cur)
    parts = []
    for c in range(SUBLANES):
        acc = None
        for k in range(TOP_K):
            piece = gbuf[cur, k, pl.ds(c, ts, stride=SUBLANES), :] * route_ref[:, TOP_K + k:TOP_K + k + 1]
            acc = piece if acc is None else acc + piece
        parts.append(acc)
    ff = jnp.concatenate(parts, axis=1)
    x = _layer_norm(DEEPNORM_ALPHA * h_ref[...] + ff, g3_ref[...], b3_ref[...])
    per = ts // SSM_N_GROUPS
    _ssd_body(x, *ssd_refs, ts=ts, per_group=lambda g: start_rows(nxt_tile, nxt, g * per, (g + 1) * per), **static)

    @pl.when(i == n_tiles - 1)
    def _():
        drain(nxt)


def _ssd_layer(h, w_in, conv_w, conv_b, dt_bias, a_log, d_skip, norm_g, w_out, ln_g, ln_b, pending=None):
    bsz, seq, d = h.shape
    n_heads = a_log.shape[0]
    d_inner = n_heads * SSM_HEAD_DIM
    gn = SSM_N_GROUPS * SSM_D_STATE
    conv_dim = d_inner + 2 * gn
    ts, q = SSD_TILE, SSD_CHUNK
    w_in = w_in.astype(BF16)
    wz = w_in[:, :d_inner]
    wx = w_in[:, d_inner:2 * d_inner]
    wb = w_in[:, 2 * d_inner:2 * d_inner + gn]
    wc = w_in[:, 2 * d_inner + gn:2 * d_inner + 2 * gn]
    pad = LANES - n_heads
    wdt = jnp.pad(w_in[:, d_inner + conv_dim:], ((0, 0), (0, pad)))
    dtb = jnp.pad(dt_bias.astype(F32), (0, pad)).reshape(1, LANES)
    alog = jnp.pad(a_log.astype(F32), (0, pad)).reshape(1, LANES)
    dexp = jnp.repeat(d_skip.astype(F32), SSM_HEAD_DIM).reshape(1, d_inner)
    expand = (jnp.arange(LANES)[:, None] == (jnp.arange(d_inner)[None, :] // SSM_HEAD_DIM)).astype(BF16)
    weight_specs = [_const_spec((d, d_inner)), _const_spec((d, d_inner)), _const_spec((d, gn)),
                    _const_spec((d, gn)), _const_spec((d, LANES)), _const_spec((SSM_CONV, conv_dim)),
                    _const_spec((1, conv_dim)), _const_spec((1, LANES)), _const_spec((1, LANES)),
                    _const_spec((1, d_inner)), _const_spec((1, d_inner)), _const_spec((LANES, d_inner)),
                    _const_spec((d_inner, d)), _const_spec((1, d)), _const_spec((1, d))]
    weights = (wz, wx, wb, wc, wdt, conv_w.astype(F32), conv_b.reshape(1, conv_dim), dtb, alog, dexp,
               norm_g.reshape(1, d_inner), expand, w_out.astype(BF16), ln_g.reshape(1, d), ln_b.reshape(1, d))
    scratch = [pltpu.VMEM((conv_dim // LANES, CONV_HALO + ts, LANES), F32), pltpu.VMEM((ts, d_inner), F32),
               pltpu.VMEM((ts, d_inner), F32), pltpu.VMEM((ts, gn), BF16), pltpu.VMEM((ts, gn), BF16),
               pltpu.VMEM((ts, d_inner), F32),
               pltpu.VMEM((SSM_N_GROUPS, SSM_D_STATE, d_inner // SSM_N_GROUPS), F32)]
    static = dict(ts=ts, q=q, d_inner=d_inner, gn=gn)
    out_shape = jax.ShapeDtypeStruct((bsz, seq, d), F32)
    if pending is None:
        tile = pl.BlockSpec((None, ts, d), lambda b, s: (b, s, 0))
        return pl.pallas_call(
            functools.partial(_ssd_kernel, **static),
            out_shape=out_shape,
            grid=(bsz, seq // ts),
            in_specs=[tile] + weight_specs,
            out_specs=tile,
            scratch_shapes=scratch,
            compiler_params=_params(2),
            name="ssd_mixer_ln",
        )(h, *weights)
    y, dest, route, g3, b3 = pending
    tile = pl.BlockSpec((None, ts, d), lambda b, s, dest: (b, s, 0))
    grid_spec = pltpu.PrefetchScalarGridSpec(
        num_scalar_prefetch=1,
        grid=(bsz, seq // ts),
        in_specs=[tile, pl.BlockSpec((None, ts, LANES), lambda b, s, dest: (b, s, 0)), _const_spec((1, d)),
                  _const_spec((1, d)), pl.BlockSpec(memory_space=pl.ANY)] + weight_specs,
        out_specs=tile,
        scratch_shapes=scratch + [pltpu.VMEM((2, TOP_K, ts * SUBLANES, LANES), F32), pltpu.SemaphoreType.DMA((2,))],
    )
    return pl.pallas_call(
        functools.partial(_ssd_combine_kernel, **static),
        out_shape=out_shape,
        grid_spec=grid_spec,
        compiler_params=_params(2),
        name="combine_ln_ssd_mixer_ln",
    )(dest, h, route, g3.reshape(1, d), b3.reshape(1, d), y, *weights)


def _xattn_kernel(h_ref, mem_ref, wq_ref, wk_ref, wv_ref, wo_ref, g_ref, b_ref, wrh_ref, wrl_ref, br_ref,
                  o_ref, route_ref, route_t_ref, cnt_ref, k_scr, v_scr, carry_scr, *, ts, hd):
    b = pl.program_id(0)
    s = pl.program_id(1)

    @pl.when(s == 0)
    def _():
        mb = mem_ref[...].astype(BF16)
        k_scr[...] = _dot(mb, wk_ref[...]).astype(BF16)
        v_scr[...] = _dot(mb, wv_ref[...]).astype(BF16)

    @pl.when((b == 0) & (s == 0))
    def _():
        carry_scr[...] = jnp.zeros(carry_scr.shape, F32)

    sub = ts // XATTN_SPLIT
    halves = []
    for u in range(XATTN_SPLIT):
        rows = slice(u * sub, (u + 1) * sub)
        h = h_ref[rows, :]
        qv = (_dot(h.astype(BF16), wq_ref[...]) * (hd ** -0.5)).astype(BF16)
        heads = []
        for hh in range(XA_HEADS):
            cols = slice(hh * hd, (hh + 1) * hd)
            sc = _dot_nt(qv[:, cols], k_scr[:, cols])
            p = jnp.exp(sc - jnp.max(sc, -1, keepdims=True))
            o = _dot(p.astype(BF16), v_scr[:, cols]) / jnp.sum(p, -1, keepdims=True)
            heads.append(o.astype(BF16))
        halves.append((h, _dot(jnp.concatenate(heads, axis=1), wo_ref[...])))

    lane = lax.broadcasted_iota(jnp.int32, (sub, LANES), 1).astype(F32)
    below = (lax.broadcasted_iota(jnp.int32, (sub, sub), 0) > lax.broadcasted_iota(jnp.int32, (sub, sub), 1))
    below = jnp.where(below, 1.0, 0.0).astype(BF16)
    for u in range(XATTN_SPLIT):
        rows = slice(u * sub, (u + 1) * sub)
        h, xa = halves[u]
        h2 = _layer_norm(DEEPNORM_ALPHA * h + xa, g_ref[...], b_ref[...])
        o_ref[rows, :] = h2
        h2_hi = h2.astype(BF16)

        h2_lo = (h2 - h2_hi.astype(F32)).astype(BF16)
        logits = _dot(h2_hi, wrh_ref[...]) + _dot(h2_lo, wrh_ref[...]) + _dot(h2_hi, wrl_ref[...]) + br_ref[...]
        work = logits
        vals, idxs, sels = [], [], []
        for _ in range(TOP_K):
            m = jnp.max(work, -1, keepdims=True)
            ik = jnp.min(jnp.where(work == m, lane, float(LANES)), -1, keepdims=True)
            sel = lane == ik
            vals.append(m)
            idxs.append(ik)
            sels.append(sel)
            work = jnp.where(sel, -jnp.inf, work)
        exps = [jnp.exp(v - vals[0]) for v in vals]
        den = exps[0]
        for e in exps[1:]:
            den = den + e
        onehot = jnp.zeros((sub, LANES), F32)
        for sel in sels:
            onehot = onehot + sel.astype(F32)
        before = _dot(below, onehot.astype(BF16)) + carry_scr[0:1, :]
        route = jnp.zeros((sub, LANES), F32)
        for k in range(TOP_K):
            rank = jnp.sum(jnp.where(sels[k], before, 0.0), -1, keepdims=True)
            route = jnp.where(lane == float(k), idxs[k], route)
            route = jnp.where(lane == float(TOP_K + k), exps[k] / den, route)
            route = jnp.where(lane == float(2 * TOP_K + k), rank, route)
        route_ref[rows, :] = route
        route_t_ref[:, rows] = route.T[0:ROUTE_ROWS, :]
        carry_scr[...] = carry_scr[...] + jnp.sum(onehot, 0, keepdims=True)
    cnt_ref[...] = carry_scr[...]


def _xattn_layer(h, mem, wq, wk, wv, wo, ln_g, ln_b, w_router, b_router):
    bsz, seq, d = h.shape
    mlen = mem.shape[1]
    ts = XATTN_TILE
    hd = d // XA_HEADS
    n_exp = w_router.shape[1]
    wr = jnp.pad(w_router.astype(F32), ((0, 0), (0, LANES - n_exp)))
    wr_hi = wr.astype(BF16)
    wr_lo = (wr - wr_hi.astype(F32)).astype(BF16)
    br = jnp.pad(b_router.astype(F32), (0, LANES - n_exp), constant_values=-jnp.inf).reshape(1, LANES)
    tile = pl.BlockSpec((None, ts, d), lambda b, s: (b, s, 0))
    rtile = pl.BlockSpec((None, ts, LANES), lambda b, s: (b, s, 0))
    return pl.pallas_call(
        functools.partial(_xattn_kernel, ts=ts, hd=hd),
        out_shape=(jax.ShapeDtypeStruct((bsz, seq, d), F32), jax.ShapeDtypeStruct((bsz, seq, LANES), F32),
                   jax.ShapeDtypeStruct((ROUTE_ROWS, bsz * seq), F32), jax.ShapeDtypeStruct((SUBLANES, LANES), F32)),
        grid=(bsz, seq // ts),
        in_specs=[tile, pl.BlockSpec((None, mlen, d), lambda b, s: (b, 0, 0)),
                  _const_spec((d, d)), _const_spec((d, d)), _const_spec((d, d)), _const_spec((d, d)),
                  _const_spec((1, d)), _const_spec((1, d)), _const_spec((d, LANES)), _const_spec((d, LANES)),
                  _const_spec((1, LANES))],
        out_specs=(tile, rtile, pl.BlockSpec((ROUTE_ROWS, ts), lambda b, s: (0, b * (seq // ts) + s)),
                   pl.BlockSpec((SUBLANES, LANES), lambda b, s: (0, 0))),
        scratch_shapes=[pltpu.VMEM((mlen, d), BF16), pltpu.VMEM((mlen, d), BF16), pltpu.VMEM((SUBLANES, LANES), F32)],
        compiler_params=_params(2),
        name="xattn_ln_router",
    )(h, mem, wq.astype(BF16), wk.astype(BF16), wv.astype(BF16), wo.astype(BF16),
      ln_g.reshape(1, d), ln_b.reshape(1, d), wr_hi, wr_lo, br)


def _rows_from_tiles(ref, n):
    return jnp.concatenate([ref[pl.ds(c, n, stride=SUBLANES), :] for c in range(SUBLANES)], axis=1)


def _rows_to_tiles(ref, v, n):
    for c in range(SUBLANES):
        ref[pl.ds(c, n, stride=SUBLANES), :] = v[:, c * LANES:(c + 1) * LANES]


def _moe_kernel(be_ref, nu_ref, nv_ref, p0_ref, tok_ref, h_ref, wgu_ref, bg_ref, bu_ref, wd_ref, bd_ref, o_ref,
                wg_scr, wu_scr, wd_scr, xbuf, xsem, *, tm):
    i = pl.program_id(0)
    nu = nu_ref[0]
    active = i < nu
    new_expert = (i == 0) | (be_ref[i] != be_ref[jnp.maximum(i - 1, 0)])
    slot = i % 2

    def drain(s):
        pltpu.make_async_copy(h_ref.at[pl.ds(0, tm * SUBLANES)], xbuf.at[s], xsem.at[s]).wait()

    @pl.when(i == 0)
    def _():
        def body(j, carry):
            for u in range(ROWS_PER_ISSUE):
                row = j * ROWS_PER_ISSUE + u
                tok = tok_ref[p0_ref[0] + jnp.minimum(row, nv_ref[0] - 1)]
                _row_copy(h_ref, tok, xbuf.at[0], row, xsem.at[0]).start(priority=u % 2)
            return carry
        lax.fori_loop(0, tm // ROWS_PER_ISSUE, body, 0)

    @pl.when(active & new_expert)
    def _():
        w2 = 2 * LANES
        src = lax.broadcasted_iota(jnp.int32, (w2, w2), 0)
        dst = lax.broadcasted_iota(jnp.int32, (w2, w2), 1)
        perm = jnp.where(src == jnp.where(dst < LANES, 2 * dst, 2 * (dst - LANES) + 1), 1.0, 0.0).astype(BF16)
        for c in range(wgu_ref.shape[1] // w2):
            res = _dot(wgu_ref[:, c * w2:(c + 1) * w2].astype(BF16), perm)
            wg_scr[:, c * LANES:(c + 1) * LANES] = res[:, :LANES].astype(BF16)
            wu_scr[:, c * LANES:(c + 1) * LANES] = res[:, LANES:].astype(BF16)
        wd_scr[...] = wd_ref[...].astype(BF16)

    nv = nv_ref[i]
    nxt = jnp.minimum(i + 1, nu - 1)
    nxt_p0 = p0_ref[nxt]
    nxt_last = nv_ref[nxt] - 1
    for r in range(MOE_ROW_GROUP, tm + 1, MOE_ROW_GROUP):
        @pl.when(active & (nv > r - MOE_ROW_GROUP) & (nv <= r))
        def _():
            drain(slot)
            x = jnp.concatenate([xbuf[slot, pl.ds(c, r, stride=SUBLANES), :] for c in range(SUBLANES)],
                                axis=1).astype(BF16)
            for row in range(tm):
                tok = tok_ref[nxt_p0 + jnp.minimum(row, nxt_last)]
                pltpu.make_async_copy(h_ref.at[pl.ds(pl.multiple_of(tok * SUBLANES, SUBLANES), SUBLANES)],
                                      xbuf.at[1 - slot, pl.ds(row * SUBLANES, SUBLANES)],
                                      xsem.at[1 - slot]).start(priority=row % 2)
            gate = jnp.minimum(_dot(x, wg_scr[...]) + bg_ref[...], SWIGLU_LIMIT)
            up = jnp.clip(_dot(x, wu_scr[...]) + bu_ref[...], -SWIGLU_LIMIT, SWIGLU_LIMIT)
            act = (up + 1.0) * (gate * _sigmoid(SWIGLU_ALPHA * gate))
            _rows_to_tiles(o_ref, _dot(act.astype(BF16), wd_scr[...]) + bd_ref[...], r)
            if r < tm:
                o_ref[r * SUBLANES:, :] = jnp.zeros(((tm - r) * SUBLANES, LANES), F32)

            @pl.when(i == nu - 1)
            def _():
                drain(1 - slot)

    @pl.when(jnp.logical_not(active))
    def _():
        o_ref[...] = jnp.zeros(o_ref.shape, F32)


def _moe_experts(h_rows, src_tok, block_e, n_used, n_valid, block_p0, n_blocks, layer, w_gate_up, b_gate, b_up, w_down,
                 b_down):
    _, n_exp, f, d = w_down.shape
    tm = MOE_TILE

    def w_map(i, be, nu, nv, p0, tok):
        return (be[i], 0, 0)

    def lw_map(i, be, nu, nv, p0, tok):
        return (layer, be[i], 0, 0)

    grid_spec = pltpu.PrefetchScalarGridSpec(
        num_scalar_prefetch=5,
        grid=(n_blocks,),
        in_specs=[pl.BlockSpec(memory_space=pl.ANY),
                  pl.BlockSpec((None, None, d, 2 * f), lw_map),
                  pl.BlockSpec((None, 1, f), w_map), pl.BlockSpec((None, 1, f), w_map),
                  pl.BlockSpec((None, None, f, d), lw_map), pl.BlockSpec((None, 1, d), w_map)],
        out_specs=pl.BlockSpec((tm * SUBLANES, LANES), lambda i, be, nu, nv, p0, tok: (i, 0)),
        scratch_shapes=[pltpu.VMEM((d, f), BF16), pltpu.VMEM((d, f), BF16), pltpu.VMEM((f, d), BF16),
                        pltpu.VMEM((2, tm * SUBLANES, LANES), F32), pltpu.SemaphoreType.DMA((2,))],
    )
    return pl.pallas_call(
        functools.partial(_moe_kernel, tm=tm),
        out_shape=jax.ShapeDtypeStruct((n_blocks * tm * SUBLANES, LANES), F32),
        grid_spec=grid_spec,
        compiler_params=_params(1),
        name="moe_gather_experts",
    )(block_e, n_used, n_valid, block_p0, src_tok, h_rows, w_gate_up, b_gate.reshape(n_exp, 1, f),
      b_up.reshape(n_exp, 1, f), w_down, b_down.reshape(n_exp, 1, d))


def _row_copy(src, src_row, dst, dst_row, sem):
    return pltpu.make_async_copy(src.at[pl.ds(pl.multiple_of(src_row * SUBLANES, SUBLANES), SUBLANES)],
                                 dst.at[pl.ds(pl.multiple_of(dst_row * SUBLANES, SUBLANES), SUBLANES)], sem)


def _dispatch_kernel(dest_ref, zblk_ref, h_ref, xs_ref, stage0, stage1, zero_scr, sem, zsem, *, ts, n_tiles, tm):
    i = pl.program_id(0)

    n_fill = zblk_ref.shape[0] // 2

    def fill_wait(lo, hi, s):
        for j in range(lo, hi):
            @pl.when(zblk_ref[j] >= 0)
            def _():
                pltpu.make_async_copy(zero_scr, xs_ref.at[pl.ds(0, tm * SUBLANES)], zsem.at[s]).wait()

    @pl.when(i == 0)
    def _():
        zero_scr[...] = jnp.zeros(zero_scr.shape, F32)
        for j in range(2 * n_fill):
            @pl.when(zblk_ref[j] >= 0)
            def _():
                start = pl.multiple_of(zblk_ref[j] * (tm * SUBLANES), tm * SUBLANES)
                pltpu.make_async_copy(zero_scr, xs_ref.at[pl.ds(start, tm * SUBLANES)], zsem.at[j // n_fill]).start()
        fill_wait(0, n_fill, 0)

    @pl.when(i == n_tiles - 1)
    def _():
        fill_wait(n_fill, 2 * n_fill, 1)

    def drain(stage, s):
        for _ in range(TOP_K):
            pltpu.make_async_copy(stage, xs_ref.at[pl.ds(0, ts * SUBLANES)], sem.at[s]).wait()

    def step(stage, s, other, so):
        @pl.when(i >= 2)
        def _():
            drain(stage, s)

        _rows_to_tiles(stage, h_ref[...], ts)

        def issue(j, carry):
            for u in range(ROWS_PER_ISSUE):
                tok = j * ROWS_PER_ISSUE + u
                for k in range(TOP_K):
                    row = dest_ref[k * (n_tiles * ts) + i * ts + tok]
                    _row_copy(stage, tok, xs_ref, row, sem.at[s]).start(priority=k % 2)
            return carry

        lax.fori_loop(0, ts // ROWS_PER_ISSUE, issue, 0)

        @pl.when(i == n_tiles - 1)
        def _():
            drain(other, so)
            drain(stage, s)

    @pl.when(i % 2 == 0)
    def _():
        step(stage0, 0, stage1, 1)

    @pl.when(i % 2 == 1)
    def _():
        step(stage1, 1, stage0, 0)


def _dispatch_rows(h, dest, zero_blocks, n_rows):
    t, d = h.shape
    ts = DISPATCH_TILE
    tm = MOE_TILE
    n_tiles = t // ts
    assert d == SUBLANES * LANES and n_tiles >= 2
    grid_spec = pltpu.PrefetchScalarGridSpec(
        num_scalar_prefetch=2,
        grid=(n_tiles,),
        in_specs=[pl.BlockSpec((ts, d), lambda i, dest, zb: (i, 0))],
        out_specs=pl.BlockSpec(memory_space=pl.ANY),
        scratch_shapes=[pltpu.VMEM((ts * SUBLANES, LANES), F32), pltpu.VMEM((ts * SUBLANES, LANES), F32),
                        pltpu.VMEM((tm * SUBLANES, LANES), F32),
                        pltpu.SemaphoreType.DMA((2,)), pltpu.SemaphoreType.DMA((2,))],
    )
    return pl.pallas_call(
        functools.partial(_dispatch_kernel, ts=ts, n_tiles=n_tiles, tm=tm),
        out_shape=jax.ShapeDtypeStruct((n_rows * SUBLANES, LANES), F32),
        grid_spec=grid_spec,
        compiler_params=_params(1),
        name="moe_dispatch",
    )(dest, zero_blocks, h)


def _gather_combine(i, n_tiles, ts, dest_ref, h_ref, route_ref, g_ref, b_ref, y_ref, o_ref, gbuf0, gbuf1, sem):
    total = n_tiles * ts

    def issue(tile, gbuf, s):
        def body(j, carry):
            for u in range(ROWS_PER_ISSUE):
                tok = j * ROWS_PER_ISSUE + u
                for k in range(TOP_K):
                    row = dest_ref[k * total + tile * ts + tok]
                    _row_copy(y_ref, row, gbuf.at[k], tok, sem.at[s]).start(priority=k % 2)
            return carry

        lax.fori_loop(0, ts // ROWS_PER_ISSUE, body, 0)

    def drain(gbuf, s):
        for k in range(TOP_K):
            pltpu.make_async_copy(y_ref.at[pl.ds(0, ts * SUBLANES)], gbuf.at[k], sem.at[s]).wait()

    def step(gbuf, s, nxt, sn):
        @pl.when(i + 1 < n_tiles)
        def _():
            issue(i + 1, nxt, sn)

        drain(gbuf, s)
        parts = []
        for c in range(SUBLANES):
            acc = None
            for k in range(TOP_K):
                piece = gbuf[k, pl.ds(c, ts, stride=SUBLANES), :] * route_ref[:, TOP_K + k:TOP_K + k + 1]
                acc = piece if acc is None else acc + piece
            parts.append(acc)
        ff = jnp.concatenate(parts, axis=1)
        o_ref[...] = _layer_norm(DEEPNORM_ALPHA * h_ref[...] + ff, g_ref[...], b_ref[...])

    @pl.when(i == 0)
    def _():
        issue(0, gbuf0, 0)

    @pl.when(i % 2 == 0)
    def _():
        step(gbuf0, 0, gbuf1, 1)

    @pl.when(i % 2 == 1)
    def _():
        step(gbuf1, 1, gbuf0, 0)


def _combine_kernel(dest_ref, h_ref, route_ref, g_ref, b_ref, y_ref, o_ref, gbuf0, gbuf1, sem, *, ts, n_tiles):
    _gather_combine(pl.program_id(0), n_tiles, ts, dest_ref, h_ref, route_ref, g_ref, b_ref, y_ref, o_ref, gbuf0, gbuf1,
                    sem)


def _combine_layer(h, y, dest, route, ln_g, ln_b):
    t, d = h.shape
    ts = COMBINE_TILE
    n_tiles = t // ts
    grid_spec = pltpu.PrefetchScalarGridSpec(
        num_scalar_prefetch=1,
        grid=(n_tiles,),
        in_specs=[pl.BlockSpec((ts, d), lambda i, dest: (i, 0)), pl.BlockSpec((ts, LANES), lambda i, dest: (i, 0)),
                  pl.BlockSpec((1, d), lambda i, dest: (0, 0)), pl.BlockSpec((1, d), lambda i, dest: (0, 0)),
                  pl.BlockSpec(memory_space=pl.ANY)],
        out_specs=pl.BlockSpec((ts, d), lambda i, dest: (i, 0)),
        scratch_shapes=[pltpu.VMEM((TOP_K, ts * SUBLANES, LANES), F32),
                        pltpu.VMEM((TOP_K, ts * SUBLANES, LANES), F32), pltpu.SemaphoreType.DMA((2,))],
    )
    return pl.pallas_call(
        functools.partial(_combine_kernel, ts=ts, n_tiles=n_tiles),
        out_shape=jax.ShapeDtypeStruct((t, d), F32),
        grid_spec=grid_spec,
        compiler_params=_params(1),
        name="moe_combine_ln",
    )(dest, h, route, ln_g.reshape(1, d), ln_b.reshape(1, d), y)


def _moe_layer(h2, route, route_t, cnt, layer, w_gate_up, b_gate_up, w_down, b_down, ln_g, ln_b, defer_combine):
    bsz, seq, d = h2.shape
    t = bsz * seq
    tm = MOE_TILE
    n_exp = w_gate_up.shape[1]
    h2 = h2.reshape(t, d)
    route = route.reshape(t, LANES)
    idx = route_t[0:TOP_K].astype(jnp.int32)
    rank = route_t[2 * TOP_K:3 * TOP_K].astype(jnp.int32)
    counts = cnt[0, :n_exp].astype(jnp.int32)
    padded = (counts + tm - 1) // tm * tm
    pend = jnp.cumsum(padded)
    pstart = pend - padded
    part = counts % tm
    skip = jnp.where(part > 0, tm - part, 0)

    def per_expert(table, e):
        return jnp.sum(jnp.where(e[..., None] == jnp.arange(n_exp, dtype=jnp.int32), table, 0), axis=-1)

    dest = (per_expert(pstart, idx) + rank
            + jnp.where(rank >= per_expert(part, idx), per_expert(skip, idx), 0)).reshape(-1)
    n_blocks = (t * TOP_K) // tm + n_exp
    n_used = (pend[-1] // tm).astype(jnp.int32)
    blk = jnp.arange(n_blocks, dtype=jnp.int32)
    block_e = jnp.sum((pend[None, :] <= (blk * tm)[:, None]).astype(jnp.int32), axis=1)
    block_e = jnp.minimum(block_e, n_exp - 1)
    block_e = jnp.where(blk < n_used, block_e, block_e[jnp.maximum(n_used - 1, 0)])
    first_blk = jnp.where(part > 0, pstart // tm, -1)
    n_valid = jnp.where((blk == per_expert(first_blk, block_e)) & (blk < n_used), per_expert(part, block_e),
                        tm).astype(jnp.int32)
    n_valid = jnp.where(blk < n_used, n_valid, 0)
    block_p0 = (jnp.cumsum(n_valid) - n_valid).astype(jnp.int32)
    _, order = lax.sort((dest, jnp.arange(t * TOP_K, dtype=jnp.int32)), num_keys=1)
    src_tok = order % t
    y = _moe_experts(h2.reshape(t * SUBLANES, LANES), src_tok, block_e, n_used.reshape(1), n_valid, block_p0, n_blocks,
                     layer, w_gate_up, b_gate_up[:, 0::2], b_gate_up[:, 1::2], w_down, b_down)
    if defer_combine:
        return h2.reshape(bsz, seq, d), (y, dest, route.reshape(bsz, seq, LANES), ln_g, ln_b)
    return _combine_layer(h2, y, dest, route, ln_g, ln_b).reshape(bsz, seq, d), None


def kernel(x, mem, pool_w_in, pool_w_grp, pool_scale, pool_w_out, ssm_w_in, ssm_conv_w, ssm_conv_b, ssm_dt_bias, ssm_a_log, ssm_d, ssm_norm_g, ssm_w_out, xa_wq, xa_wk, xa_wv, xa_wo, moe_w_router, moe_b_router, moe_w_gate_up, moe_b_gate_up, moe_w_down, moe_b_down, ln_mix_g, ln_mix_b, ln_xa_g, ln_xa_b, ln_ffn_g, ln_ffn_b):
    h = x
    pending = None
    for i in range(DEPTH):
        j = i // 2
        if i % 2 == 0:
            assert pending is None
            h = _pool_layer(h, pool_w_in[j], pool_w_grp[j], pool_scale[j], pool_w_out[j], ln_mix_g[i], ln_mix_b[i])
        else:
            h = _ssd_layer(h, ssm_w_in[j], ssm_conv_w[j], ssm_conv_b[j], ssm_dt_bias[j], ssm_a_log[j], ssm_d[j],
                           ssm_norm_g[j], ssm_w_out[j], ln_mix_g[i], ln_mix_b[i], pending)
        h2, route, route_t, cnt = _xattn_layer(h, mem, xa_wq[i], xa_wk[i], xa_wv[i], xa_wo[i], ln_xa_g[i], ln_xa_b[i],
                                      moe_w_router[i], moe_b_router[i])
        next_is_ssd = i + 1 < DEPTH and (i + 1) % 2 == 1
        h, pending = _moe_layer(h2, route, route_t, cnt, i, moe_w_gate_up, moe_b_gate_up[i], moe_w_down, moe_b_down[i],
                                ln_ffn_g[i], ln_ffn_b[i], next_is_ssd)
    return h
```
